```python
import jax, jax.numpy as jnp
from jax import lax
import numpy as np

D_MODEL = 2048
BATCH = 4
SEQ = 2048
DEPTH = 4

EPS = 1e-6
HEAD_DIM = 64
A_HEADS = D_MODEL // (2 * HEAD_DIM)
A_KV_HEADS = A_HEADS // 4
WINDOW = 128
ROPE_THETA = 10000.0
B_GROUP_DIM = 128
B_GROUPS = D_MODEL // (2 * B_GROUP_DIM)
B_CHUNK = 128
A_Q = A_HEADS * HEAD_DIM
A_KV = A_KV_HEADS * HEAD_DIM
B_W = B_GROUPS * B_GROUP_DIM
AB_IN = A_Q + 2 * A_KV + 2 * B_W
AB_MIX = A_Q + B_W
C_KDIM = 128
C_HEADS = D_MODEL // C_KDIM
C_VDIM = D_MODEL // C_HEADS
C_K = C_HEADS * C_KDIM
C_V = C_HEADS * C_VDIM
C_IN = 2 * C_K + 2 * C_V
C_CHUNK = 64
N_GROUPS = 4
EXPERTS_PER_GROUP = 4
N_EXPERTS = N_GROUPS * EXPERTS_PER_GROUP
TOP_K = 2
D_EXPERT = D_MODEL // 4
N_A_LAYERS = (DEPTH + 1) // 2
N_C_LAYERS = DEPTH // 2

kernel_name = "hybrid_swa_gmlp_hgrn2_grouped_moe"

F32 = jnp.float32


def rms_norm(x, g):
    xf = x.astype(F32)
    y = xf * lax.rsqrt(jnp.mean(xf * xf, axis=-1, keepdims=True) + EPS)
    return (y * g.astype(F32)).astype(x.dtype)


def layer_norm_gain(x, g):
    xf = x.astype(F32)
    mu = jnp.mean(xf, axis=-1, keepdims=True)
    xc = xf - mu
    y = xc * lax.rsqrt(jnp.mean(xc * xc, axis=-1, keepdims=True) + EPS)
    return (y * g.astype(F32)).astype(x.dtype)


def modulate(h, shift, scale):
    return h * (1 + scale[:, None, :]) + shift[:, None, :]


def rope(x, positions):
    half = x.shape[-1] // 2
    inv_freq = ROPE_THETA ** (-jnp.arange(half, dtype=F32) / half)
    ang = positions.astype(F32)[..., None] * inv_freq
    cos = jnp.cos(ang)[:, :, None, :]
    sin = jnp.sin(ang)[:, :, None, :]
    xf = x.astype(F32)
    x1, x2 = xf[..., :half], xf[..., half:]
    out = jnp.concatenate([x1 * cos - x2 * sin, x2 * cos + x1 * sin], axis=-1)
    return out.astype(x.dtype)


def sliding_window_attention(q, k, v, sinks):
    bsz, s_len, n_h, hd = q.shape
    nb = s_len // WINDOW
    rep = n_h // A_KV_HEADS
    qb = q.reshape(bsz, nb, WINDOW, A_KV_HEADS, rep, hd)

    def band(t):
        tb = t.reshape(bsz, nb, WINDOW, A_KV_HEADS, hd)
        prev = jnp.pad(tb, ((0, 0), (1, 0), (0, 0), (0, 0), (0, 0)))[:, :-1]
        return jnp.concatenate([prev, tb], axis=2)

    kb, vb = band(k), band(v)
    s = jnp.einsum('bnqgrd,bnkgd->bngrqk', qb, kb).astype(F32) * (hd ** -0.5)
    qi = jnp.arange(WINDOW)[:, None]
    kj = jnp.arange(2 * WINDOW)[None, :]
    diff = qi + WINDOW - kj
    band_mask = (diff >= 0) & (diff < WINDOW)
    key_pos = jnp.arange(nb)[:, None] * WINDOW + jnp.arange(2 * WINDOW)[None, :] - WINDOW
    mask = band_mask[None] & (key_pos >= 0)[:, None, :]
    s = jnp.where(mask[None, :, None, None], s, -jnp.inf)
    sink = sinks.astype(F32).reshape(A_KV_HEADS, rep)[None, None, :, :, None, None]
    m = jnp.maximum(jnp.max(s, axis=-1, keepdims=True), sink)
    p = jnp.exp(s - m)
    p = (p / (jnp.sum(p, axis=-1, keepdims=True) + jnp.exp(sink - m))).astype(v.dtype)
    o = jnp.einsum('bngrqk,bnkgd->bnqgrd', p, vb)
    return o.reshape(bsz, s_len, n_h * hd)


def chunked_spatial_gating(u, v, vnorm_g, w_s, b_s):
    bsz, s_len, _ = v.shape
    nc = s_len // B_CHUNK
    vn = layer_norm_gain(v, vnorm_g).reshape(bsz, nc, B_CHUNK, B_GROUPS, B_GROUP_DIM)
    causal = jnp.tril(jnp.ones((B_CHUNK, B_CHUNK), dtype=bool))
    w = jnp.where(causal[None], w_s, jnp.zeros_like(w_s))
    sv = jnp.einsum('gts,bnsgd->bntgd', w, vn) + b_s.T[None, None, :, :, None]
    return u * sv.reshape(bsz, s_len, B_W)


def hgrn2_chunkwise(q, f_logit, inp, lb):
    bsz, s_len, _ = q.shape
    nc = s_len // C_CHUNK
    lbf = lb.astype(F32)
    f = lbf + (1 - lbf) * jax.nn.sigmoid(f_logit.astype(F32))
    k = 1 - f
    logf = jnp.log(f)

    def to_chunks(t, d):
        return t.astype(F32).reshape(bsz, nc, C_CHUNK, C_HEADS, d).transpose(1, 0, 3, 2, 4)

    qc, kc, lfc = to_chunks(q, C_KDIM), to_chunks(k, C_KDIM), to_chunks(logf, C_KDIM)
    vc = to_chunks(inp, C_VDIM)
    causal = jnp.tril(jnp.ones((C_CHUNK, C_CHUNK), dtype=bool))[:, :, None]

    def step(state, chunk):
        qh, kh, lf, vh = chunk
        b = jnp.cumsum(lf, axis=2)
        o_inter = jnp.einsum('bhtk,bhkv->bhtv', qh * jnp.exp(b), state)
        rel = b[:, :, :, None, :] - b[:, :, None, :, :]
        decay = jnp.exp(jnp.where(causal, rel, -jnp.inf))
        scores = jnp.einsum('bhtsk,bhsk->bhts', qh[:, :, :, None, :] * decay, kh)
        o = o_inter + jnp.einsum('bhts,bhsv->bhtv', scores, vh)
        b_last = b[:, :, -1:, :]
        new_state = jnp.exp(b_last[:, :, 0, :])[..., None] * state + \
            jnp.einsum('bhsk,bhsv->bhkv', kh * jnp.exp(b_last - b), vh)
        return new_state, o

    s0 = jnp.zeros((bsz, C_HEADS, C_KDIM, C_VDIM), F32)
    _, o = lax.scan(step, s0, (qc, kc, lfc, vc))
    return o.transpose(1, 0, 3, 2, 4).reshape(bsz, s_len, C_HEADS, C_VDIM)


def grouped_moe(h, router_w, router_bias, w_gate, w_up, w_down):
    bsz, s_len, d = h.shape
    t = h.reshape(-1, d)
    n_tok = t.shape[0]
    scores = jax.nn.sigmoid(jnp.dot(t, router_w).astype(F32))
    sel = scores + router_bias.astype(F32)
    grp_top = lax.top_k(sel.reshape(n_tok, N_GROUPS, EXPERTS_PER_GROUP), TOP_K)[0]
    _, g_idx = lax.top_k(jnp.sum(grp_top, axis=-1), 1)
    gmask = jnp.repeat(jax.nn.one_hot(g_idx[:, 0], N_GROUPS, dtype=F32), EXPERTS_PER_GROUP, axis=1) > 0
    _, e_idx = lax.top_k(jnp.where(gmask, sel, -jnp.inf), TOP_K)
    w_sel = jnp.take_along_axis(scores, e_idx, axis=1)
    w_sel = w_sel / jnp.sum(w_sel, axis=-1, keepdims=True)
    gates = jnp.sum(jax.nn.one_hot(e_idx, N_EXPERTS, dtype=F32) * w_sel[..., None], axis=1)
    a = jnp.einsum('td,edf->tef', t, w_gate)
    u = jnp.einsum('td,edf->tef', t, w_up)
    hid = jax.nn.silu(a) * u * gates[..., None].astype(t.dtype)
    y = jnp.einsum('tef,efd->td', hid, w_down)
    return y.reshape(bsz, s_len, d)


def setup_inputs(seed: int = 0) -> dict:
    key = jax.random.key(seed)
    ks = iter(jax.random.split(key, 32))

    def nrm(shape, scale):
        return jax.random.normal(next(ks), shape, F32) * scale

    x = nrm((BATCH, SEQ, D_MODEL), 1.0)
    c = nrm((BATCH, D_MODEL), 1.0)
    start = jax.random.randint(next(ks), (BATCH, 1), 0, 4096, dtype=jnp.int32)
    positions = start + jnp.arange(SEQ, dtype=jnp.int32)[None, :]
    return {
        "x": x,
        "c": c,
        "positions": positions,
        "ada_w": nrm((DEPTH, D_MODEL, 6 * D_MODEL), 0.5 * D_MODEL ** -0.5),
        "ada_b": nrm((DEPTH, 6 * D_MODEL), 0.02),
        "norm1_g": 1.0 + nrm((DEPTH, D_MODEL), 0.05),
        "norm2_g": 1.0 + nrm((DEPTH, D_MODEL), 0.05),
        "ab_w_in": nrm((N_A_LAYERS, D_MODEL, AB_IN), D_MODEL ** -0.5),
        "q_norm_g": 1.0 + nrm((N_A_LAYERS, HEAD_DIM), 0.05),
        "k_norm_g": 1.0 + nrm((N_A_LAYERS, HEAD_DIM), 0.05),
        "sinks": nrm((N_A_LAYERS, A_HEADS), 1.0),
        "gm_vnorm_g": 1.0 + nrm((N_A_LAYERS, B_W), 0.05),
        "gm_ws": nrm((N_A_LAYERS, B_GROUPS, B_CHUNK, B_CHUNK), B_CHUNK ** -0.5),
        "gm_b": 1.0 + nrm((N_A_LAYERS, B_GROUPS, B_CHUNK), 0.1),
        "ab_w_out": nrm((N_A_LAYERS, AB_MIX, D_MODEL), AB_MIX ** -0.5),
        "c_w_in": nrm((N_C_LAYERS, D_MODEL, C_IN), D_MODEL ** -0.5),
        "c_lower_bounds": nrm((DEPTH, C_K), 0.1),
        "c_onorm_g": 1.0 + nrm((N_C_LAYERS, C_VDIM), 0.05),
        "c_w_out": nrm((N_C_LAYERS, C_V, D_MODEL), C_V ** -0.5),
        "router_w": nrm((D_MODEL, N_EXPERTS), D_MODEL ** -0.5),
        "router_bias": nrm((N_EXPERTS,), 0.01),
        "moe_w_gate": nrm((DEPTH, N_EXPERTS, D_MODEL, D_EXPERT), D_MODEL ** -0.5),
        "moe_w_up": nrm((DEPTH, N_EXPERTS, D_MODEL, D_EXPERT), D_MODEL ** -0.5),
        "moe_w_down": nrm((DEPTH, N_EXPERTS, D_EXPERT, D_MODEL), D_EXPERT ** -0.5),
    }


def reference(x, c, positions, ada_w, ada_b, norm1_g, norm2_g, ab_w_in, q_norm_g, k_norm_g,
              sinks, gm_vnorm_g, gm_ws, gm_b, ab_w_out, c_w_in, c_lower_bounds, c_onorm_g,
              c_w_out, router_w, router_bias, moe_w_gate, moe_w_up, moe_w_down):
    bsz, s_len, _ = x.shape
    sm = jax.nn.softmax(c_lower_bounds.astype(F32), axis=0)
    lower_bounds = jnp.cumsum(sm, axis=0) - sm[0:1]
    cond = jax.nn.silu(c)
    for l in range(DEPTH):
        mod = jnp.dot(cond, ada_w[l]) + ada_b[l]
        sh1, sc1, g1, sh2, sc2, g2 = jnp.split(mod, 6, axis=-1)
        h = modulate(rms_norm(x, norm1_g[l]), sh1, sc1)
        if l % 2 == 0:
            i = l // 2
            proj = jnp.dot(h, ab_w_in[i])
            q, k, v, u_b, v_b = jnp.split(
                proj, [A_Q, A_Q + A_KV, A_Q + 2 * A_KV, A_Q + 2 * A_KV + B_W], axis=-1)
            q = q.reshape(bsz, s_len, A_HEADS, HEAD_DIM)
            k = k.reshape(bsz, s_len, A_KV_HEADS, HEAD_DIM)
            v = v.reshape(bsz, s_len, A_KV_HEADS, HEAD_DIM)
            q = rope(rms_norm(q, q_norm_g[i]), positions)
            k = rope(rms_norm(k, k_norm_g[i]), positions)
            o_a = sliding_window_attention(q, k, v, sinks[i])
            o_b = chunked_spatial_gating(jax.nn.gelu(u_b), jax.nn.gelu(v_b),
                                         gm_vnorm_g[i], gm_ws[i], gm_b[i])
            y = jnp.dot(jnp.concatenate([o_a, o_b], axis=-1), ab_w_out[i])
        else:
            j = l // 2
            proj = jnp.dot(h, c_w_in[j])
            q, f_logit, inp, gate = jnp.split(proj, [C_K, 2 * C_K, 2 * C_K + C_V], axis=-1)
            o = hgrn2_chunkwise(q, f_logit, inp, lower_bounds[l]).astype(h.dtype)
            o = rms_norm(o, c_onorm_g[j]) * jax.nn.silu(gate.reshape(bsz, s_len, C_HEADS, C_VDIM))
            y = jnp.dot(o.reshape(bsz, s_len, C_V), c_w_out[j])
        x = x + g1[:, None, :] * y
        h = modulate(rms_norm(x, norm2_g[l]), sh2, sc2)
        x = x + g2[:, None, :] * grouped_moe(h, router_w, router_bias,
                                             moe_w_gate[l], moe_w_up[l], moe_w_down[l])
    return x
```

```python
import functools

import jax
import jax.numpy as jnp
from jax import lax
from jax.experimental import pallas as pl
from jax.experimental.pallas import tpu as pltpu

F32 = jnp.float32
BF16 = jnp.bfloat16

EPS = 1e-6
HEAD_DIM = 64
KV_REP = 4
WINDOW = 128
ROPE_THETA = 10000.0
B_GROUP_DIM = 128
B_CHUNK = 128
C_KDIM = 128
C_VDIM = 128
C_CHUNK = 64
N_GROUPS = 4
EXPERTS_PER_GROUP = 4
N_EXPERTS = N_GROUPS * EXPERTS_PER_GROUP
TOP_K = 2

LANES = 128
SUBLANES = 8
VMEM_LIMIT = 56 * 1024 * 1024

ROW_TILE = 512
EXPERT_TILE = 256


def _params(sem):
    return pltpu.CompilerParams(dimension_semantics=sem, vmem_limit_bytes=VMEM_LIMIT)


def _ada_kernel(c_ref, w_ref, b_ref, o_ref):
    c = c_ref[...]
    cond = c * jax.nn.sigmoid(c)
    o_ref[0] = jnp.dot(cond, w_ref[0], preferred_element_type=F32) + b_ref[0]


def _ada_mod(c_pad, ada_w, ada_b):
    depth, d, n = ada_w.shape
    tn = 1536
    rows = c_pad.shape[0]
    return pl.pallas_call(
        _ada_kernel,
        out_shape=jax.ShapeDtypeStruct((depth, rows, n), F32),
        grid=(depth, n // tn),
        in_specs=[
            pl.BlockSpec((rows, d), lambda l, j: (0, 0)),
            pl.BlockSpec((1, d, tn), lambda l, j: (l, 0, j)),
            pl.BlockSpec((1, 1, tn), lambda l, j: (l, 0, j)),
        ],
        out_specs=pl.BlockSpec((1, rows, tn), lambda l, j: (l, 0, j)),
        compiler_params=_params(("arbitrary", "arbitrary")),
        name="ada_mod",
    )(c_pad, ada_w, ada_b.reshape(depth, 1, n))


def _rope_table_kernel(pos_ref, freq_ref, sign_ref, cos_ref, sin_ref):
    ang = pos_ref[...].astype(F32) * freq_ref[...]
    cos_ref[...] = jnp.cos(ang)
    sin_ref[...] = jnp.sin(ang) * sign_ref[...]


def _rope_tables(positions):
    t = positions.size
    half = HEAD_DIM // 2
    inv_freq = ROPE_THETA ** (-jnp.arange(half, dtype=F32) / half)
    freq = jnp.tile(inv_freq, LANES // half).reshape(1, LANES)
    sign = jnp.tile(jnp.concatenate([-jnp.ones((half,), F32), jnp.ones((half,), F32)]),
                    LANES // HEAD_DIM).reshape(1, LANES)
    tm = min(1024, t)
    return pl.pallas_call(
        _rope_table_kernel,
        out_shape=(jax.ShapeDtypeStruct((t, LANES), F32), jax.ShapeDtypeStruct((t, LANES), F32)),
        grid=(t // tm,),
        in_specs=[
            pl.BlockSpec((tm, 1), lambda i: (i, 0)),
            pl.BlockSpec((1, LANES), lambda i: (0, 0)),
            pl.BlockSpec((1, LANES), lambda i: (0, 0)),
        ],
        out_specs=(pl.BlockSpec((tm, LANES), lambda i: (i, 0)),
                   pl.BlockSpec((tm, LANES), lambda i: (i, 0))),
        compiler_params=_params(("arbitrary",)),
        name="rope_tables",
    )(positions.reshape(t, 1), freq, sign)


def _norm_mod(x, g, shift, scale):
    y = x * lax.rsqrt(jnp.mean(x * x, axis=-1, keepdims=True) + EPS)
    return (y * g) * (1 + scale) + shift


def _in_proj_kernel(x_ref, g_ref, sh_ref, sc_ref, w_ref, o_ref, h_scr):
    @pl.when(pl.program_id(1) == 0)
    def _():
        h_scr[...] = _norm_mod(x_ref[...], g_ref[...], sh_ref[0], sc_ref[0]).astype(BF16)

    o_ref[...] = jnp.dot(h_scr[...], w_ref[...], preferred_element_type=F32).astype(o_ref.dtype)


def _in_proj(x, g, shift, scale, w, seq):
    t, d = x.shape
    n = w.shape[1]
    tm, tn = ROW_TILE, 512
    return pl.pallas_call(
        _in_proj_kernel,
        out_shape=jax.ShapeDtypeStruct((t, n), BF16),
        grid=(t // tm, n // tn),
        in_specs=[
            pl.BlockSpec((tm, d), lambda i, j: (i, 0)),
            pl.BlockSpec((1, d), lambda i, j: (0, 0)),
            pl.BlockSpec((1, 1, d), lambda i, j: (i * tm // seq, 0, 0)),
            pl.BlockSpec((1, 1, d), lambda i, j: (i * tm // seq, 0, 0)),
            pl.BlockSpec((d, tn), lambda i, j: (0, j)),
        ],
        out_specs=pl.BlockSpec((tm, tn), lambda i, j: (i, j)),
        scratch_shapes=[pltpu.VMEM((tm, d), BF16)],
        compiler_params=_params(("arbitrary", "arbitrary")),
        name="in_proj",
    )(x, g.reshape(1, d), shift, scale, w)


def _out_proj_kernel(a_ref, w_ref, x_ref, g_ref, o_ref):
    y = jnp.dot(a_ref[...], w_ref[...], preferred_element_type=F32)
    o_ref[...] = x_ref[...] + g_ref[0] * y


def _out_proj(a, w, x, gate, seq):
    t, k = a.shape
    n = w.shape[1]
    tm, tn = ROW_TILE, 512
    return pl.pallas_call(
        _out_proj_kernel,
        out_shape=jax.ShapeDtypeStruct((t, n), F32),
        grid=(t // tm, n // tn),
        in_specs=[
            pl.BlockSpec((tm, k), lambda i, j: (i, 0)),
            pl.BlockSpec((k, tn), lambda i, j: (0, j)),
            pl.BlockSpec((tm, tn), lambda i, j: (i, j)),
            pl.BlockSpec((1, 1, tn), lambda i, j: (i * tm // seq, 0, j)),
        ],
        out_specs=pl.BlockSpec((tm, tn), lambda i, j: (i, j)),
        compiler_params=_params(("arbitrary", "arbitrary")),
        name="out_proj",
    )(a, w, x, gate)


def _segment_mean_matrix():
    r = lax.broadcasted_iota(jnp.int32, (LANES, LANES), 0) // HEAD_DIM
    c = lax.broadcasted_iota(jnp.int32, (LANES, LANES), 1) // HEAD_DIM
    return jnp.where(r == c, 1.0 / HEAD_DIM, 0.0).astype(F32)


def _head_norm_rope(x, gain, cos, sin, seg_mean, low_half):
    ms = jnp.dot(x * x, seg_mean, preferred_element_type=F32)
    y = x * lax.rsqrt(ms + EPS) * gain
    partner = jnp.where(low_half, pltpu.roll(y, LANES - HEAD_DIM // 2, 1),
                        pltpu.roll(y, HEAD_DIM // 2, 1))
    return y * cos + partner * sin


def _gelu_tanh(x):
    return 0.5 * x * (1.0 + jnp.tanh(0.7978845608028654 * (x + 0.044715 * (x * x * x))))


def _mix_kernel(sinks_ref,
                q0_ref, q1_ref, kv_ref, u0_ref, u1_ref, v0_ref, v1_ref,
                cos_ref, sin_ref, qg_ref, kg_ref, vng_ref, ws_ref, bs_ref,
                o_ref, kd_scr, vd_scr):
    n = pl.program_id(1)
    w = WINDOW
    n_kv = kd_scr.shape[0]

    @pl.when(n == 0)
    def _():
        kd_scr[...] = jnp.zeros_like(kd_scr)
        vd_scr[...] = jnp.zeros_like(vd_scr)

    lane = lax.broadcasted_iota(jnp.int32, (w, LANES), 1)
    low_half = (lane % HEAD_DIM) < (HEAD_DIM // 2)
    first_head = lane < HEAD_DIM
    seg_mean = _segment_mean_matrix()
    cos = cos_ref[...]
    sin = sin_ref[...]

    kv = kv_ref[...].astype(F32)
    for c in range(n_kv // 2):
        kc = _head_norm_rope(kv[:, c * LANES:(c + 1) * LANES], kg_ref[...], cos, sin, seg_mean, low_half)
        vc = kv[:, (n_kv // 2 + c) * LANES:(n_kv // 2 + c + 1) * LANES]
        kc_sw = pltpu.roll(kc, HEAD_DIM, 1)
        vc_sw = pltpu.roll(vc, HEAD_DIM, 1)
        kd_scr[2 * c, w:, :] = jnp.where(first_head, kc, kc_sw).astype(BF16)
        kd_scr[2 * c + 1, w:, :] = jnp.where(first_head, kc_sw, kc).astype(BF16)
        vd_scr[2 * c, w:, :] = jnp.where(first_head, vc, vc_sw).astype(BF16)
        vd_scr[2 * c + 1, w:, :] = jnp.where(first_head, vc_sw, vc).astype(BF16)

    qi = lax.broadcasted_iota(jnp.int32, (w, 2 * w), 0)
    kj = lax.broadcasted_iota(jnp.int32, (w, 2 * w), 1)
    diff = qi + w - kj
    band = (diff >= 0) & (diff < w) & ((kj >= w) | (n > 0))
    band4 = jnp.concatenate([band] * KV_REP, axis=0)

    scale = HEAD_DIM ** -0.5
    q_refs = (q0_ref, q1_ref)
    chunks_per_ref = q0_ref.shape[1] // LANES
    for g in range(n_kv):
        rows = []
        for cc in range(2):
            c = 2 * g + cc
            qc = q_refs[c // chunks_per_ref][:, (c % chunks_per_ref) * LANES:(c % chunks_per_ref + 1) * LANES]
            qc = _head_norm_rope(qc.astype(F32), qg_ref[...], cos, sin, seg_mean, low_half)
            rows.append(jnp.where(first_head, qc, 0.0))
            rows.append(jnp.where(first_head, 0.0, qc))
        qm = jnp.concatenate(rows, axis=0).astype(BF16)
        s = lax.dot_general(qm, kd_scr[g], (((1,), (1,)), ((), ())),
                            preferred_element_type=F32) * scale
        s = jnp.where(band4, s, -jnp.inf)
        sink = jnp.concatenate(
            [jnp.full((w, 1), sinks_ref[KV_REP * g + r], F32) for r in range(KV_REP)], axis=0)
        m = jnp.maximum(jnp.max(s, axis=-1, keepdims=True), sink)
        p = jnp.exp(s - m)
        p = p / (jnp.sum(p, axis=-1, keepdims=True) + jnp.exp(sink - m))
        o = jnp.dot(p.astype(BF16), vd_scr[g], preferred_element_type=F32)
        for cc in range(2):
            oc = jnp.where(first_head, o[(2 * cc) * w:(2 * cc + 1) * w], o[(2 * cc + 1) * w:(2 * cc + 2) * w])
            c = 2 * g + cc
            o_ref[:, c * LANES:(c + 1) * LANES] = oc.astype(o_ref.dtype)

    kd_scr[:, :w, :] = kd_scr[:, w:, :]
    vd_scr[:, :w, :] = vd_scr[:, w:, :]

    a_q = 2 * q0_ref.shape[1]
    u = _gelu_tanh(jnp.concatenate([u0_ref[...], u1_ref[...]], axis=1).astype(F32))
    v = _gelu_tanh(jnp.concatenate([v0_ref[...], v1_ref[...]], axis=1).astype(F32))
    mu = jnp.mean(v, axis=-1, keepdims=True)
    vc = v - mu
    vn = (vc * lax.rsqrt(jnp.mean(vc * vc, axis=-1, keepdims=True) + EPS) * vng_ref[...]).astype(BF16)
    ti = lax.broadcasted_iota(jnp.int32, (w, w), 0)
    si = lax.broadcasted_iota(jnp.int32, (w, w), 1)
    causal = si <= ti
    bs = bs_ref[...]
    for g in range(ws_ref.shape[0]):
        wg = jnp.where(causal, ws_ref[g], 0.0).astype(BF16)
        sv = jnp.dot(wg, vn[:, g * LANES:(g + 1) * LANES], preferred_element_type=F32) + bs[:, g:g + 1]
        o_ref[:, a_q + g * LANES:a_q + (g + 1) * LANES] = (u[:, g * LANES:(g + 1) * LANES] * sv).astype(o_ref.dtype)


def _mix_even(proj, cos_t, sin_t, q_g, k_g, sinks, vnorm_g, ws, bs, bsz, seq):
    t = proj.shape[0]
    w = WINDOW
    nb = seq // w
    heads = sinks.shape[0]
    n_kv = heads // KV_REP
    a_q = heads * HEAD_DIM
    a_kv = n_kv * HEAD_DIM
    b_w = vnorm_g.shape[0]
    cw = 512
    assert a_q == 2 * cw and 2 * a_kv == cw and b_w == 2 * cw
    row = lambda b, n: b * nb + n
    gain2 = lambda g: jnp.tile(g, LANES // HEAD_DIM).reshape(1, LANES)
    col = lambda j: pl.BlockSpec((w, cw), lambda b, n, s: (row(b, n), j))
    grid_spec = pltpu.PrefetchScalarGridSpec(
        num_scalar_prefetch=1,
        grid=(bsz, nb),
        in_specs=[
            col(0), col(1), col(2), col(3), col(4), col(5), col(6),
            pl.BlockSpec((w, LANES), lambda b, n, s: (row(b, n), 0)),
            pl.BlockSpec((w, LANES), lambda b, n, s: (row(b, n), 0)),
            pl.BlockSpec((1, LANES), lambda b, n, s: (0, 0)),
            pl.BlockSpec((1, LANES), lambda b, n, s: (0, 0)),
            pl.BlockSpec((1, b_w), lambda b, n, s: (0, 0)),
            pl.BlockSpec(ws.shape, lambda b, n, s: (0, 0, 0)),
            pl.BlockSpec((w, ws.shape[0]), lambda b, n, s: (0, 0)),
        ],
        out_specs=pl.BlockSpec((w, a_q + b_w), lambda b, n, s: (row(b, n), 0)),
        scratch_shapes=[pltpu.VMEM((n_kv, 2 * w, LANES), BF16),
                        pltpu.VMEM((n_kv, 2 * w, LANES), BF16)],
    )
    return pl.pallas_call(
        _mix_kernel,
        out_shape=jax.ShapeDtypeStruct((t, a_q + b_w), BF16),
        grid_spec=grid_spec,
        compiler_params=_params(("arbitrary", "arbitrary")),
        name="mix_even",
    )(sinks, proj, proj, proj, proj, proj, proj, proj, cos_t, sin_t,
      gain2(q_g), gain2(k_g), vnorm_g.reshape(1, b_w), ws, bs.T)


def _split3(x):
    hi = x.astype(BF16)
    r1 = x - hi.astype(F32)
    mid = r1.astype(BF16)
    lo = (r1 - mid.astype(F32)).astype(BF16)
    return hi, mid, lo


def _prefix_matrix():
    c = C_CHUNK
    levels = c.bit_length() - 1
    t = lax.broadcasted_iota(jnp.int32, (c, c), 0)
    j = lax.broadcasted_iota(jnp.int32, (c, c), 1)
    mats = [j <= t]
    for l in range(levels):
        m = 1 << l
        mid = (t // (2 * m)) * (2 * m) + m
        mats.append(j <= mid - 1)
    return jnp.concatenate(mats, axis=0).astype(BF16)


def _hgrn_kernel(q_ref, f_ref, i_ref, gt_ref, lb_ref, og_ref, o_ref, state_scr):
    c = C_CHUNK
    levels = c.bit_length() - 1
    nc = q_ref.shape[0] // c
    lb = lb_ref[...]
    og = og_ref[...]
    pm = _prefix_matrix()
    t_row = lax.broadcasted_iota(jnp.int32, (c, LANES), 0)
    tt = lax.broadcasted_iota(jnp.int32, (c, c), 0)
    ss = lax.broadcasted_iota(jnp.int32, (c, c), 1)
    state_scr[...] = jnp.zeros_like(state_scr)

    def chunk(ci, carry):
        r0 = pl.multiple_of(ci * c, c)
        q = q_ref[pl.ds(r0, c), :].astype(F32)
        fl = f_ref[pl.ds(r0, c), :].astype(F32)
        v = i_ref[pl.ds(r0, c), :]
        gate = gt_ref[pl.ds(r0, c), :].astype(F32)
        f = lb + (1 - lb) * jax.nn.sigmoid(fl)
        k = 1 - f
        lf = jnp.log(f)
        hi, mid, lo = _split3(lf)
        pref = (jnp.dot(pm, hi, preferred_element_type=F32)
                + jnp.dot(pm, mid, preferred_element_type=F32)
                + jnp.dot(pm, lo, preferred_element_type=F32))
        b = pref[:c]
        scores = jnp.where(tt == ss, jnp.sum(q * k, axis=-1, keepdims=True), 0.0)
        for l in range(levels):
            m = 1 << l
            ref = pref[(l + 1) * c:(l + 2) * c]
            upper = (t_row % (2 * m)) >= m
            a_l = jnp.where(upper, q * jnp.exp(jnp.minimum(b - ref, 0.0)), 0.0).astype(BF16)
            b_l = jnp.where(upper, 0.0, k * jnp.exp(jnp.minimum(ref - b, 0.0))).astype(BF16)
            s_l = lax.dot_general(a_l, b_l, (((1,), (1,)), ((), ())), preferred_element_type=F32)
            scores = scores + jnp.where((tt // (2 * m)) == (ss // (2 * m)), s_l, 0.0)
        state_t = state_scr[...]
        o = lax.dot_general((q * jnp.exp(b)).astype(BF16), state_t.astype(BF16),
                            (((1,), (1,)), ((), ())), preferred_element_type=F32)
        o = o + jnp.dot(scores.astype(BF16), v, preferred_element_type=F32)
        b_last = b[c - 1:c, :]
        kd = (k * jnp.exp(b_last - b)).astype(BF16)
        state_scr[...] = state_t * jnp.exp(b_last) + lax.dot_general(
            v, kd, (((0,), (0,)), ((), ())), preferred_element_type=F32)
        y = o * lax.rsqrt(jnp.mean(o * o, axis=-1, keepdims=True) + EPS) * og
        o_ref[pl.ds(r0, c), :] = (y * (gate * jax.nn.sigmoid(gate))).astype(o_ref.dtype)
        return carry

    lax.fori_loop(0, nc, chunk, 0)


def _hgrn(proj, lb, og, bsz, seq):
    t, n4 = proj.shape
    heads = n4 // (4 * LANES)
    return pl.pallas_call(
        _hgrn_kernel,
        out_shape=jax.ShapeDtypeStruct((t, heads * C_VDIM), BF16),
        grid=(bsz, heads),
        in_specs=[
            pl.BlockSpec((seq, LANES), lambda b, h: (b, h)),
            pl.BlockSpec((seq, LANES), lambda b, h: (b, heads + h)),
            pl.BlockSpec((seq, LANES), lambda b, h: (b, 2 * heads + h)),
            pl.BlockSpec((seq, LANES), lambda b, h: (b, 3 * heads + h)),
            pl.BlockSpec((1, LANES), lambda b, h: (0, h)),
            pl.BlockSpec((1, LANES), lambda b, h: (0, 0)),
        ],
        out_specs=pl.BlockSpec((seq, LANES), lambda b, h: (b, h)),
        scratch_shapes=[pltpu.VMEM((C_VDIM, C_KDIM), F32)],
        compiler_params=_params(("arbitrary", "arbitrary")),
        name="hgrn2",
    )(proj, proj, proj, proj, lb.reshape(1, -1), og.reshape(1, LANES))


def _first_max(vals):
    best, idx = vals[0], jnp.zeros(vals[0].shape, jnp.int32)
    for i in range(1, len(vals)):
        better = vals[i] > best
        best = jnp.where(better, vals[i], best)
        idx = jnp.where(better, i, idx)
    return best, idx


def _pick(idx, vals):
    out = vals[-1]
    for i in range(len(vals) - 2, -1, -1):
        out = jnp.where(idx == i, vals[i], out)
    return out


def _router_kernel(x_ref, g_ref, sh_ref, sc_ref, rw_ref, rb_ref,
                   h_ref, eidx_ref, wsel_ref, rank_ref, cnt_ref, carry_scr):
    i = pl.program_id(0)
    tm = x_ref.shape[0]

    @pl.when(i == 0)
    def _():
        carry_scr[...] = jnp.zeros_like(carry_scr)

    h = _norm_mod(x_ref[...], g_ref[...], sh_ref[0], sc_ref[0])
    h_ref[...] = h
    logits = lax.dot_general(rw_ref[...], h, (((1,), (1,)), ((), ())),
                             preferred_element_type=F32, precision=lax.Precision.HIGHEST)
    scores = jax.nn.sigmoid(logits)
    sel = scores + rb_ref[...]
    epg = EXPERTS_PER_GROUP
    sel_rows = [sel[e:e + 1, :] for e in range(N_EXPERTS)]
    sc_rows = [scores[e:e + 1, :] for e in range(N_EXPERTS)]
    gsum = []
    for g in range(N_GROUPS):
        r = sel_rows[g * epg:(g + 1) * epg]
        pair = None
        for a in range(epg):
            for b in range(a + 1, epg):
                pair = r[a] + r[b] if pair is None else jnp.maximum(pair, r[a] + r[b])
        gsum.append(pair)
    _, g_idx = _first_max(gsum)
    cand = [_pick(g_idx, [sel_rows[g * epg + j] for g in range(N_GROUPS)]) for j in range(epg)]
    cand_sc = [_pick(g_idx, [sc_rows[g * epg + j] for g in range(N_GROUPS)]) for j in range(epg)]
    _, i1 = _first_max(cand)
    _, i2 = _first_max([jnp.where(i1 == j, -jnp.inf, cand[j]) for j in range(epg)])
    w1 = _pick(i1, cand_sc)
    w2 = _pick(i2, cand_sc)
    tot = w1 + w2
    e1 = g_idx * epg + i1
    e2 = g_idx * epg + i2
    eidx_ref[0:1, :] = e1
    eidx_ref[1:2, :] = e2
    wsel_ref[0:1, :] = w1 / tot
    wsel_ref[1:2, :] = w2 / tot

    erow = lax.broadcasted_iota(jnp.int32, (N_EXPERTS, tm), 0)
    oh1 = erow == e1
    oh2 = erow == e2
    onehot = jnp.where(oh1 | oh2, 1.0, 0.0)
    si = lax.broadcasted_iota(jnp.int32, (tm, tm), 0)
    ti = lax.broadcasted_iota(jnp.int32, (tm, tm), 1)
    incl = jnp.where(si <= ti, 1.0, 0.0).astype(BF16)
    csum = jnp.dot(onehot.astype(BF16), incl, preferred_element_type=F32)
    carry = carry_scr[...]
    before = csum - onehot + carry[:, 0:1]
    r1 = jnp.sum(jnp.where(oh1, before, 0.0), axis=0, keepdims=True)
    r2 = jnp.sum(jnp.where(oh2, before, 0.0), axis=0, keepdims=True)
    rank_ref[0:1, :] = r1.astype(jnp.int32)
    rank_ref[1:2, :] = r2.astype(jnp.int32)
    new_carry = carry + csum[:, tm - 1:tm]
    carry_scr[...] = new_carry
    cnt_ref[...] = new_carry.astype(jnp.int32)


def _router(x, g, shift, scale, rw_t, rbias, seq):
    t, d = x.shape
    tm = ROW_TILE
    e = rw_t.shape[0]
    row = lambda i: (0, i)
    return pl.pallas_call(
        _router_kernel,
        out_shape=(jax.ShapeDtypeStruct((t, d), F32),
                   jax.ShapeDtypeStruct((TOP_K, t), jnp.int32),
                   jax.ShapeDtypeStruct((TOP_K, t), F32),
                   jax.ShapeDtypeStruct((TOP_K, t), jnp.int32),
                   jax.ShapeDtypeStruct((e, LANES), jnp.int32)),
        grid=(t // tm,),
        in_specs=[
            pl.BlockSpec((tm, d), lambda i: (i, 0)),
            pl.BlockSpec((1, d), lambda i: (0, 0)),
            pl.BlockSpec((1, 1, d), lambda i: (i * tm // seq, 0, 0)),
            pl.BlockSpec((1, 1, d), lambda i: (i * tm // seq, 0, 0)),
            pl.BlockSpec((e, d), lambda i: (0, 0)),
            pl.BlockSpec((e, 1), lambda i: (0, 0)),
        ],
        out_specs=(pl.BlockSpec((tm, d), lambda i: (i, 0)),
                   pl.BlockSpec((TOP_K, tm), row),
                   pl.BlockSpec((TOP_K, tm), row),
                   pl.BlockSpec((TOP_K, tm), row),
                   pl.BlockSpec((e, LANES), lambda i: (0, 0))),
        scratch_shapes=[pltpu.VMEM((e, LANES), F32)],
        compiler_params=_params(("arbitrary",)),
        name="router",
    )(x, g.reshape(1, d), shift, scale, rw_t, rbias.reshape(e, 1))


def _dispatch_kernel(pos_ref, pad_ref, h_ref, xs_ref, zero_scr, sem, zsem):
    i = pl.program_id(0)
    tm = h_ref.shape[0]
    t = pl.num_programs(0) * tm
    tz = zero_scr.shape[0]

    @pl.when(i == 0)
    def _():
        zero_scr[...] = jnp.zeros_like(zero_scr)
        pieces = [tz >> s for s in range(tz.bit_length()) if (tz >> s) >= SUBLANES]

        def fills(e):
            base, length = pad_ref[e], pad_ref[N_EXPERTS + e]
            off = 0
            out = []
            for n in pieces:
                start = pl.multiple_of(base + off, SUBLANES)
                out.append(((length & n) != 0,
                            pltpu.make_async_copy(zero_scr.at[pl.ds(0, n)], xs_ref.at[pl.ds(start, n)], zsem)))
                off = off + (length & n)
            return out

        def tail_fill(k):
            start = pl.multiple_of(pad_ref[2 * N_EXPERTS] + k * tz, tz)
            return pltpu.make_async_copy(zero_scr, xs_ref.at[pl.ds(start, tz)], zsem)

        n_tail = (xs_ref.shape[0] - pad_ref[2 * N_EXPERTS]) // tz
        for e in range(N_EXPERTS):
            for cond, copy in fills(e):
                pl.when(cond)(copy.start)
            pl.when(e < n_tail)(tail_fill(e).start)
        for e in range(N_EXPERTS):
            for cond, copy in fills(e):
                pl.when(cond)(copy.wait)
            pl.when(e < n_tail)(tail_fill(e).wait)

    def row_copy(r, slot):
        p = pos_ref[slot * t + i * tm + r]
        return pltpu.make_async_copy(h_ref.at[pl.ds(r, 1)], xs_ref.at[pl.ds(p, 1)], sem)

    def start(r, carry):
        row_copy(r, 0).start()
        row_copy(r, 1).start()
        return carry

    def wait(r, carry):
        row_copy(r, 0).wait()
        row_copy(r, 1).wait()
        return carry

    lax.fori_loop(0, tm, start, 0)
    lax.fori_loop(0, tm, wait, 0)


def _dispatch(pos_flat, pad_start, h, n_rows):
    t, d = h.shape
    tm = ROW_TILE
    grid_spec = pltpu.PrefetchScalarGridSpec(
        num_scalar_prefetch=2,
        grid=(t // tm,),
        in_specs=[pl.BlockSpec((tm, d), lambda i, p, z: (i, 0))],
        out_specs=pl.BlockSpec(memory_space=pl.ANY),
        scratch_shapes=[pltpu.VMEM((EXPERT_TILE, d), F32),
                        pltpu.SemaphoreType.DMA(()),
                        pltpu.SemaphoreType.DMA(())],
    )
    return pl.pallas_call(
        _dispatch_kernel,
        out_shape=jax.ShapeDtypeStruct((n_rows, d), F32),
        grid_spec=grid_spec,
        compiler_params=_params(("arbitrary",)),
        name="dispatch",
    )(pos_flat, pad_start, h)


def _expert_kernel(te_ref, tv_ref, xs_ref, wg_ref, wu_ref, wd_ref, y_ref):
    i = pl.program_id(0)

    @pl.when(tv_ref[i] > 0)
    def _():
        x = xs_ref[...].astype(BF16)
        a = jnp.dot(x, wg_ref[0].astype(BF16), preferred_element_type=F32)
        u = jnp.dot(x, wu_ref[0].astype(BF16), preferred_element_type=F32)
        hid = (a * jax.nn.sigmoid(a) * u).astype(BF16)
        y_ref[...] = jnp.dot(hid, wd_ref[0].astype(BF16), preferred_element_type=F32)

    @pl.when(tv_ref[i] == 0)
    def _():
        y_ref[...] = jnp.zeros_like(y_ref)


def _experts(tile_expert, tile_valid, xs, w_gate, w_up, w_down):
    p, d = xs.shape
    f = w_gate.shape[2]
    tm = EXPERT_TILE
    grid_spec = pltpu.PrefetchScalarGridSpec(
        num_scalar_prefetch=2,
        grid=(p // tm,),
        in_specs=[
            pl.BlockSpec((tm, d), lambda i, te, tv: (i * tv[i], 0)),
            pl.BlockSpec((1, d, f), lambda i, te, tv: (te[i], 0, 0)),
            pl.BlockSpec((1, d, f), lambda i, te, tv: (te[i], 0, 0)),
            pl.BlockSpec((1, f, d), lambda i, te, tv: (te[i], 0, 0)),
        ],
        out_specs=pl.BlockSpec((tm, d), lambda i, te, tv: (i, 0)),
    )
    return pl.pallas_call(
        _expert_kernel,
        out_shape=jax.ShapeDtypeStruct((p, d), F32),
        grid_spec=grid_spec,
        compiler_params=_params(("arbitrary",)),
        name="experts",
    )(tile_expert, tile_valid, xs, w_gate, w_up, w_down)


def _combine_kernel(pos_ref, x_ref, w_ref, g_ref, y_ref, o_ref, buf, sem):
    i = pl.program_id(0)
    tm = x_ref.shape[0]
    t = pl.num_programs(0) * tm

    def row_copy(r, slot):
        p = pos_ref[slot * t + i * tm + r]
        return pltpu.make_async_copy(y_ref.at[pl.ds(p, 1)], buf.at[slot, pl.ds(r, 1)], sem)

    def start(r, carry):
        row_copy(r, 0).start()
        row_copy(r, 1).start()
        return carry

    def wait(r, carry):
        row_copy(r, 0).wait()
        row_copy(r, 1).wait()
        return carry

    lax.fori_loop(0, tm, start, 0)
    lax.fori_loop(0, tm, wait, 0)
    w = w_ref[...]
    y = w[:, 0:1] * buf[0] + w[:, 1:2] * buf[1]
    o_ref[...] = x_ref[...] + g_ref[0] * y


def _combine(pos_flat, x, wsel_t, gate, y, seq):
    t, d = x.shape
    tm = ROW_TILE
    grid_spec = pltpu.PrefetchScalarGridSpec(
        num_scalar_prefetch=1,
        grid=(t // tm,),
        in_specs=[
            pl.BlockSpec((tm, d), lambda i, p: (i, 0)),
            pl.BlockSpec((tm, TOP_K), lambda i, p: (i, 0)),
            pl.BlockSpec((1, 1, d), lambda i, p: (i * tm // seq, 0, 0)),
            pl.BlockSpec(memory_space=pl.ANY),
        ],
        out_specs=pl.BlockSpec((tm, d), lambda i, p: (i, 0)),
        scratch_shapes=[pltpu.VMEM((TOP_K, tm, d), F32), pltpu.SemaphoreType.DMA(())],
    )
    return pl.pallas_call(
        _combine_kernel,
        out_shape=jax.ShapeDtypeStruct((t, d), F32),
        grid_spec=grid_spec,
        compiler_params=_params(("arbitrary",)),
        name="combine",
    )(pos_flat, x, wsel_t, gate, y)


def _moe(x, g, shift, scale, gate, rw_t, rbias, w_gate, w_up, w_down, seq):
    t, d = x.shape
    te = EXPERT_TILE
    n_rows = TOP_K * t + N_EXPERTS * te
    n_tiles = n_rows // te
    h, eidx, wsel, rank, cnt = _router(x, g, shift, scale, rw_t, rbias, seq)
    counts = cnt[:, 0]
    padded = ((counts + te - 1) // te) * te
    ends = jnp.cumsum(padded)
    starts = ends - padded
    pos = rank
    for e in range(N_EXPERTS):
        pos = pos + jnp.where(eidx == e, starts[e], 0)
    pos_flat = pos.reshape(-1)
    tile_start = jnp.arange(n_tiles, dtype=jnp.int32) * te
    tile_expert = jnp.minimum(jnp.sum(tile_start[:, None] >= ends[None, :], axis=1), N_EXPERTS - 1).astype(jnp.int32)
    tile_valid = (tile_start < ends[-1]).astype(jnp.int32)
    tile_expert = jnp.where(tile_valid > 0, tile_expert, tile_expert[jnp.maximum(ends[-1] // te - 1, 0)])
    pad_base = ((starts + counts) // SUBLANES) * SUBLANES
    pad_meta = jnp.concatenate([pad_base, ends - pad_base, ends[-1:]]).astype(jnp.int32)
    xs = _dispatch(pos_flat, pad_meta, h, n_rows)
    y = _experts(tile_expert, tile_valid, xs, w_gate, w_up, w_down)
    return _combine(pos_flat, x, wsel.T, gate, y, seq)


def kernel(x, c, positions, ada_w, ada_b, norm1_g, norm2_g, ab_w_in, q_norm_g, k_norm_g, sinks, gm_vnorm_g, gm_ws, gm_b, ab_w_out, c_w_in, c_lower_bounds, c_onorm_g, c_w_out, router_w, router_bias, moe_w_gate, moe_w_up, moe_w_down):
    bsz, seq, d = x.shape
    depth = ada_w.shape[0]
    t = bsz * seq
    xt = x.reshape(t, d)

    c_pad = jnp.concatenate([c, jnp.zeros((8 - bsz % 8, d), F32)], axis=0) if bsz % 8 else c
    mod = _ada_mod(c_pad, ada_w, ada_b)[:, :bsz]
    mod = mod.reshape(depth, bsz, 6, 1, d)
    cos_t, sin_t = _rope_tables(positions)

    sm = jax.nn.softmax(c_lower_bounds.astype(F32), axis=0)
    lower_bounds = jnp.cumsum(sm, axis=0) - sm[0:1]
    rw_t = router_w.T

    for l in range(depth):
        sh1, sc1, g1, sh2, sc2, g2 = (mod[l, :, k] for k in range(6))
        if l % 2 == 0:
            i = l // 2
            proj = _in_proj(xt, norm1_g[l], sh1, sc1, ab_w_in[i].astype(BF16), seq)
            mix = _mix_even(proj, cos_t, sin_t, q_norm_g[i], k_norm_g[i], sinks[i],
                            gm_vnorm_g[i], gm_ws[i], gm_b[i], bsz, seq)
            xt = _out_proj(mix, ab_w_out[i].astype(BF16), xt, g1, seq)
        else:
            j = l // 2
            proj = _in_proj(xt, norm1_g[l], sh1, sc1, c_w_in[j].astype(BF16), seq)
            o = _hgrn(proj, lower_bounds[l], c_onorm_g[j], bsz, seq)
            xt = _out_proj(o, c_w_out[j].astype(BF16), xt, g1, seq)
        xt = _moe(xt, norm2_g[l], sh2, sc2, g2, rw_t, router_bias,
                  moe_w_gate[l], moe_w_up[l], moe_w_down[l], seq)
    return xt.reshape(bsz, seq, d)
```

```python
import functools

import jax
import jax.numpy as jnp
from jax import lax
from jax.experimental import pallas as pl
from jax.experimental.pallas import tpu as pltpu

F32 = jnp.float32
BF16 = jnp.bfloat16

EPS = 1e-6
HEAD_DIM = 64
KV_REP = 4
WINDOW = 128
ROPE_THETA = 10000.0
B_GROUP_DIM = 128
B_CHUNK = 128
C_KDIM = 128
C_VDIM = 128
C_CHUNK = 64
N_GROUPS = 4
EXPERTS_PER_GROUP = 4
N_EXPERTS = N_GROUPS * EXPERTS_PER_GROUP
TOP_K = 2

LANES = 128
SUBLANES = 8
VMEM_LIMIT = 56 * 1024 * 1024

ROW_TILE = 512
PROJ_ROW_TILE = 1024
PROJ_COL_TILE = 2048
EXPERT_TILE = 256


def _params(sem):
    return pltpu.CompilerParams(dimension_semantics=sem, vmem_limit_bytes=VMEM_LIMIT)


def _ada_kernel(c_ref, w_ref, b_ref, o_ref):
    c = c_ref[...]
    cond = c * jax.nn.sigmoid(c)
    o_ref[0] = jnp.dot(cond, w_ref[0], preferred_element_type=F32) + b_ref[0]


def _ada_mod(c_pad, ada_w, ada_b):
    depth, d, n = ada_w.shape
    tn = 1536
    rows = c_pad.shape[0]
    return pl.pallas_call(
        _ada_kernel,
        out_shape=jax.ShapeDtypeStruct((depth, rows, n), F32),
        grid=(depth, n // tn),
        in_specs=[
            pl.BlockSpec((rows, d), lambda l, j: (0, 0)),
            pl.BlockSpec((1, d, tn), lambda l, j: (l, 0, j)),
            pl.BlockSpec((1, 1, tn), lambda l, j: (l, 0, j)),
        ],
        out_specs=pl.BlockSpec((1, rows, tn), lambda l, j: (l, 0, j)),
        compiler_params=_params(("arbitrary", "arbitrary")),
        name="ada_mod",
    )(c_pad, ada_w, ada_b.reshape(depth, 1, n))


def _rope_table_kernel(pos_ref, freq_ref, sign_ref, cos_ref, sin_ref):
    ang = pos_ref[...].astype(F32) * freq_ref[...]
    cos_ref[...] = jnp.cos(ang)
    sin_ref[...] = jnp.sin(ang) * sign_ref[...]


def _rope_tables(positions):
    t = positions.size
    half = HEAD_DIM // 2
    inv_freq = ROPE_THETA ** (-jnp.arange(half, dtype=F32) / half)
    freq = jnp.tile(inv_freq, LANES // half).reshape(1, LANES)
    sign = jnp.tile(jnp.concatenate([-jnp.ones((half,), F32), jnp.ones((half,), F32)]),
                    LANES // HEAD_DIM).reshape(1, LANES)
    tm = min(1024, t)
    return pl.pallas_call(
        _rope_table_kernel,
        out_shape=(jax.ShapeDtypeStruct((t, LANES), F32), jax.ShapeDtypeStruct((t, LANES), F32)),
        grid=(t // tm,),
        in_specs=[
            pl.BlockSpec((tm, 1), lambda i: (i, 0)),
            pl.BlockSpec((1, LANES), lambda i: (0, 0)),
            pl.BlockSpec((1, LANES), lambda i: (0, 0)),
        ],
        out_specs=(pl.BlockSpec((tm, LANES), lambda i: (i, 0)),
                   pl.BlockSpec((tm, LANES), lambda i: (i, 0))),
        compiler_params=_params(("arbitrary",)),
        name="rope_tables",
    )(positions.reshape(t, 1), freq, sign)


def _norm_mod(x, g, shift, scale):
    y = x * lax.rsqrt(jnp.mean(x * x, axis=-1, keepdims=True) + EPS)
    return (y * g) * (1 + scale) + shift


def _in_proj_kernel(x_ref, g_ref, sh_ref, sc_ref, w_ref, o_ref, h_scr):
    @pl.when(pl.program_id(1) == 0)
    def _():
        h_scr[...] = _norm_mod(x_ref[...], g_ref[...], sh_ref[0], sc_ref[0]).astype(BF16)

    o_ref[...] = jnp.dot(h_scr[...], w_ref[...], preferred_element_type=F32).astype(o_ref.dtype)


def _in_proj(x, g, shift, scale, w, seq):
    t, d = x.shape
    n = w.shape[1]
    tm = PROJ_ROW_TILE
    tn = max(c for c in range(LANES, PROJ_COL_TILE + 1, LANES) if n % c == 0)
    return pl.pallas_call(
        _in_proj_kernel,
        out_shape=jax.ShapeDtypeStruct((t, n), BF16),
        grid=(t // tm, n // tn),
        in_specs=[
            pl.BlockSpec((tm, d), lambda i, j: (i, 0)),
            pl.BlockSpec((1, d), lambda i, j: (0, 0)),
            pl.BlockSpec((1, 1, d), lambda i, j: (i * tm // seq, 0, 0)),
            pl.BlockSpec((1, 1, d), lambda i, j: (i * tm // seq, 0, 0)),
            pl.BlockSpec((d, tn), lambda i, j: (0, j)),
        ],
        out_specs=pl.BlockSpec((tm, tn), lambda i, j: (i, j)),
        scratch_shapes=[pltpu.VMEM((tm, d), BF16)],
        compiler_params=_params(("arbitrary", "arbitrary")),
        name="in_proj",
    )(x, g.reshape(1, d), shift, scale, w)


def _out_proj_kernel(a_ref, w_ref, x_ref, g_ref, o_ref):
    y = jnp.dot(a_ref[...], w_ref[...], preferred_element_type=F32)
    o_ref[...] = x_ref[...] + g_ref[0] * y


def _out_proj(a, w, x, gate, seq):
    t, k = a.shape
    n = w.shape[1]
    tm, tn = PROJ_ROW_TILE, 1024
    return pl.pallas_call(
        _out_proj_kernel,
        out_shape=jax.ShapeDtypeStruct((t, n), F32),
        grid=(t // tm, n // tn),
        in_specs=[
            pl.BlockSpec((tm, k), lambda i, j: (i, 0)),
            pl.BlockSpec((k, tn), lambda i, j: (0, j)),
            pl.BlockSpec((tm, tn), lambda i, j: (i, j)),
            pl.BlockSpec((1, 1, tn), lambda i, j: (i * tm // seq, 0, j)),
        ],
        out_specs=pl.BlockSpec((tm, tn), lambda i, j: (i, j)),
        compiler_params=_params(("arbitrary", "arbitrary")),
        name="out_proj",
    )(a, w, x, gate)


def _segment_mean_matrix():
    r = lax.broadcasted_iota(jnp.int32, (LANES, LANES), 0) // HEAD_DIM
    c = lax.broadcasted_iota(jnp.int32, (LANES, LANES), 1) // HEAD_DIM
    return jnp.where(r == c, 1.0 / HEAD_DIM, 0.0).astype(F32)


def _head_norm_rope(x, gain, cos, sin, seg_mean, low_half):
    ms = jnp.dot(x * x, seg_mean, preferred_element_type=F32)
    y = x * lax.rsqrt(ms + EPS) * gain
    partner = jnp.where(low_half, pltpu.roll(y, LANES - HEAD_DIM // 2, 1),
                        pltpu.roll(y, HEAD_DIM // 2, 1))
    return y * cos + partner * sin


def _gelu_tanh(x):
    return 0.5 * x * (1.0 + jnp.tanh(0.7978845608028654 * (x + 0.044715 * (x * x * x))))


def _mix_kernel(sinks_ref,
                q0_ref, q1_ref, kv_ref, u0_ref, u1_ref, v0_ref, v1_ref,
                cos_ref, sin_ref, qg_ref, kg_ref, vng_ref, ws_ref, bs_ref,
                o_ref, kd_scr, vd_scr):
    n = pl.program_id(1)
    w = WINDOW
    n_kv = kd_scr.shape[0]

    @pl.when(n == 0)
    def _():
        kd_scr[...] = jnp.zeros_like(kd_scr)
        vd_scr[...] = jnp.zeros_like(vd_scr)

    lane = lax.broadcasted_iota(jnp.int32, (w, LANES), 1)
    low_half = (lane % HEAD_DIM) < (HEAD_DIM // 2)
    first_head = lane < HEAD_DIM
    seg_mean = _segment_mean_matrix()
    cos = cos_ref[...]
    sin = sin_ref[...]

    kv = kv_ref[...].astype(F32)
    for c in range(n_kv // 2):
        kc = _head_norm_rope(kv[:, c * LANES:(c + 1) * LANES], kg_ref[...], cos, sin, seg_mean, low_half)
        vc = kv[:, (n_kv // 2 + c) * LANES:(n_kv // 2 + c + 1) * LANES]
        kc_sw = pltpu.roll(kc, HEAD_DIM, 1)
        vc_sw = pltpu.roll(vc, HEAD_DIM, 1)
        kd_scr[2 * c, w:, :] = jnp.where(first_head, kc, kc_sw).astype(BF16)
        kd_scr[2 * c + 1, w:, :] = jnp.where(first_head, kc_sw, kc).astype(BF16)
        vd_scr[2 * c, w:, :] = jnp.where(first_head, vc, vc_sw).astype(BF16)
        vd_scr[2 * c + 1, w:, :] = jnp.where(first_head, vc_sw, vc).astype(BF16)

    qi = lax.broadcasted_iota(jnp.int32, (w, 2 * w), 0)
    kj = lax.broadcasted_iota(jnp.int32, (w, 2 * w), 1)
    diff = qi + w - kj
    band = (diff >= 0) & (diff < w) & ((kj >= w) | (n > 0))
    band4 = jnp.concatenate([band] * KV_REP, axis=0)

    scale = HEAD_DIM ** -0.5
    q_refs = (q0_ref, q1_ref)
    chunks_per_ref = q0_ref.shape[1] // LANES
    for g in range(n_kv):
        rows = []
        for cc in range(2):
            c = 2 * g + cc
            qc = q_refs[c // chunks_per_ref][:, (c % chunks_per_ref) * LANES:(c % chunks_per_ref + 1) * LANES]
            qc = _head_norm_rope(qc.astype(F32), qg_ref[...], cos, sin, seg_mean, low_half)
            rows.append(jnp.where(first_head, qc, 0.0))
            rows.append(jnp.where(first_head, 0.0, qc))
        qm = jnp.concatenate(rows, axis=0).astype(BF16)
        s = lax.dot_general(qm, kd_scr[g], (((1,), (1,)), ((), ())),
                            preferred_element_type=F32) * scale
        s = jnp.where(band4, s, -jnp.inf)
        sink = jnp.concatenate(
            [jnp.full((w, 1), sinks_ref[KV_REP * g + r], F32) for r in range(KV_REP)], axis=0)
        m = jnp.maximum(jnp.max(s, axis=-1, keepdims=True), sink)
        p = jnp.exp(s - m)
        p = p / (jnp.sum(p, axis=-1, keepdims=True) + jnp.exp(sink - m))
        o = jnp.dot(p.astype(BF16), vd_scr[g], preferred_element_type=F32)
        for cc in range(2):
            oc = jnp.where(first_head, o[(2 * cc) * w:(2 * cc + 1) * w], o[(2 * cc + 1) * w:(2 * cc + 2) * w])
            c = 2 * g + cc
            o_ref[:, c * LANES:(c + 1) * LANES] = oc.astype(o_ref.dtype)

    kd_scr[:, :w, :] = kd_scr[:, w:, :]
    vd_scr[:, :w, :] = vd_scr[:, w:, :]

    a_q = 2 * q0_ref.shape[1]
    u = _gelu_tanh(jnp.concatenate([u0_ref[...], u1_ref[...]], axis=1).astype(F32))
    v = _gelu_tanh(jnp.concatenate([v0_ref[...], v1_ref[...]], axis=1).astype(F32))
    mu = jnp.mean(v, axis=-1, keepdims=True)
    vc = v - mu
    vn = (vc * lax.rsqrt(jnp.mean(vc * vc, axis=-1, keepdims=True) + EPS) * vng_ref[...]).astype(BF16)
    ti = lax.broadcasted_iota(jnp.int32, (w, w), 0)
    si = lax.broadcasted_iota(jnp.int32, (w, w), 1)
    causal = si <= ti
    bs = bs_ref[...]
    for g in range(ws_ref.shape[0]):
        wg = jnp.where(causal, ws_ref[g], 0.0).astype(BF16)
        sv = jnp.dot(wg, vn[:, g * LANES:(g + 1) * LANES], preferred_element_type=F32) + bs[:, g:g + 1]
        o_ref[:, a_q + g * LANES:a_q + (g + 1) * LANES] = (u[:, g * LANES:(g + 1) * LANES] * sv).astype(o_ref.dtype)


def _mix_even(proj, cos_t, sin_t, q_g, k_g, sinks, vnorm_g, ws, bs, bsz, seq):
    t = proj.shape[0]
    w = WINDOW
    nb = seq // w
    heads = sinks.shape[0]
    n_kv = heads // KV_REP
    a_q = heads * HEAD_DIM
    a_kv = n_kv * HEAD_DIM
    b_w = vnorm_g.shape[0]
    cw = 512
    assert a_q == 2 * cw and 2 * a_kv == cw and b_w == 2 * cw
    row = lambda b, n: b * nb + n
    gain2 = lambda g: jnp.tile(g, LANES // HEAD_DIM).reshape(1, LANES)
    col = lambda j: pl.BlockSpec((w, cw), lambda b, n, s: (row(b, n), j))
    grid_spec = pltpu.PrefetchScalarGridSpec(
        num_scalar_prefetch=1,
        grid=(bsz, nb),
        in_specs=[
            col(0), col(1), col(2), col(3), col(4), col(5), col(6),
            pl.BlockSpec((w, LANES), lambda b, n, s: (row(b, n), 0)),
            pl.BlockSpec((w, LANES), lambda b, n, s: (row(b, n), 0)),
            pl.BlockSpec((1, LANES), lambda b, n, s: (0, 0)),
            pl.BlockSpec((1, LANES), lambda b, n, s: (0, 0)),
            pl.BlockSpec((1, b_w), lambda b, n, s: (0, 0)),
            pl.BlockSpec(ws.shape, lambda b, n, s: (0, 0, 0)),
            pl.BlockSpec((w, ws.shape[0]), lambda b, n, s: (0, 0)),
        ],
        out_specs=pl.BlockSpec((w, a_q + b_w), lambda b, n, s: (row(b, n), 0)),
        scratch_shapes=[pltpu.VMEM((n_kv, 2 * w, LANES), BF16),
                        pltpu.VMEM((n_kv, 2 * w, LANES), BF16)],
    )
    return pl.pallas_call(
        _mix_kernel,
        out_shape=jax.ShapeDtypeStruct((t, a_q + b_w), BF16),
        grid_spec=grid_spec,
        compiler_params=_params(("arbitrary", "arbitrary")),
        name="mix_even",
    )(sinks, proj, proj, proj, proj, proj, proj, proj, cos_t, sin_t,
      gain2(q_g), gain2(k_g), vnorm_g.reshape(1, b_w), ws, bs.T)


def _nt_dot(a, b):
    return lax.dot_general(a, b, (((1,), (1,)), ((), ())), preferred_element_type=F32)


def _level_refs(b):
    c = C_CHUNK
    nv = c // SUBLANES
    b3 = b.reshape(nv, SUBLANES, LANES)
    sub = lax.broadcasted_iota(jnp.int32, (nv, SUBLANES, LANES), 1)

    def rows(v, s, n):
        return jnp.broadcast_to(b3[v:v + 1, s:s + 1, :], (n, SUBLANES, LANES))

    refs = []
    for m in (32, 16, 8):
        per = 2 * m // SUBLANES
        refs.append(jnp.concatenate(
            [rows((blk * 2 * m + m) // SUBLANES - 1, SUBLANES - 1, per) for blk in range(c // (2 * m))], axis=0))
    refs.append(jnp.broadcast_to(b3[:, 3:4, :], b3.shape))
    refs.append(jnp.where(sub < 4, jnp.broadcast_to(b3[:, 1:2, :], b3.shape),
                          jnp.broadcast_to(b3[:, 5:6, :], b3.shape)))
    return [r.reshape(c, LANES) for r in refs]


def _hgrn_kernel(q_ref, f_ref, i_ref, gt_ref, lb_ref, og_ref, o_ref,
                 oi_scr, qd_scr, u_scr, d_scr):
    c = C_CHUNK
    nc = q_ref.shape[0] // c
    lb = lb_ref[...]
    og = og_ref[...]
    t_row = lax.broadcasted_iota(jnp.int32, (c, LANES), 0)
    tt = lax.broadcasted_iota(jnp.int32, (c, c), 0)
    ss = lax.broadcasted_iota(jnp.int32, (c, c), 1)
    halves = (32, 16, 8, 4, 2)

    def intra(ci, carry):
        r0 = pl.multiple_of(ci * c, c)
        q = q_ref[pl.ds(r0, c), :].astype(F32)
        fl = f_ref[pl.ds(r0, c), :].astype(F32)
        v = i_ref[pl.ds(r0, c), :]
        f = lb + (1 - lb) * jax.nn.sigmoid(fl)
        k = 1 - f
        b = jnp.log2(f)
        for d in (1, 2, 4, 8, 16, 32):
            b = b + jnp.where(t_row >= d, pltpu.roll(b, d, 0), 0.0)
        scores = jnp.where(tt == ss, jnp.sum(q * k, axis=-1, keepdims=True), 0.0)
        odd = (t_row % 2) == 1
        s_1 = _nt_dot(jnp.where(odd, q * f, 0.0).astype(BF16), jnp.where(odd, 0.0, k).astype(BF16))
        scores = scores + jnp.where((tt // 2) == (ss // 2), s_1, 0.0)
        for m, ref in zip(halves, _level_refs(b)):
            if m >= SUBLANES:
                a_parts, b_parts = [], []
                for blk in range(c // (2 * m)):
                    lo, mid, hi = blk * 2 * m, blk * 2 * m + m, (blk + 1) * 2 * m
                    zero = jnp.zeros((m, LANES), BF16)
                    a_parts += [zero, (q[mid:hi] * jnp.exp2(b[mid:hi] - ref[mid:hi])).astype(BF16)]
                    b_parts += [(k[lo:mid] * jnp.exp2(ref[lo:mid] - b[lo:mid])).astype(BF16), zero]
                a_l = jnp.concatenate(a_parts, axis=0)
                b_l = jnp.concatenate(b_parts, axis=0)
            else:
                upper = (t_row % (2 * m)) >= m
                a_l = jnp.where(upper, q * jnp.exp2(b - ref), 0.0).astype(BF16)
                b_l = jnp.where(upper, 0.0, k * jnp.exp2(ref - b)).astype(BF16)
            s_l = _nt_dot(a_l, b_l)
            scores = scores + (s_l if 2 * m == c else jnp.where((tt // (2 * m)) == (ss // (2 * m)), s_l, 0.0))
        oi_scr[ci] = jnp.dot(scores.astype(BF16), v, preferred_element_type=F32)
        qd_scr[ci] = (q * jnp.exp2(b)).astype(BF16)
        b_last = b[c - 1:c, :]
        kd = (k * jnp.exp2(b_last - b)).astype(BF16)
        u_scr[ci] = lax.dot_general(v, kd, (((0,), (0,)), ((), ())), preferred_element_type=F32)
        d_scr[ci] = jnp.exp2(b_last)
        return carry

    lax.fori_loop(0, nc, intra, 0, unroll=8)

    def inter(ci, state_t):
        r0 = pl.multiple_of(ci * c, c)
        gate = gt_ref[pl.ds(r0, c), :].astype(F32)
        o = oi_scr[ci] + _nt_dot(qd_scr[ci], state_t.astype(BF16))
        y = o * lax.rsqrt(jnp.mean(o * o, axis=-1, keepdims=True) + EPS) * og
        o_ref[pl.ds(r0, c), :] = (y * (gate * jax.nn.sigmoid(gate))).astype(o_ref.dtype)
        return state_t * d_scr[ci] + u_scr[ci]

    lax.fori_loop(0, nc, inter, jnp.zeros((C_VDIM, C_KDIM), F32), unroll=8)


def _hgrn(proj, lb, og, bsz, seq):
    t, n4 = proj.shape
    heads = n4 // (4 * LANES)
    nc = seq // C_CHUNK
    return pl.pallas_call(
        _hgrn_kernel,
        out_shape=jax.ShapeDtypeStruct((t, heads * C_VDIM), BF16),
        grid=(bsz, heads),
        in_specs=[
            pl.BlockSpec((seq, LANES), lambda b, h: (b, h)),
            pl.BlockSpec((seq, LANES), lambda b, h: (b, heads + h)),
            pl.BlockSpec((seq, LANES), lambda b, h: (b, 2 * heads + h)),
            pl.BlockSpec((seq, LANES), lambda b, h: (b, 3 * heads + h)),
            pl.BlockSpec((1, LANES), lambda b, h: (0, h)),
            pl.BlockSpec((1, LANES), lambda b, h: (0, 0)),
        ],
        out_specs=pl.BlockSpec((seq, LANES), lambda b, h: (b, h)),
        scratch_shapes=[pltpu.VMEM((nc, C_CHUNK, C_VDIM), F32),
                        pltpu.VMEM((nc, C_CHUNK, C_KDIM), BF16),
                        pltpu.VMEM((nc, C_VDIM, C_KDIM), F32),
                        pltpu.VMEM((nc, 1, C_KDIM), F32)],
        compiler_params=_params(("arbitrary", "arbitrary")),
        name="hgrn2",
    )(proj, proj, proj, proj, lb.reshape(1, -1), og.reshape(1, LANES))


def _first_max(vals):
    best, idx = vals[0], jnp.zeros(vals[0].shape, jnp.int32)
    for i in range(1, len(vals)):
        better = vals[i] > best
        best = jnp.where(better, vals[i], best)
        idx = jnp.where(better, i, idx)
    return best, idx


def _pick(idx, vals):
    out = vals[-1]
    for i in range(len(vals) - 2, -1, -1):
        out = jnp.where(idx == i, vals[i], out)
    return out


def _router_kernel(x_ref, g_ref, sh_ref, sc_ref, rw_ref, rb_ref,
                   h_ref, eidx_ref, wsel_ref, rank_ref, cnt_ref, carry_scr):
    i = pl.program_id(0)
    tm = x_ref.shape[0]

    @pl.when(i == 0)
    def _():
        carry_scr[...] = jnp.zeros_like(carry_scr)

    h = _norm_mod(x_ref[...], g_ref[...], sh_ref[0], sc_ref[0])
    h_ref[...] = h
    logits = lax.dot_general(rw_ref[...], h, (((1,), (1,)), ((), ())),
                             preferred_element_type=F32, precision=lax.Precision.HIGHEST)
    scores = jax.nn.sigmoid(logits)
    sel = scores + rb_ref[...]
    epg = EXPERTS_PER_GROUP
    sel_rows = [sel[e:e + 1, :] for e in range(N_EXPERTS)]
    sc_rows = [scores[e:e + 1, :] for e in range(N_EXPERTS)]
    gsum = []
    for g in range(N_GROUPS):
        r = sel_rows[g * epg:(g + 1) * epg]
        pair = None
        for a in range(epg):
            for b in range(a + 1, epg):
                pair = r[a] + r[b] if pair is None else jnp.maximum(pair, r[a] + r[b])
        gsum.append(pair)
    _, g_idx = _first_max(gsum)
    cand = [_pick(g_idx, [sel_rows[g * epg + j] for g in range(N_GROUPS)]) for j in range(epg)]
    cand_sc = [_pick(g_idx, [sc_rows[g * epg + j] for g in range(N_GROUPS)]) for j in range(epg)]
    _, i1 = _first_max(cand)
    _, i2 = _first_max([jnp.where(i1 == j, -jnp.inf, cand[j]) for j in range(epg)])
    w1 = _pick(i1, cand_sc)
    w2 = _pick(i2, cand_sc)
    tot = w1 + w2
    e1 = g_idx * epg + i1
    e2 = g_idx * epg + i2
    eidx_ref[0:1, :] = e1
    eidx_ref[1:2, :] = e2
    wsel_ref[0:1, :] = w1 / tot
    wsel_ref[1:2, :] = w2 / tot

    erow = lax.broadcasted_iota(jnp.int32, (N_EXPERTS, tm), 0)
    oh1 = erow == e1
    oh2 = erow == e2
    onehot = jnp.where(oh1 | oh2, 1.0, 0.0)
    si = lax.broadcasted_iota(jnp.int32, (tm, tm), 0)
    ti = lax.broadcasted_iota(jnp.int32, (tm, tm), 1)
    incl = jnp.where(si <= ti, 1.0, 0.0).astype(BF16)
    csum = jnp.dot(onehot.astype(BF16), incl, preferred_element_type=F32)
    carry = carry_scr[...]
    before = csum - onehot + carry[:, 0:1]
    r1 = jnp.sum(jnp.where(oh1, before, 0.0), axis=0, keepdims=True)
    r2 = jnp.sum(jnp.where(oh2, before, 0.0), axis=0, keepdims=True)
    rank_ref[0:1, :] = r1.astype(jnp.int32)
    rank_ref[1:2, :] = r2.astype(jnp.int32)
    new_carry = carry + csum[:, tm - 1:tm]
    carry_scr[...] = new_carry
    cnt_ref[...] = new_carry.astype(jnp.int32)


def _router(x, g, shift, scale, rw_t, rbias, seq):
    t, d = x.shape
    tm = ROW_TILE
    e = rw_t.shape[0]
    row = lambda i: (0, i)
    return pl.pallas_call(
        _router_kernel,
        out_shape=(jax.ShapeDtypeStruct((t, d), F32),
                   jax.ShapeDtypeStruct((TOP_K, t), jnp.int32),
                   jax.ShapeDtypeStruct((TOP_K, t), F32),
                   jax.ShapeDtypeStruct((TOP_K, t), jnp.int32),
                   jax.ShapeDtypeStruct((e, LANES), jnp.int32)),
        grid=(t // tm,),
        in_specs=[
            pl.BlockSpec((tm, d), lambda i: (i, 0)),
            pl.BlockSpec((1, d), lambda i: (0, 0)),
            pl.BlockSpec((1, 1, d), lambda i: (i * tm // seq, 0, 0)),
            pl.BlockSpec((1, 1, d), lambda i: (i * tm // seq, 0, 0)),
            pl.BlockSpec((e, d), lambda i: (0, 0)),
            pl.BlockSpec((e, 1), lambda i: (0, 0)),
        ],
        out_specs=(pl.BlockSpec((tm, d), lambda i: (i, 0)),
                   pl.BlockSpec((TOP_K, tm), row),
                   pl.BlockSpec((TOP_K, tm), row),
                   pl.BlockSpec((TOP_K, tm), row),
                   pl.BlockSpec((e, LANES), lambda i: (0, 0))),
        scratch_shapes=[pltpu.VMEM((e, LANES), F32)],
        compiler_params=_params(("arbitrary",)),
        name="router",
    )(x, g.reshape(1, d), shift, scale, rw_t, rbias.reshape(e, 1))


def _expert_kernel(te_ref, tv_ref, tf_ref, tn_ref, src_ref, h_hbm, wg_hbm, wu_hbm, wd_hbm, y_ref,
                   xbuf0, xbuf1, sg, su, sd, bg, bu, bd, gsem, wsem, *, layer):
    i = pl.program_id(0)
    xbufs = (xbuf0, xbuf1)
    tm = xbuf0.shape[0]

    def gather(tile, buf):
        for r in range(tm):
            tok = src_ref[tile * tm + r]
            pltpu.make_async_copy(h_hbm.at[pl.ds(tok, 1)], xbufs[buf].at[pl.ds(r, 1)], gsem.at[buf]).start()

    def fetch(e):
        return (pltpu.make_async_copy(wg_hbm.at[layer, e], sg, wsem.at[0]),
                pltpu.make_async_copy(wu_hbm.at[layer, e], su, wsem.at[1]),
                pltpu.make_async_copy(wd_hbm.at[layer, e], sd, wsem.at[2]))

    @pl.when(i == 0)
    def _():
        for c in fetch(te_ref[0]):
            c.start()
        gather(0, 0)

    pending = (i == 0) | (tv_ref[jnp.maximum(i - 1, 0)] > 0)
    for par in range(2):
        @pl.when(pending & (i % 2 == par))
        def _(par=par):
            pltpu.make_async_copy(h_hbm.at[pl.ds(0, tm)], xbufs[par], gsem.at[par]).wait()

    @pl.when(tf_ref[i] > 0)
    def _():
        for c in fetch(te_ref[i]):
            c.wait()
        bg[...] = sg[...].astype(BF16)
        bu[...] = su[...].astype(BF16)
        bd[...] = sd[...].astype(BF16)

        @pl.when(tn_ref[i] >= 0)
        def _():
            for c in fetch(tn_ref[i]):
                c.start()

    for par in range(2):
        @pl.when((tv_ref[i] > 0) & (i % 2 == par))
        def _(par=par):
            gather(i + 1, 1 - par)
            x = xbufs[par][...].astype(BF16)
            a = jnp.dot(x, bg[...], preferred_element_type=F32)
            u = jnp.dot(x, bu[...], preferred_element_type=F32)
            hid = (a * jax.nn.sigmoid(a) * u).astype(BF16)
            y_ref[...] = jnp.dot(hid, bd[...], preferred_element_type=F32)

    @pl.when(tv_ref[i] == 0)
    def _():
        y_ref[...] = jnp.zeros_like(y_ref)


def _experts(tile_expert, tile_valid, tile_first, tile_next, src, h, w_gate, w_up, w_down, layer, n_rows):
    d = h.shape[1]
    f = w_gate.shape[3]
    tm = EXPERT_TILE
    grid_spec = pltpu.PrefetchScalarGridSpec(
        num_scalar_prefetch=5,
        grid=(n_rows // tm,),
        in_specs=[pl.BlockSpec(memory_space=pl.ANY)] * 4,
        out_specs=pl.BlockSpec((tm, d), lambda i, *_: (i, 0)),
        scratch_shapes=[pltpu.VMEM((tm, d), F32), pltpu.VMEM((tm, d), F32),
                        pltpu.VMEM((d, f), F32), pltpu.VMEM((d, f), F32), pltpu.VMEM((f, d), F32),
                        pltpu.VMEM((d, f), BF16), pltpu.VMEM((d, f), BF16), pltpu.VMEM((f, d), BF16),
                        pltpu.SemaphoreType.DMA((2,)), pltpu.SemaphoreType.DMA((3,))],
    )
    return pl.pallas_call(
        functools.partial(_expert_kernel, layer=layer),
        out_shape=jax.ShapeDtypeStruct((n_rows, d), F32),
        grid_spec=grid_spec,
        compiler_params=_params(("arbitrary",)),
        name="experts",
    )(tile_expert, tile_valid, tile_first, tile_next, src, h, w_gate, w_up, w_down)


def _combine_kernel(pos_ref, x_ref, w_ref, g_ref, y_ref, o_ref, buf, sem):
    i = pl.program_id(0)
    tm = x_ref.shape[0]
    t = pl.num_programs(0) * tm

    def row_copy(r, slot):
        p = pos_ref[slot * t + i * tm + r]
        return pltpu.make_async_copy(y_ref.at[pl.ds(p, 1)], buf.at[slot, pl.ds(r, 1)], sem)

    def start(r, carry):
        row_copy(r, 0).start()
        row_copy(r, 1).start()
        return carry

    lax.fori_loop(0, tm, start, 0, unroll=8)
    for slot in range(TOP_K):
        pltpu.make_async_copy(y_ref.at[pl.ds(0, tm)], buf.at[slot], sem).wait()
    w = w_ref[...]
    y = w[:, 0:1] * buf[0] + w[:, 1:2] * buf[1]
    o_ref[...] = x_ref[...] + g_ref[0] * y


def _combine(pos_flat, x, wsel_t, gate, y, seq):
    t, d = x.shape
    tm = ROW_TILE
    grid_spec = pltpu.PrefetchScalarGridSpec(
        num_scalar_prefetch=1,
        grid=(t // tm,),
        in_specs=[
            pl.BlockSpec((tm, d), lambda i, p: (i, 0)),
            pl.BlockSpec((tm, TOP_K), lambda i, p: (i, 0)),
            pl.BlockSpec((1, 1, d), lambda i, p: (i * tm // seq, 0, 0)),
            pl.BlockSpec(memory_space=pl.ANY),
        ],
        out_specs=pl.BlockSpec((tm, d), lambda i, p: (i, 0)),
        scratch_shapes=[pltpu.VMEM((TOP_K, tm, d), F32), pltpu.SemaphoreType.DMA(())],
    )
    return pl.pallas_call(
        _combine_kernel,
        out_shape=jax.ShapeDtypeStruct((t, d), F32),
        grid_spec=grid_spec,
        compiler_params=_params(("arbitrary",)),
        name="combine",
    )(pos_flat, x, wsel_t, gate, y)


def _moe(x, g, shift, scale, gate, rw_t, rbias, w_gate, w_up, w_down, layer, seq):
    t, d = x.shape
    te = EXPERT_TILE
    n_rows = TOP_K * t + N_EXPERTS * te
    n_tiles = n_rows // te
    h, eidx, wsel, rank, cnt = _router(x, g, shift, scale, rw_t, rbias, seq)
    counts = cnt[:, 0]
    padded = ((counts + te - 1) // te) * te
    ends = jnp.cumsum(padded)
    starts = ends - padded
    pos = rank
    for e in range(N_EXPERTS):
        pos = pos + jnp.where(eidx == e, starts[e], 0)
    pos_flat = pos.reshape(-1)
    tile_start = jnp.arange(n_tiles, dtype=jnp.int32) * te
    tile_expert = jnp.minimum(jnp.sum(tile_start[:, None] >= ends[None, :], axis=1), N_EXPERTS - 1).astype(jnp.int32)
    tile_valid = (tile_start < ends[-1]).astype(jnp.int32)
    tile_expert = jnp.where(tile_valid > 0, tile_expert, tile_expert[jnp.maximum(ends[-1] // te - 1, 0)])
    prev_expert = jnp.concatenate([jnp.full((1,), -1, jnp.int32), tile_expert[:-1]])
    tile_first = ((tile_valid > 0) & (tile_expert != prev_expert)).astype(jnp.int32)
    later = (tile_expert[None, :] > tile_expert[:, None]) & (tile_valid[None, :] > 0)
    tile_next = jnp.min(jnp.where(later, tile_expert[None, :], N_EXPERTS), axis=1)
    tile_next = jnp.where(tile_next < N_EXPERTS, tile_next, -1).astype(jnp.int32)
    tok = jnp.tile(jnp.arange(t, dtype=jnp.int32), TOP_K)
    src = jnp.zeros((n_rows + te,), jnp.int32).at[pos_flat].set(tok, unique_indices=True)
    y = _experts(tile_expert, tile_valid, tile_first, tile_next, src, h, w_gate, w_up, w_down, layer, n_rows)
    return _combine(pos_flat, x, wsel.T, gate, y, seq)


def kernel(x, c, positions, ada_w, ada_b, norm1_g, norm2_g, ab_w_in, q_norm_g, k_norm_g, sinks, gm_vnorm_g, gm_ws, gm_b, ab_w_out, c_w_in, c_lower_bounds, c_onorm_g, c_w_out, router_w, router_bias, moe_w_gate, moe_w_up, moe_w_down):
    bsz, seq, d = x.shape
    depth = ada_w.shape[0]
    t = bsz * seq
    xt = x.reshape(t, d)

    c_pad = jnp.concatenate([c, jnp.zeros((8 - bsz % 8, d), F32)], axis=0) if bsz % 8 else c
    mod = _ada_mod(c_pad, ada_w, ada_b)[:, :bsz]
    mod = mod.reshape(depth, bsz, 6, 1, d)
    cos_t, sin_t = _rope_tables(positions)

    sm = jax.nn.softmax(c_lower_bounds.astype(F32), axis=0)
    lower_bounds = jnp.cumsum(sm, axis=0) - sm[0:1]
    rw_t = router_w.T

    for l in range(depth):
        sh1, sc1, g1, sh2, sc2, g2 = (mod[l, :, k] for k in range(6))
        if l % 2 == 0:
            i = l // 2
            proj = _in_proj(xt, norm1_g[l], sh1, sc1, ab_w_in[i].astype(BF16), seq)
            mix = _mix_even(proj, cos_t, sin_t, q_norm_g[i], k_norm_g[i], sinks[i],
                            gm_vnorm_g[i], gm_ws[i], gm_b[i], bsz, seq)
            xt = _out_proj(mix, ab_w_out[i].astype(BF16), xt, g1, seq)
        else:
            j = l // 2
            proj = _in_proj(xt, norm1_g[l], sh1, sc1, c_w_in[j].astype(BF16), seq)
            o = _hgrn(proj, lower_bounds[l], c_onorm_g[j], bsz, seq)
            xt = _out_proj(o, c_w_out[j].astype(BF16), xt, g1, seq)
        xt = _moe(xt, norm2_g[l], sh2, sc2, g2, rw_t, router_bias,
                  moe_w_gate, moe_w_up, moe_w_down, l, seq)
    return xt.reshape(bsz, seq, d)
```

```python
import functools

import jax
import jax.numpy as jnp
from jax import lax
from jax.experimental import pallas as pl
from jax.experimental.pallas import tpu as pltpu

F32 = jnp.float32
BF16 = jnp.bfloat16

EPS = 1e-6
HEAD_DIM = 64
KV_REP = 4
WINDOW = 128
ROPE_THETA = 10000.0
B_GROUP_DIM = 128
B_CHUNK = 128
C_KDIM = 128
C_VDIM = 128
C_CHUNK = 64
N_GROUPS = 4
EXPERTS_PER_GROUP = 4
N_EXPERTS = N_GROUPS * EXPERTS_PER_GROUP
TOP_K = 2

LANES = 128
SUBLANES = 8
VMEM_LIMIT = 56 * 1024 * 1024

ROW_TILE = 512
PROJ_ROW_TILE = 1024
PROJ_COL_TILE = 2048
EXPERT_TILE = 256
WEIGHT_DMA_PRIORITY = 1
HGRN_BLOCK = 16
HGRN_MAX_BLOCK_DECAY = 100.0


def _params(sem):
    return pltpu.CompilerParams(dimension_semantics=sem, vmem_limit_bytes=VMEM_LIMIT)


def _ada_kernel(c_ref, w_ref, b_ref, o_ref):
    c = c_ref[...]
    cond = c * jax.nn.sigmoid(c)
    o_ref[0] = jnp.dot(cond, w_ref[0], preferred_element_type=F32) + b_ref[0]


def _ada_mod(c_pad, ada_w, ada_b):
    depth, d, n = ada_w.shape
    tn = 1536
    rows = c_pad.shape[0]
    return pl.pallas_call(
        _ada_kernel,
        out_shape=jax.ShapeDtypeStruct((depth, rows, n), F32),
        grid=(depth, n // tn),
        in_specs=[
            pl.BlockSpec((rows, d), lambda l, j: (0, 0)),
            pl.BlockSpec((1, d, tn), lambda l, j: (l, 0, j)),
            pl.BlockSpec((1, 1, tn), lambda l, j: (l, 0, j)),
        ],
        out_specs=pl.BlockSpec((1, rows, tn), lambda l, j: (l, 0, j)),
        compiler_params=_params(("arbitrary", "arbitrary")),
        name="ada_mod",
    )(c_pad, ada_w, ada_b.reshape(depth, 1, n))


def _rope_table_kernel(pos_ref, freq_ref, sign_ref, cos_ref, sin_ref):
    ang = pos_ref[...].astype(F32) * freq_ref[...]
    cos_ref[...] = jnp.cos(ang)
    sin_ref[...] = jnp.sin(ang) * sign_ref[...]


def _rope_tables(positions):
    t = positions.size
    half = HEAD_DIM // 2
    inv_freq = ROPE_THETA ** (-jnp.arange(half, dtype=F32) / half)
    freq = jnp.tile(inv_freq, LANES // half).reshape(1, LANES)
    sign = jnp.tile(jnp.concatenate([-jnp.ones((half,), F32), jnp.ones((half,), F32)]),
                    LANES // HEAD_DIM).reshape(1, LANES)
    tm = min(1024, t)
    return pl.pallas_call(
        _rope_table_kernel,
        out_shape=(jax.ShapeDtypeStruct((t, LANES), F32), jax.ShapeDtypeStruct((t, LANES), F32)),
        grid=(t // tm,),
        in_specs=[
            pl.BlockSpec((tm, 1), lambda i: (i, 0)),
            pl.BlockSpec((1, LANES), lambda i: (0, 0)),
            pl.BlockSpec((1, LANES), lambda i: (0, 0)),
        ],
        out_specs=(pl.BlockSpec((tm, LANES), lambda i: (i, 0)),
                   pl.BlockSpec((tm, LANES), lambda i: (i, 0))),
        compiler_params=_params(("arbitrary",)),
        name="rope_tables",
    )(positions.reshape(t, 1), freq, sign)


def _norm_mod(x, g, shift, scale):
    y = x * lax.rsqrt(jnp.mean(x * x, axis=-1, keepdims=True) + EPS)
    return (y * g) * (1 + scale) + shift


def _in_proj_kernel(x_ref, g_ref, sh_ref, sc_ref, w_ref, o_ref, h_scr):
    @pl.when(pl.program_id(1) == 0)
    def _():
        h_scr[...] = _norm_mod(x_ref[...], g_ref[...], sh_ref[0], sc_ref[0]).astype(BF16)

    o_ref[...] = jnp.dot(h_scr[...], w_ref[0], preferred_element_type=F32).astype(o_ref.dtype)


def _in_proj(x, g, shift, scale, w, layer, seq):
    t, d = x.shape
    n = w.shape[2]
    tm = PROJ_ROW_TILE
    tn = max(c for c in range(LANES, PROJ_COL_TILE + 1, LANES) if n % c == 0)
    assert seq % tm == 0 and t % tm == 0
    return pl.pallas_call(
        _in_proj_kernel,
        out_shape=jax.ShapeDtypeStruct((t, n), BF16),
        grid=(t // tm, n // tn),
        in_specs=[
            pl.BlockSpec((tm, d), lambda i, j: (i, 0)),
            pl.BlockSpec((1, d), lambda i, j: (0, 0)),
            pl.BlockSpec((1, 1, d), lambda i, j: (i * tm // seq, 0, 0)),
            pl.BlockSpec((1, 1, d), lambda i, j: (i * tm // seq, 0, 0)),
            pl.BlockSpec((1, d, tn), lambda i, j: (layer, 0, j)),
        ],
        out_specs=pl.BlockSpec((tm, tn), lambda i, j: (i, j)),
        scratch_shapes=[pltpu.VMEM((tm, d), BF16)],
        compiler_params=_params(("arbitrary", "arbitrary")),
        name="in_proj",
    )(x, g.reshape(1, d), shift, scale, w)


def _out_proj_kernel(a_ref, w_ref, x_ref, g_ref, o_ref):
    y = jnp.dot(a_ref[...], w_ref[0], preferred_element_type=F32)
    o_ref[...] = x_ref[...] + g_ref[0] * y


def _out_proj(a, w, layer, x, gate, seq):
    t, k = a.shape
    n = w.shape[2]
    tm, tn = PROJ_ROW_TILE, 1024
    assert seq % tm == 0 and t % tm == 0 and n % tn == 0
    return pl.pallas_call(
        _out_proj_kernel,
        out_shape=jax.ShapeDtypeStruct((t, n), F32),
        grid=(t // tm, n // tn),
        in_specs=[
            pl.BlockSpec((tm, k), lambda i, j: (i, 0)),
            pl.BlockSpec((1, k, tn), lambda i, j: (layer, 0, j)),
            pl.BlockSpec((tm, tn), lambda i, j: (i, j)),
            pl.BlockSpec((1, 1, tn), lambda i, j: (i * tm // seq, 0, j)),
        ],
        out_specs=pl.BlockSpec((tm, tn), lambda i, j: (i, j)),
        compiler_params=_params(("arbitrary", "arbitrary")),
        name="out_proj",
    )(a, w, x, gate)


def _segment_mean_matrix():
    r = lax.broadcasted_iota(jnp.int32, (LANES, LANES), 0) // HEAD_DIM
    c = lax.broadcasted_iota(jnp.int32, (LANES, LANES), 1) // HEAD_DIM
    return jnp.where(r == c, 1.0 / HEAD_DIM, 0.0).astype(F32)


def _head_norm_rope(x, gain, cos, sin, seg_mean, low_half):
    ms = jnp.dot(x * x, seg_mean, preferred_element_type=F32)
    y = x * lax.rsqrt(ms + EPS) * gain
    partner = jnp.where(low_half, pltpu.roll(y, LANES - HEAD_DIM // 2, 1),
                        pltpu.roll(y, HEAD_DIM // 2, 1))
    return y * cos + partner * sin


def _gelu_tanh(x):
    return 0.5 * x * (1.0 + jnp.tanh(0.7978845608028654 * (x + 0.044715 * (x * x * x))))


def _mix_kernel(sinks_ref,
                q0_ref, q1_ref, kv_ref, u0_ref, u1_ref, v0_ref, v1_ref,
                cos_ref, sin_ref, qg_ref, kg_ref, vng_ref, ws_ref, bs_ref,
                o_ref, kd_scr, vd_scr):
    n = pl.program_id(1)
    w = WINDOW
    n_kv = kd_scr.shape[0]

    @pl.when(n == 0)
    def _():
        kd_scr[...] = jnp.zeros_like(kd_scr)
        vd_scr[...] = jnp.zeros_like(vd_scr)

    lane = lax.broadcasted_iota(jnp.int32, (w, LANES), 1)
    low_half = (lane % HEAD_DIM) < (HEAD_DIM // 2)
    first_head = lane < HEAD_DIM
    seg_mean = _segment_mean_matrix()
    cos = cos_ref[...]
    sin = sin_ref[...]

    kv = kv_ref[...].astype(F32)
    for c in range(n_kv // 2):
        kc = _head_norm_rope(kv[:, c * LANES:(c + 1) * LANES], kg_ref[...], cos, sin, seg_mean, low_half)
        vc = kv[:, (n_kv // 2 + c) * LANES:(n_kv // 2 + c + 1) * LANES]
        kc_sw = pltpu.roll(kc, HEAD_DIM, 1)
        vc_sw = pltpu.roll(vc, HEAD_DIM, 1)
        kd_scr[2 * c, w:, :] = jnp.where(first_head, kc, kc_sw).astype(BF16)
        kd_scr[2 * c + 1, w:, :] = jnp.where(first_head, kc_sw, kc).astype(BF16)
        vd_scr[2 * c, w:, :] = jnp.where(first_head, vc, vc_sw).astype(BF16)
        vd_scr[2 * c + 1, w:, :] = jnp.where(first_head, vc_sw, vc).astype(BF16)

    qi = lax.broadcasted_iota(jnp.int32, (w, 2 * w), 0)
    kj = lax.broadcasted_iota(jnp.int32, (w, 2 * w), 1)
    diff = qi + w - kj
    band = (diff >= 0) & (diff < w) & ((kj >= w) | (n > 0))
    band4 = jnp.concatenate([band] * KV_REP, axis=0)

    scale = HEAD_DIM ** -0.5
    q_refs = (q0_ref, q1_ref)
    chunks_per_ref = q0_ref.shape[1] // LANES
    for g in range(n_kv):
        rows = []
        for cc in range(2):
            c = 2 * g + cc
            qc = q_refs[c // chunks_per_ref][:, (c % chunks_per_ref) * LANES:(c % chunks_per_ref + 1) * LANES]
            qc = _head_norm_rope(qc.astype(F32), qg_ref[...], cos, sin, seg_mean, low_half)
            rows.append(jnp.where(first_head, qc, 0.0))
            rows.append(jnp.where(first_head, 0.0, qc))
        qm = jnp.concatenate(rows, axis=0).astype(BF16)
        s = lax.dot_general(qm, kd_scr[g], (((1,), (1,)), ((), ())),
                            preferred_element_type=F32) * scale
        s = jnp.where(band4, s, -jnp.inf)
        sink = jnp.concatenate(
            [jnp.full((w, 1), sinks_ref[KV_REP * g + r], F32) for r in range(KV_REP)], axis=0)
        m = jnp.maximum(jnp.max(s, axis=-1, keepdims=True), sink)
        p = jnp.exp(s - m)
        p = p / (jnp.sum(p, axis=-1, keepdims=True) + jnp.exp(sink - m))
        o = jnp.dot(p.astype(BF16), vd_scr[g], preferred_element_type=F32)
        for cc in range(2):
            oc = jnp.where(first_head, o[(2 * cc) * w:(2 * cc + 1) * w], o[(2 * cc + 1) * w:(2 * cc + 2) * w])
            c = 2 * g + cc
            o_ref[:, c * LANES:(c + 1) * LANES] = oc.astype(o_ref.dtype)

    kd_scr[:, :w, :] = kd_scr[:, w:, :]
    vd_scr[:, :w, :] = vd_scr[:, w:, :]

    a_q = 2 * q0_ref.shape[1]
    u = _gelu_tanh(jnp.concatenate([u0_ref[...], u1_ref[...]], axis=1).astype(F32))
    v = _gelu_tanh(jnp.concatenate([v0_ref[...], v1_ref[...]], axis=1).astype(F32))
    mu = jnp.mean(v, axis=-1, keepdims=True)
    vc = v - mu
    vn = (vc * lax.rsqrt(jnp.mean(vc * vc, axis=-1, keepdims=True) + EPS) * vng_ref[...]).astype(BF16)
    ti = lax.broadcasted_iota(jnp.int32, (w, w), 0)
    si = lax.broadcasted_iota(jnp.int32, (w, w), 1)
    causal = si <= ti
    bs = bs_ref[...]
    for g in range(ws_ref.shape[0]):
        wg = jnp.where(causal, ws_ref[g], 0.0).astype(BF16)
        sv = jnp.dot(wg, vn[:, g * LANES:(g + 1) * LANES], preferred_element_type=F32) + bs[:, g:g + 1]
        o_ref[:, a_q + g * LANES:a_q + (g + 1) * LANES] = (u[:, g * LANES:(g + 1) * LANES] * sv).astype(o_ref.dtype)


def _mix_even(proj, cos_t, sin_t, q_g, k_g, sinks, vnorm_g, ws, bs, bsz, seq):
    t = proj.shape[0]
    w = WINDOW
    nb = seq // w
    heads = sinks.shape[0]
    n_kv = heads // KV_REP
    a_q = heads * HEAD_DIM
    a_kv = n_kv * HEAD_DIM
    b_w = vnorm_g.shape[0]
    cw = 512
    assert a_q == 2 * cw and 2 * a_kv == cw and b_w == 2 * cw
    row = lambda b, n: b * nb + n
    gain2 = lambda g: jnp.tile(g, LANES // HEAD_DIM).reshape(1, LANES)
    col = lambda j: pl.BlockSpec((w, cw), lambda b, n, s: (row(b, n), j))
    grid_spec = pltpu.PrefetchScalarGridSpec(
        num_scalar_prefetch=1,
        grid=(bsz, nb),
        in_specs=[
            col(0), col(1), col(2), col(3), col(4), col(5), col(6),
            pl.BlockSpec((w, LANES), lambda b, n, s: (row(b, n), 0)),
            pl.BlockSpec((w, LANES), lambda b, n, s: (row(b, n), 0)),
            pl.BlockSpec((1, LANES), lambda b, n, s: (0, 0)),
            pl.BlockSpec((1, LANES), lambda b, n, s: (0, 0)),
            pl.BlockSpec((1, b_w), lambda b, n, s: (0, 0)),
            pl.BlockSpec(ws.shape, lambda b, n, s: (0, 0, 0)),
            pl.BlockSpec((w, ws.shape[0]), lambda b, n, s: (0, 0)),
        ],
        out_specs=pl.BlockSpec((w, a_q + b_w), lambda b, n, s: (row(b, n), 0)),
        scratch_shapes=[pltpu.VMEM((n_kv, 2 * w, LANES), BF16),
                        pltpu.VMEM((n_kv, 2 * w, LANES), BF16)],
    )
    return pl.pallas_call(
        _mix_kernel,
        out_shape=jax.ShapeDtypeStruct((t, a_q + b_w), BF16),
        grid_spec=grid_spec,
        compiler_params=_params(("arbitrary", "arbitrary")),
        name="mix_even",
    )(sinks, proj, proj, proj, proj, proj, proj, proj, cos_t, sin_t,
      gain2(q_g), gain2(k_g), vnorm_g.reshape(1, b_w), ws, bs.T)


def _nt_dot(a, b):
    return lax.dot_general(a, b, (((1,), (1,)), ((), ())), preferred_element_type=F32)


def _hgrn_kernel(q_ref, f_ref, i_ref, gt_ref, lb_ref, og_ref, o_ref,
                 oi_scr, qd_scr, u_scr, d_scr):
    c = C_CHUNK
    nc = q_ref.shape[0] // c
    nv = c // SUBLANES
    lb = lb_ref[...]
    og = og_ref[...]
    t_row = lax.broadcasted_iota(jnp.int32, (c, LANES), 0)
    tt = lax.broadcasted_iota(jnp.int32, (c, c), 0)
    ss = lax.broadcasted_iota(jnp.int32, (c, c), 1)

    def inputs(ci):
        r0 = pl.multiple_of(ci * c, c)
        q = q_ref[pl.ds(r0, c), :].astype(F32)
        fl = f_ref[pl.ds(r0, c), :].astype(F32)
        v = i_ref[pl.ds(r0, c), :]
        f = lb + (1 - lb) * jax.nn.sigmoid(fl)
        k = 1 - f
        b = jnp.log2(f)
        for d in (1, 2, 4, 8, 16, 32):
            b = b + jnp.where(t_row >= d, pltpu.roll(b, d, 0), 0.0)
        return q, f, k, b, v

    def row(b, r, n):
        b3 = b.reshape(nv, SUBLANES, LANES)
        rep = jnp.broadcast_to(b3[r // SUBLANES:r // SUBLANES + 1, r % SUBLANES:r % SUBLANES + 1, :],
                               (n // SUBLANES, SUBLANES, LANES))
        return rep.reshape(n, LANES)

    def half_level(q, k, b, m):
        a_parts, b_parts = [], []
        for blk in range(c // (2 * m)):
            lo, mid, hi = blk * 2 * m, blk * 2 * m + m, (blk + 1) * 2 * m
            zero = jnp.zeros((m, LANES), BF16)
            ref = row(b, mid - 1, m)
            a_parts += [zero, (q[mid:hi] * jnp.exp2(b[mid:hi] - ref)).astype(BF16)]
            b_parts += [(k[lo:mid] * jnp.exp2(ref - b[lo:mid])).astype(BF16), zero]
        return _nt_dot(jnp.concatenate(a_parts, axis=0), jnp.concatenate(b_parts, axis=0))

    def finish(ci, q, k, b, v, scores):
        oi_scr[ci] = jnp.dot(scores.astype(BF16), v, preferred_element_type=F32)
        qd_scr[ci] = (q * jnp.exp2(b)).astype(BF16)
        b_last = b[c - 1:c, :]
        kd = (k * jnp.exp2(b_last - b)).astype(BF16)
        u_scr[ci] = lax.dot_general(v, kd, (((0,), (0,)), ((), ())), preferred_element_type=F32)
        d_scr[ci] = jnp.exp2(b_last)

    def intra_fast(ci, worst):
        q, f, k, b, v = inputs(ci)
        blk = HGRN_BLOCK
        starts = [jnp.zeros((blk, LANES), F32)] + [row(b, s - 1, blk) for s in range(blk, c, blk)]
        base = jnp.concatenate(starts, axis=0)
        s_d = _nt_dot((q * jnp.exp2(b - base)).astype(BF16), (k * jnp.exp2(base - b)).astype(BF16))
        s_16 = half_level(q, k, b, blk)
        s_32 = half_level(q, k, b, 2 * blk)
        scores = jnp.where(((tt // blk) == (ss // blk)) & (ss <= tt), s_d,
                           jnp.where((tt // (2 * blk)) == (ss // (2 * blk)), s_16, s_32))
        finish(ci, q, k, b, v, scores)
        for s in range(0, c, blk):
            worst = jnp.maximum(worst, base[s:s + 1, :] - b[s + blk - 1:s + blk, :])
        return worst

    def intra_safe(ci, carry):
        q, f, k, b, v = inputs(ci)
        scores = jnp.where(tt == ss, jnp.sum(q * k, axis=-1, keepdims=True), 0.0)
        odd = (t_row % 2) == 1
        s_1 = _nt_dot(jnp.where(odd, q * f, 0.0).astype(BF16), jnp.where(odd, 0.0, k).astype(BF16))
        scores = scores + jnp.where((tt // 2) == (ss // 2), s_1, 0.0)
        b3 = b.reshape(nv, SUBLANES, LANES)
        sub = lax.broadcasted_iota(jnp.int32, b3.shape, 1)
        small_refs = {4: jnp.broadcast_to(b3[:, 3:4, :], b3.shape).reshape(c, LANES),
                      2: jnp.where(sub < 4, jnp.broadcast_to(b3[:, 1:2, :], b3.shape),
                                   jnp.broadcast_to(b3[:, 5:6, :], b3.shape)).reshape(c, LANES)}
        for m in (32, 16, 8, 4, 2):
            if m >= SUBLANES:
                s_l = half_level(q, k, b, m)
            else:
                ref = small_refs[m]
                upper = (t_row % (2 * m)) >= m
                s_l = _nt_dot(jnp.where(upper, q * jnp.exp2(b - ref), 0.0).astype(BF16),
                              jnp.where(upper, 0.0, k * jnp.exp2(ref - b)).astype(BF16))
            scores = scores + (s_l if 2 * m == c else jnp.where((tt // (2 * m)) == (ss // (2 * m)), s_l, 0.0))
        finish(ci, q, k, b, v, scores)
        return carry

    worst = lax.fori_loop(0, nc, intra_fast, jnp.zeros((1, LANES), F32), unroll=8)

    @pl.when(jnp.max(worst) > HGRN_MAX_BLOCK_DECAY)
    def _():
        lax.fori_loop(0, nc, intra_safe, 0, unroll=2)

    def inter(ci, state_t):
        r0 = pl.multiple_of(ci * c, c)
        gate = gt_ref[pl.ds(r0, c), :].astype(F32)
        o = oi_scr[ci] + _nt_dot(qd_scr[ci], state_t.astype(BF16))
        y = o * lax.rsqrt(jnp.mean(o * o, axis=-1, keepdims=True) + EPS) * og
        o_ref[pl.ds(r0, c), :] = (y * (gate * jax.nn.sigmoid(gate))).astype(o_ref.dtype)
        return state_t * d_scr[ci] + u_scr[ci]

    lax.fori_loop(0, nc, inter, jnp.zeros((C_VDIM, C_KDIM), F32), unroll=8)


def _hgrn(proj, lb, og, bsz, seq):
    t, n4 = proj.shape
    heads = n4 // (4 * LANES)
    nc = seq // C_CHUNK
    return pl.pallas_call(
        _hgrn_kernel,
        out_shape=jax.ShapeDtypeStruct((t, heads * C_VDIM), BF16),
        grid=(bsz, heads),
        in_specs=[
            pl.BlockSpec((seq, LANES), lambda b, h: (b, h)),
            pl.BlockSpec((seq, LANES), lambda b, h: (b, heads + h)),
            pl.BlockSpec((seq, LANES), lambda b, h: (b, 2 * heads + h)),
            pl.BlockSpec((seq, LANES), lambda b, h: (b, 3 * heads + h)),
            pl.BlockSpec((1, LANES), lambda b, h: (0, h)),
            pl.BlockSpec((1, LANES), lambda b, h: (0, 0)),
        ],
        out_specs=pl.BlockSpec((seq, LANES), lambda b, h: (b, h)),
        scratch_shapes=[pltpu.VMEM((nc, C_CHUNK, C_VDIM), F32),
                        pltpu.VMEM((nc, C_CHUNK, C_KDIM), BF16),
                        pltpu.VMEM((nc, C_VDIM, C_KDIM), F32),
                        pltpu.VMEM((nc, 1, C_KDIM), F32)],
        compiler_params=_params(("arbitrary", "arbitrary")),
        name="hgrn2",
    )(proj, proj, proj, proj, lb.reshape(1, -1), og.reshape(1, LANES))


def _first_max(vals):
    best, idx = vals[0], jnp.zeros(vals[0].shape, jnp.int32)
    for i in range(1, len(vals)):
        better = vals[i] > best
        best = jnp.where(better, vals[i], best)
        idx = jnp.where(better, i, idx)
    return best, idx


def _pick(idx, vals):
    out = vals[-1]
    for i in range(len(vals) - 2, -1, -1):
        out = jnp.where(idx == i, vals[i], out)
    return out


def _router_kernel(x_ref, g_ref, sh_ref, sc_ref, rw_ref, rb_ref,
                   h_ref, eidx_ref, wsel_ref, rank_ref, cnt_ref, carry_scr):
    i = pl.program_id(0)
    tm = x_ref.shape[0]

    @pl.when(i == 0)
    def _():
        carry_scr[...] = jnp.zeros_like(carry_scr)

    h = _norm_mod(x_ref[...], g_ref[...], sh_ref[0], sc_ref[0])
    h_ref[...] = h
    logits = lax.dot_general(rw_ref[...], h, (((1,), (1,)), ((), ())),
                             preferred_element_type=F32, precision=lax.Precision.HIGHEST)
    scores = jax.nn.sigmoid(logits)
    sel = scores + rb_ref[...]
    epg = EXPERTS_PER_GROUP
    sel_rows = [sel[e:e + 1, :] for e in range(N_EXPERTS)]
    sc_rows = [scores[e:e + 1, :] for e in range(N_EXPERTS)]
    gsum = []
    for g in range(N_GROUPS):
        r = sel_rows[g * epg:(g + 1) * epg]
        pair = None
        for a in range(epg):
            for b in range(a + 1, epg):
                pair = r[a] + r[b] if pair is None else jnp.maximum(pair, r[a] + r[b])
        gsum.append(pair)
    _, g_idx = _first_max(gsum)
    cand = [_pick(g_idx, [sel_rows[g * epg + j] for g in range(N_GROUPS)]) for j in range(epg)]
    cand_sc = [_pick(g_idx, [sc_rows[g * epg + j] for g in range(N_GROUPS)]) for j in range(epg)]
    _, i1 = _first_max(cand)
    _, i2 = _first_max([jnp.where(i1 == j, -jnp.inf, cand[j]) for j in range(epg)])
    w1 = _pick(i1, cand_sc)
    w2 = _pick(i2, cand_sc)
    tot = w1 + w2
    e1 = g_idx * epg + i1
    e2 = g_idx * epg + i2
    eidx_ref[0:1, :] = e1
    eidx_ref[1:2, :] = e2
    wsel_ref[0:1, :] = w1 / tot
    wsel_ref[1:2, :] = w2 / tot

    erow = lax.broadcasted_iota(jnp.int32, (N_EXPERTS, tm), 0)
    oh1 = erow == e1
    oh2 = erow == e2
    onehot = jnp.where(oh1 | oh2, 1.0, 0.0)
    si = lax.broadcasted_iota(jnp.int32, (tm, tm), 0)
    ti = lax.broadcasted_iota(jnp.int32, (tm, tm), 1)
    incl = jnp.where(si <= ti, 1.0, 0.0).astype(BF16)
    csum = jnp.dot(onehot.astype(BF16), incl, preferred_element_type=F32)
    carry = carry_scr[...]
    before = csum - onehot + carry[:, 0:1]
    r1 = jnp.sum(jnp.where(oh1, before, 0.0), axis=0, keepdims=True)
    r2 = jnp.sum(jnp.where(oh2, before, 0.0), axis=0, keepdims=True)
    rank_ref[0:1, :] = r1.astype(jnp.int32)
    rank_ref[1:2, :] = r2.astype(jnp.int32)
    new_carry = carry + csum[:, tm - 1:tm]
    carry_scr[...] = new_carry
    cnt_ref[...] = new_carry.astype(jnp.int32)


def _router(x, g, shift, scale, rw_t, rbias, seq):
    t, d = x.shape
    tm = ROW_TILE
    e = rw_t.shape[0]
    row = lambda i: (0, i)
    return pl.pallas_call(
        _router_kernel,
        out_shape=(jax.ShapeDtypeStruct((t, d), F32),
                   jax.ShapeDtypeStruct((TOP_K, t), jnp.int32),
                   jax.ShapeDtypeStruct((TOP_K, t), F32),
                   jax.ShapeDtypeStruct((TOP_K, t), jnp.int32),
                   jax.ShapeDtypeStruct((e, LANES), jnp.int32)),
        grid=(t // tm,),
        in_specs=[
            pl.BlockSpec((tm, d), lambda i: (i, 0)),
            pl.BlockSpec((1, d), lambda i: (0, 0)),
            pl.BlockSpec((1, 1, d), lambda i: (i * tm // seq, 0, 0)),
            pl.BlockSpec((1, 1, d), lambda i: (i * tm // seq, 0, 0)),
            pl.BlockSpec((e, d), lambda i: (0, 0)),
            pl.BlockSpec((e, 1), lambda i: (0, 0)),
        ],
        out_specs=(pl.BlockSpec((tm, d), lambda i: (i, 0)),
                   pl.BlockSpec((TOP_K, tm), row),
                   pl.BlockSpec((TOP_K, tm), row),
                   pl.BlockSpec((TOP_K, tm), row),
                   pl.BlockSpec((e, LANES), lambda i: (0, 0))),
        scratch_shapes=[pltpu.VMEM((e, LANES), F32)],
        compiler_params=_params(("arbitrary",)),
        name="router",
    )(x, g.reshape(1, d), shift, scale, rw_t, rbias.reshape(e, 1))


def _expert_kernel(te_ref, tv_ref, tf_ref, tn_ref, src_ref, h_hbm, wg_hbm, wu_hbm, wd_hbm, y_ref,
                   xbuf0, xbuf1, sg, su, sd, bg, bu, bd, gsem, wsem, *, layer):
    i = pl.program_id(0)
    xbufs = (xbuf0, xbuf1)
    tm = xbuf0.shape[0]

    def gather(tile, buf):
        for r in range(tm):
            tok = src_ref[tile * tm + r]
            pltpu.make_async_copy(h_hbm.at[pl.ds(tok, 1)], xbufs[buf].at[pl.ds(r, 1)], gsem.at[buf]).start()

    def fetch(e):
        return (pltpu.make_async_copy(wg_hbm.at[layer, e], sg, wsem.at[0]),
                pltpu.make_async_copy(wu_hbm.at[layer, e], su, wsem.at[1]),
                pltpu.make_async_copy(wd_hbm.at[layer, e], sd, wsem.at[2]))

    @pl.when(i == 0)
    def _():
        for c in fetch(te_ref[0]):
            c.start(priority=WEIGHT_DMA_PRIORITY)
        gather(0, 0)

    pending = (i == 0) | (tv_ref[jnp.maximum(i - 1, 0)] > 0)
    for par in range(2):
        @pl.when(pending & (i % 2 == par))
        def _(par=par):
            pltpu.make_async_copy(h_hbm.at[pl.ds(0, tm)], xbufs[par], gsem.at[par]).wait()

    @pl.when(tf_ref[i] > 0)
    def _():
        for c in fetch(te_ref[i]):
            c.wait()
        bg[...] = sg[...].astype(BF16)
        bu[...] = su[...].astype(BF16)
        bd[...] = sd[...].astype(BF16)

        @pl.when(tn_ref[i] >= 0)
        def _():
            for c in fetch(tn_ref[i]):
                c.start(priority=WEIGHT_DMA_PRIORITY)

    for par in range(2):
        @pl.when((tv_ref[i] > 0) & (i % 2 == par))
        def _(par=par):
            gather(i + 1, 1 - par)
            x = xbufs[par][...].astype(BF16)
            a = jnp.dot(x, bg[...], preferred_element_type=F32)
            u = jnp.dot(x, bu[...], preferred_element_type=F32)
            hid = (a * jax.nn.sigmoid(a) * u).astype(BF16)
            y_ref[...] = jnp.dot(hid, bd[...], preferred_element_type=F32)

    @pl.when(tv_ref[i] == 0)
    def _():
        y_ref[...] = jnp.zeros_like(y_ref)


def _experts(tile_expert, tile_valid, tile_first, tile_next, src, h, w_gate, w_up, w_down, layer, n_rows):
    d = h.shape[1]
    f = w_gate.shape[3]
    tm = EXPERT_TILE
    grid_spec = pltpu.PrefetchScalarGridSpec(
        num_scalar_prefetch=5,
        grid=(n_rows // tm,),
        in_specs=[pl.BlockSpec(memory_space=pl.ANY)] * 4,
        out_specs=pl.BlockSpec((tm, d), lambda i, *_: (i, 0)),
        scratch_shapes=[pltpu.VMEM((tm, d), F32), pltpu.VMEM((tm, d), F32),
                        pltpu.VMEM((d, f), F32), pltpu.VMEM((d, f), F32), pltpu.VMEM((f, d), F32),
                        pltpu.VMEM((d, f), BF16), pltpu.VMEM((d, f), BF16), pltpu.VMEM((f, d), BF16),
                        pltpu.SemaphoreType.DMA((2,)), pltpu.SemaphoreType.DMA((3,))],
    )
    return pl.pallas_call(
        functools.partial(_expert_kernel, layer=layer),
        out_shape=jax.ShapeDtypeStruct((n_rows, d), F32),
        grid_spec=grid_spec,
        compiler_params=_params(("arbitrary",)),
        name="experts",
    )(tile_expert, tile_valid, tile_first, tile_next, src, h, w_gate, w_up, w_down)


def _combine_kernel(pos_ref, x_ref, w_ref, g_ref, y_ref, o_ref, buf, sem):
    i = pl.program_id(0)
    tm = x_ref.shape[0]
    t = pl.num_programs(0) * tm

    def row_copy(r, slot):
        p = pos_ref[slot * t + i * tm + r]
        return pltpu.make_async_copy(y_ref.at[pl.ds(p, 1)], buf.at[slot, pl.ds(r, 1)], sem)

    def start(r8, carry):
        for k in range(8):
            r = r8 * 8 + k
            row_copy(r, 0).start(priority=0)
            row_copy(r, 1).start(priority=1)
        return carry

    lax.fori_loop(0, tm // 8, start, 0)
    for slot in range(TOP_K):
        pltpu.make_async_copy(y_ref.at[pl.ds(0, tm)], buf.at[slot], sem).wait()
    w = w_ref[...]
    y = w[:, 0:1] * buf[0] + w[:, 1:2] * buf[1]
    o_ref[...] = x_ref[...] + g_ref[0] * y


def _combine(pos_flat, x, wsel_t, gate, y, seq):
    t, d = x.shape
    tm = ROW_TILE
    grid_spec = pltpu.PrefetchScalarGridSpec(
        num_scalar_prefetch=1,
        grid=(t // tm,),
        in_specs=[
            pl.BlockSpec((tm, d), lambda i, p: (i, 0)),
            pl.BlockSpec((tm, TOP_K), lambda i, p: (i, 0)),
            pl.BlockSpec((1, 1, d), lambda i, p: (i * tm // seq, 0, 0)),
            pl.BlockSpec(memory_space=pl.ANY),
        ],
        out_specs=pl.BlockSpec((tm, d), lambda i, p: (i, 0)),
        scratch_shapes=[pltpu.VMEM((TOP_K, tm, d), F32), pltpu.SemaphoreType.DMA(())],
    )
    return pl.pallas_call(
        _combine_kernel,
        out_shape=jax.ShapeDtypeStruct((t, d), F32),
        grid_spec=grid_spec,
        compiler_params=_params(("arbitrary",)),
        name="combine",
    )(pos_flat, x, wsel_t, gate, y)


def _moe(x, g, shift, scale, gate, rw_t, rbias, w_gate, w_up, w_down, layer, seq):
    t, d = x.shape
    te = EXPERT_TILE
    n_rows = TOP_K * t + N_EXPERTS * te
    n_tiles = n_rows // te
    h, eidx, wsel, rank, cnt = _router(x, g, shift, scale, rw_t, rbias, seq)
    counts = cnt[:, 0]
    padded = ((counts + te - 1) // te) * te
    ends = jnp.cumsum(padded)
    starts = ends - padded
    pos = rank
    for e in range(N_EXPERTS):
        pos = pos + jnp.where(eidx == e, starts[e], 0)
    pos_flat = pos.reshape(-1)
    tile_start = jnp.arange(n_tiles, dtype=jnp.int32) * te
    tile_expert = jnp.minimum(jnp.sum(tile_start[:, None] >= ends[None, :], axis=1), N_EXPERTS - 1).astype(jnp.int32)
    tile_valid = (tile_start < ends[-1]).astype(jnp.int32)
    tile_expert = jnp.where(tile_valid > 0, tile_expert, tile_expert[jnp.maximum(ends[-1] // te - 1, 0)])
    prev_expert = jnp.concatenate([jnp.full((1,), -1, jnp.int32), tile_expert[:-1]])
    tile_first = ((tile_valid > 0) & (tile_expert != prev_expert)).astype(jnp.int32)
    later = (tile_expert[None, :] > tile_expert[:, None]) & (tile_valid[None, :] > 0)
    tile_next = jnp.min(jnp.where(later, tile_expert[None, :], N_EXPERTS), axis=1)
    tile_next = jnp.where(tile_next < N_EXPERTS, tile_next, -1).astype(jnp.int32)
    tok = jnp.tile(jnp.arange(t, dtype=jnp.int32), TOP_K)
    src = jnp.zeros((n_rows + te,), jnp.int32).at[pos_flat].set(tok, unique_indices=True)
    y = _experts(tile_expert, tile_valid, tile_first, tile_next, src, h, w_gate, w_up, w_down, layer, n_rows)
    return _combine(pos_flat, x, wsel.T, gate, y, seq)


def kernel(x, c, positions, ada_w, ada_b, norm1_g, norm2_g, ab_w_in, q_norm_g, k_norm_g, sinks, gm_vnorm_g, gm_ws, gm_b, ab_w_out, c_w_in, c_lower_bounds, c_onorm_g, c_w_out, router_w, router_bias, moe_w_gate, moe_w_up, moe_w_down):
    bsz, seq, d = x.shape
    depth = ada_w.shape[0]
    t = bsz * seq
    xt = x.reshape(t, d)

    c_pad = jnp.concatenate([c, jnp.zeros((8 - bsz % 8, d), F32)], axis=0) if bsz % 8 else c
    mod = _ada_mod(c_pad, ada_w, ada_b)[:, :bsz]
    mod = mod.reshape(depth, bsz, 6, 1, d)
    cos_t, sin_t = _rope_tables(positions)

    sm = jax.nn.softmax(c_lower_bounds.astype(F32), axis=0)
    lower_bounds = jnp.cumsum(sm, axis=0) - sm[0:1]
    rw_t = router_w.T
    ab_in, ab_out, c_in, c_out = (w.astype(BF16) for w in (ab_w_in, ab_w_out, c_w_in, c_w_out))

    for l in range(depth):
        sh1, sc1, g1, sh2, sc2, g2 = (mod[l, :, k] for k in range(6))
        if l % 2 == 0:
            i = l // 2
            proj = _in_proj(xt, norm1_g[l], sh1, sc1, ab_in, i, seq)
            mix = _mix_even(proj, cos_t, sin_t, q_norm_g[i], k_norm_g[i], sinks[i],
                            gm_vnorm_g[i], gm_ws[i], gm_b[i], bsz, seq)
            xt = _out_proj(mix, ab_out, i, xt, g1, seq)
        else:
            j = l // 2
            proj = _in_proj(xt, norm1_g[l], sh1, sc1, c_in, j, seq)
            o = _hgrn(proj, lower_bounds[l], c_onorm_g[j], bsz, seq)
            xt = _out_proj(o, c_out, j, xt, g1, seq)
        xt = _moe(xt, norm2_g[l], sh2, sc2, g2, rw_t, router_bias,
                  moe_w_gate, moe_w_up, moe_w_down, l, seq)
    return xt.reshape(bsz, seq, d)
```

```python
import functools

import jax
import jax.numpy as jnp
from jax import lax
from jax.experimental import pallas as pl
from jax.experimental.pallas import tpu as pltpu

F32 = jnp.float32
BF16 = jnp.bfloat16

EPS = 1e-6
HEAD_DIM = 64
KV_REP = 4
WINDOW = 128
ROPE_THETA = 10000.0
B_GROUP_DIM = 128
B_CHUNK = 128
C_KDIM = 128
C_VDIM = 128
C_CHUNK = 64
N_GROUPS = 4
EXPERTS_PER_GROUP = 4
N_EXPERTS = N_GROUPS * EXPERTS_PER_GROUP
TOP_K = 2

LANES = 128
SUBLANES = 8
VMEM_LIMIT = 56 * 1024 * 1024

ROW_TILE = 512
PROJ_ROW_TILE = 1024
PROJ_COL_TILE = 2048
EXPERT_TILE = 256
WEIGHT_DMA_PRIORITY = 1
HGRN_BLOCK = 16
HGRN_MAX_BLOCK_DECAY = 100.0


def _params(sem):
    return pltpu.CompilerParams(dimension_semantics=sem, vmem_limit_bytes=VMEM_LIMIT)


def _ada_kernel(c_ref, w_ref, b_ref, o_ref):
    c = c_ref[...]
    cond = c * jax.nn.sigmoid(c)
    o_ref[0] = jnp.dot(cond, w_ref[0], preferred_element_type=F32) + b_ref[0]


def _ada_mod(c_pad, ada_w, ada_b):
    depth, d, n = ada_w.shape
    tn = 1536
    rows = c_pad.shape[0]
    return pl.pallas_call(
        _ada_kernel,
        out_shape=jax.ShapeDtypeStruct((depth, rows, n), F32),
        grid=(depth, n // tn),
        in_specs=[
            pl.BlockSpec((rows, d), lambda l, j: (0, 0)),
            pl.BlockSpec((1, d, tn), lambda l, j: (l, 0, j)),
            pl.BlockSpec((1, 1, tn), lambda l, j: (l, 0, j)),
        ],
        out_specs=pl.BlockSpec((1, rows, tn), lambda l, j: (l, 0, j)),
        compiler_params=_params(("arbitrary", "arbitrary")),
        name="ada_mod",
    )(c_pad, ada_w, ada_b.reshape(depth, 1, n))


def _rope_table_kernel(pos_ref, freq_ref, sign_ref, cos_ref, sin_ref):
    ang = pos_ref[...].astype(F32) * freq_ref[...]
    cos_ref[...] = jnp.cos(ang)
    sin_ref[...] = jnp.sin(ang) * sign_ref[...]


def _rope_tables(positions):
    t = positions.size
    half = HEAD_DIM // 2
    inv_freq = ROPE_THETA ** (-jnp.arange(half, dtype=F32) / half)
    freq = jnp.tile(inv_freq, LANES // half).reshape(1, LANES)
    sign = jnp.tile(jnp.concatenate([-jnp.ones((half,), F32), jnp.ones((half,), F32)]),
                    LANES // HEAD_DIM).reshape(1, LANES)
    tm = min(1024, t)
    return pl.pallas_call(
        _rope_table_kernel,
        out_shape=(jax.ShapeDtypeStruct((t, LANES), F32), jax.ShapeDtypeStruct((t, LANES), F32)),
        grid=(t // tm,),
        in_specs=[
            pl.BlockSpec((tm, 1), lambda i: (i, 0)),
            pl.BlockSpec((1, LANES), lambda i: (0, 0)),
            pl.BlockSpec((1, LANES), lambda i: (0, 0)),
        ],
        out_specs=(pl.BlockSpec((tm, LANES), lambda i: (i, 0)),
                   pl.BlockSpec((tm, LANES), lambda i: (i, 0))),
        compiler_params=_params(("arbitrary",)),
        name="rope_tables",
    )(positions.reshape(t, 1), freq, sign)


def _norm_mod(x, g, shift, scale):
    y = x * lax.rsqrt(jnp.mean(x * x, axis=-1, keepdims=True) + EPS)
    return (y * g) * (1 + scale) + shift


def _in_proj_kernel(x_ref, g_ref, sh_ref, sc_ref, w_ref, o_ref, h_scr):
    @pl.when(pl.program_id(1) == 0)
    def _():
        h_scr[...] = _norm_mod(x_ref[...], g_ref[...], sh_ref[0], sc_ref[0]).astype(BF16)

    o_ref[...] = jnp.dot(h_scr[...], w_ref[0], preferred_element_type=F32).astype(o_ref.dtype)


def _in_proj(x, g, shift, scale, w, layer, seq):
    t, d = x.shape
    n = w.shape[2]
    tm = PROJ_ROW_TILE
    tn = max(c for c in range(LANES, PROJ_COL_TILE + 1, LANES) if n % c == 0)
    assert seq % tm == 0 and t % tm == 0
    return pl.pallas_call(
        _in_proj_kernel,
        out_shape=jax.ShapeDtypeStruct((t, n), BF16),
        grid=(t // tm, n // tn),
        in_specs=[
            pl.BlockSpec((tm, d), lambda i, j: (i, 0)),
            pl.BlockSpec((1, d), lambda i, j: (0, 0)),
            pl.BlockSpec((1, 1, d), lambda i, j: (i * tm // seq, 0, 0)),
            pl.BlockSpec((1, 1, d), lambda i, j: (i * tm // seq, 0, 0)),
            pl.BlockSpec((1, d, tn), lambda i, j: (layer, 0, j)),
        ],
        out_specs=pl.BlockSpec((tm, tn), lambda i, j: (i, j)),
        scratch_shapes=[pltpu.VMEM((tm, d), BF16)],
        compiler_params=_params(("arbitrary", "arbitrary")),
        name="in_proj",
    )(x, g.reshape(1, d), shift, scale, w)


def _out_proj_kernel(a_ref, w_ref, x_ref, g_ref, o_ref):
    y = jnp.dot(a_ref[...], w_ref[0], preferred_element_type=F32)
    o_ref[...] = x_ref[...] + g_ref[0] * y


def _out_proj(a, w, layer, x, gate, seq):
    t, k = a.shape
    n = w.shape[2]
    tm, tn = PROJ_ROW_TILE, 1024
    assert seq % tm == 0 and t % tm == 0 and n % tn == 0
    return pl.pallas_call(
        _out_proj_kernel,
        out_shape=jax.ShapeDtypeStruct((t, n), F32),
        grid=(t // tm, n // tn),
        in_specs=[
            pl.BlockSpec((tm, k), lambda i, j: (i, 0)),
            pl.BlockSpec((1, k, tn), lambda i, j: (layer, 0, j)),
            pl.BlockSpec((tm, tn), lambda i, j: (i, j)),
            pl.BlockSpec((1, 1, tn), lambda i, j: (i * tm // seq, 0, j)),
        ],
        out_specs=pl.BlockSpec((tm, tn), lambda i, j: (i, j)),
        compiler_params=_params(("arbitrary", "arbitrary")),
        name="out_proj",
    )(a, w, x, gate)


def _segment_mean_matrix():
    r = lax.broadcasted_iota(jnp.int32, (LANES, LANES), 0) // HEAD_DIM
    c = lax.broadcasted_iota(jnp.int32, (LANES, LANES), 1) // HEAD_DIM
    return jnp.where(r == c, 1.0 / HEAD_DIM, 0.0).astype(F32)


def _head_norm_rope(x, gain, cos, sin, seg_mean, low_half):
    ms = jnp.dot(x * x, seg_mean, preferred_element_type=F32)
    y = x * lax.rsqrt(ms + EPS) * gain
    partner = jnp.where(low_half, pltpu.roll(y, LANES - HEAD_DIM // 2, 1),
                        pltpu.roll(y, HEAD_DIM // 2, 1))
    return y * cos + partner * sin


def _gelu_tanh(x):
    return 0.5 * x * (1.0 + jnp.tanh(0.7978845608028654 * (x + 0.044715 * (x * x * x))))


def _mix_kernel(sinks_ref,
                q0_ref, q1_ref, kv_ref, u0_ref, u1_ref, v0_ref, v1_ref,
                cos_ref, sin_ref, qg_ref, kg_ref, vng_ref, ws_ref, bs_ref,
                o_ref, kd_scr, vd_scr):
    n = pl.program_id(1)
    w = WINDOW
    n_kv = kd_scr.shape[0]

    @pl.when(n == 0)
    def _():
        kd_scr[...] = jnp.zeros_like(kd_scr)
        vd_scr[...] = jnp.zeros_like(vd_scr)

    lane = lax.broadcasted_iota(jnp.int32, (w, LANES), 1)
    low_half = (lane % HEAD_DIM) < (HEAD_DIM // 2)
    first_head = lane < HEAD_DIM
    seg_mean = _segment_mean_matrix()
    cos = cos_ref[...]
    sin = sin_ref[...]

    kv = kv_ref[...].astype(F32)
    for c in range(n_kv // 2):
        kc = _head_norm_rope(kv[:, c * LANES:(c + 1) * LANES], kg_ref[...], cos, sin, seg_mean, low_half)
        vc = kv[:, (n_kv // 2 + c) * LANES:(n_kv // 2 + c + 1) * LANES]
        kc_sw = pltpu.roll(kc, HEAD_DIM, 1)
        vc_sw = pltpu.roll(vc, HEAD_DIM, 1)
        kd_scr[2 * c, w:, :] = jnp.where(first_head, kc, kc_sw).astype(BF16)
        kd_scr[2 * c + 1, w:, :] = jnp.where(first_head, kc_sw, kc).astype(BF16)
        vd_scr[2 * c, w:, :] = jnp.where(first_head, vc, vc_sw).astype(BF16)
        vd_scr[2 * c + 1, w:, :] = jnp.where(first_head, vc_sw, vc).astype(BF16)

    qi = lax.broadcasted_iota(jnp.int32, (w, 2 * w), 0)
    kj = lax.broadcasted_iota(jnp.int32, (w, 2 * w), 1)
    diff = qi + w - kj
    band = (diff >= 0) & (diff < w) & ((kj >= w) | (n > 0))
    band4 = jnp.concatenate([band] * KV_REP, axis=0)

    scale = HEAD_DIM ** -0.5
    q_refs = (q0_ref, q1_ref)
    chunks_per_ref = q0_ref.shape[1] // LANES
    for g in range(n_kv):
        rows = []
        for cc in range(2):
            c = 2 * g + cc
            qc = q_refs[c // chunks_per_ref][:, (c % chunks_per_ref) * LANES:(c % chunks_per_ref + 1) * LANES]
            qc = _head_norm_rope(qc.astype(F32), qg_ref[...], cos, sin, seg_mean, low_half)
            rows.append(jnp.where(first_head, qc, 0.0))
            rows.append(jnp.where(first_head, 0.0, qc))
        qm = jnp.concatenate(rows, axis=0).astype(BF16)
        s = lax.dot_general(qm, kd_scr[g], (((1,), (1,)), ((), ())),
                            preferred_element_type=F32) * scale
        s = jnp.where(band4, s, -jnp.inf)
        sink = jnp.concatenate(
            [jnp.full((w, 1), sinks_ref[KV_REP * g + r], F32) for r in range(KV_REP)], axis=0)
        m = jnp.maximum(jnp.max(s, axis=-1, keepdims=True), sink)
        p = jnp.exp(s - m)
        p = p / (jnp.sum(p, axis=-1, keepdims=True) + jnp.exp(sink - m))
        o = jnp.dot(p.astype(BF16), vd_scr[g], preferred_element_type=F32)
        for cc in range(2):
            oc = jnp.where(first_head, o[(2 * cc) * w:(2 * cc + 1) * w], o[(2 * cc + 1) * w:(2 * cc + 2) * w])
            c = 2 * g + cc
            o_ref[:, c * LANES:(c + 1) * LANES] = oc.astype(o_ref.dtype)

    kd_scr[:, :w, :] = kd_scr[:, w:, :]
    vd_scr[:, :w, :] = vd_scr[:, w:, :]

    a_q = 2 * q0_ref.shape[1]
    u = _gelu_tanh(jnp.concatenate([u0_ref[...], u1_ref[...]], axis=1).astype(F32))
    v = _gelu_tanh(jnp.concatenate([v0_ref[...], v1_ref[...]], axis=1).astype(F32))
    mu = jnp.mean(v, axis=-1, keepdims=True)
    vc = v - mu
    vn = (vc * lax.rsqrt(jnp.mean(vc * vc, axis=-1, keepdims=True) + EPS) * vng_ref[...]).astype(BF16)
    ti = lax.broadcasted_iota(jnp.int32, (w, w), 0)
    si = lax.broadcasted_iota(jnp.int32, (w, w), 1)
    causal = si <= ti
    bs = bs_ref[...]
    for g in range(ws_ref.shape[0]):
        wg = jnp.where(causal, ws_ref[g], 0.0).astype(BF16)
        sv = jnp.dot(wg, vn[:, g * LANES:(g + 1) * LANES], preferred_element_type=F32) + bs[:, g:g + 1]
        o_ref[:, a_q + g * LANES:a_q + (g + 1) * LANES] = (u[:, g * LANES:(g + 1) * LANES] * sv).astype(o_ref.dtype)


def _mix_even(proj, cos_t, sin_t, q_g, k_g, sinks, vnorm_g, ws, bs, bsz, seq):
    t = proj.shape[0]
    w = WINDOW
    nb = seq // w
    heads = sinks.shape[0]
    n_kv = heads // KV_REP
    a_q = heads * HEAD_DIM
    a_kv = n_kv * HEAD_DIM
    b_w = vnorm_g.shape[0]
    cw = 512
    assert a_q == 2 * cw and 2 * a_kv == cw and b_w == 2 * cw
    row = lambda b, n: b * nb + n
    gain2 = lambda g: jnp.tile(g, LANES // HEAD_DIM).reshape(1, LANES)
    col = lambda j: pl.BlockSpec((w, cw), lambda b, n, s: (row(b, n), j))
    grid_spec = pltpu.PrefetchScalarGridSpec(
        num_scalar_prefetch=1,
        grid=(bsz, nb),
        in_specs=[
            col(0), col(1), col(2), col(3), col(4), col(5), col(6),
            pl.BlockSpec((w, LANES), lambda b, n, s: (row(b, n), 0)),
            pl.BlockSpec((w, LANES), lambda b, n, s: (row(b, n), 0)),
            pl.BlockSpec((1, LANES), lambda b, n, s: (0, 0)),
            pl.BlockSpec((1, LANES), lambda b, n, s: (0, 0)),
            pl.BlockSpec((1, b_w), lambda b, n, s: (0, 0)),
            pl.BlockSpec(ws.shape, lambda b, n, s: (0, 0, 0)),
            pl.BlockSpec((w, ws.shape[0]), lambda b, n, s: (0, 0)),
        ],
        out_specs=pl.BlockSpec((w, a_q + b_w), lambda b, n, s: (row(b, n), 0)),
        scratch_shapes=[pltpu.VMEM((n_kv, 2 * w, LANES), BF16),
                        pltpu.VMEM((n_kv, 2 * w, LANES), BF16)],
    )
    return pl.pallas_call(
        _mix_kernel,
        out_shape=jax.ShapeDtypeStruct((t, a_q + b_w), BF16),
        grid_spec=grid_spec,
        compiler_params=_params(("arbitrary", "arbitrary")),
        name="mix_even",
    )(sinks, proj, proj, proj, proj, proj, proj, proj, cos_t, sin_t,
      gain2(q_g), gain2(k_g), vnorm_g.reshape(1, b_w), ws, bs.T)


def _nt_dot(a, b):
    return lax.dot_general(a, b, (((1,), (1,)), ((), ())), preferred_element_type=F32)


def _hgrn_kernel(q_ref, f_ref, i_ref, gt_ref, lb_ref, og_ref, o_ref,
                 oi_scr, qd_scr, u_scr, d_scr):
    c = C_CHUNK
    nc = q_ref.shape[0] // c
    nv = c // SUBLANES
    lb = lb_ref[...]
    og = og_ref[...]
    t_row = lax.broadcasted_iota(jnp.int32, (c, LANES), 0)
    tt = lax.broadcasted_iota(jnp.int32, (c, c), 0)
    ss = lax.broadcasted_iota(jnp.int32, (c, c), 1)

    def inputs(ci):
        r0 = pl.multiple_of(ci * c, c)
        q = q_ref[pl.ds(r0, c), :].astype(F32)
        fl = f_ref[pl.ds(r0, c), :].astype(F32)
        v = i_ref[pl.ds(r0, c), :]
        f = lb + (1 - lb) * jax.nn.sigmoid(fl)
        k = 1 - f
        b = jnp.log2(f)
        for d in (1, 2, 4, 8, 16, 32):
            b = b + jnp.where(t_row >= d, pltpu.roll(b, d, 0), 0.0)
        return q, f, k, b, v

    def row(b, r, n):
        b3 = b.reshape(nv, SUBLANES, LANES)
        rep = jnp.broadcast_to(b3[r // SUBLANES:r // SUBLANES + 1, r % SUBLANES:r % SUBLANES + 1, :],
                               (n // SUBLANES, SUBLANES, LANES))
        return rep.reshape(n, LANES)

    def half_level(q, k, b, m):
        a_parts, b_parts = [], []
        for blk in range(c // (2 * m)):
            lo, mid, hi = blk * 2 * m, blk * 2 * m + m, (blk + 1) * 2 * m
            zero = jnp.zeros((m, LANES), BF16)
            ref = row(b, mid - 1, m)
            a_parts += [zero, (q[mid:hi] * jnp.exp2(b[mid:hi] - ref)).astype(BF16)]
            b_parts += [(k[lo:mid] * jnp.exp2(ref - b[lo:mid])).astype(BF16), zero]
        return _nt_dot(jnp.concatenate(a_parts, axis=0), jnp.concatenate(b_parts, axis=0))

    def finish(ci, q, k, b, v, scores):
        oi_scr[ci] = jnp.dot(scores.astype(BF16), v, preferred_element_type=F32)
        qd_scr[ci] = (q * jnp.exp2(b)).astype(BF16)
        b_last = b[c - 1:c, :]
        kd = (k * jnp.exp2(b_last - b)).astype(BF16)
        u_scr[ci] = lax.dot_general(v, kd, (((0,), (0,)), ((), ())), preferred_element_type=F32)
        d_scr[ci] = jnp.exp2(b_last)

    def intra_fast(ci, worst):
        q, f, k, b, v = inputs(ci)
        blk = HGRN_BLOCK
        starts = [jnp.zeros((blk, LANES), F32)] + [row(b, s - 1, blk) for s in range(blk, c, blk)]
        base = jnp.concatenate(starts, axis=0)
        s_d = _nt_dot((q * jnp.exp2(b - base)).astype(BF16), (k * jnp.exp2(base - b)).astype(BF16))
        s_16 = half_level(q, k, b, blk)
        s_32 = half_level(q, k, b, 2 * blk)
        scores = jnp.where(((tt // blk) == (ss // blk)) & (ss <= tt), s_d,
                           jnp.where((tt // (2 * blk)) == (ss // (2 * blk)), s_16, s_32))
        finish(ci, q, k, b, v, scores)
        for s in range(0, c, blk):
            worst = jnp.maximum(worst, base[s:s + 1, :] - b[s + blk - 1:s + blk, :])
        return worst

    def intra_safe(ci, carry):
        q, f, k, b, v = inputs(ci)
        scores = jnp.where(tt == ss, jnp.sum(q * k, axis=-1, keepdims=True), 0.0)
        odd = (t_row % 2) == 1
        s_1 = _nt_dot(jnp.where(odd, q * f, 0.0).astype(BF16), jnp.where(odd, 0.0, k).astype(BF16))
        scores = scores + jnp.where((tt // 2) == (ss // 2), s_1, 0.0)
        b3 = b.reshape(nv, SUBLANES, LANES)
        sub = lax.broadcasted_iota(jnp.int32, b3.shape, 1)
        small_refs = {4: jnp.broadcast_to(b3[:, 3:4, :], b3.shape).reshape(c, LANES),
                      2: jnp.where(sub < 4, jnp.broadcast_to(b3[:, 1:2, :], b3.shape),
                                   jnp.broadcast_to(b3[:, 5:6, :], b3.shape)).reshape(c, LANES)}
        for m in (32, 16, 8, 4, 2):
            if m >= SUBLANES:
                s_l = half_level(q, k, b, m)
            else:
                ref = small_refs[m]
                upper = (t_row % (2 * m)) >= m
                s_l = _nt_dot(jnp.where(upper, q * jnp.exp2(b - ref), 0.0).astype(BF16),
                              jnp.where(upper, 0.0, k * jnp.exp2(ref - b)).astype(BF16))
            scores = scores + (s_l if 2 * m == c else jnp.where((tt // (2 * m)) == (ss // (2 * m)), s_l, 0.0))
        finish(ci, q, k, b, v, scores)
        return carry

    worst = lax.fori_loop(0, nc, intra_fast, jnp.zeros((1, LANES), F32), unroll=8)

    @pl.when(jnp.max(worst) > HGRN_MAX_BLOCK_DECAY)
    def _():
        lax.fori_loop(0, nc, intra_safe, 0, unroll=2)

    def inter(ci, state_t):
        r0 = pl.multiple_of(ci * c, c)
        gate = gt_ref[pl.ds(r0, c), :].astype(F32)
        o = oi_scr[ci] + _nt_dot(qd_scr[ci], state_t.astype(BF16))
        y = o * lax.rsqrt(jnp.mean(o * o, axis=-1, keepdims=True) + EPS) * og
        o_ref[pl.ds(r0, c), :] = (y * (gate * jax.nn.sigmoid(gate))).astype(o_ref.dtype)
        return state_t * d_scr[ci] + u_scr[ci]

    lax.fori_loop(0, nc, inter, jnp.zeros((C_VDIM, C_KDIM), F32), unroll=8)


def _hgrn(proj, lb, og, bsz, seq):
    t, n4 = proj.shape
    heads = n4 // (4 * LANES)
    nc = seq // C_CHUNK
    return pl.pallas_call(
        _hgrn_kernel,
        out_shape=jax.ShapeDtypeStruct((t, heads * C_VDIM), BF16),
        grid=(bsz, heads),
        in_specs=[
            pl.BlockSpec((seq, LANES), lambda b, h: (b, h)),
            pl.BlockSpec((seq, LANES), lambda b, h: (b, heads + h)),
            pl.BlockSpec((seq, LANES), lambda b, h: (b, 2 * heads + h)),
            pl.BlockSpec((seq, LANES), lambda b, h: (b, 3 * heads + h)),
            pl.BlockSpec((1, LANES), lambda b, h: (0, h)),
            pl.BlockSpec((1, LANES), lambda b, h: (0, 0)),
        ],
        out_specs=pl.BlockSpec((seq, LANES), lambda b, h: (b, h)),
        scratch_shapes=[pltpu.VMEM((nc, C_CHUNK, C_VDIM), F32),
                        pltpu.VMEM((nc, C_CHUNK, C_KDIM), BF16),
                        pltpu.VMEM((nc, C_VDIM, C_KDIM), F32),
                        pltpu.VMEM((nc, 1, C_KDIM), F32)],
        compiler_params=_params(("arbitrary", "arbitrary")),
        name="hgrn2",
    )(proj, proj, proj, proj, lb.reshape(1, -1), og.reshape(1, LANES))


def _first_max(vals):
    best, idx = vals[0], jnp.zeros(vals[0].shape, jnp.int32)
    for i in range(1, len(vals)):
        better = vals[i] > best
        best = jnp.where(better, vals[i], best)
        idx = jnp.where(better, i, idx)
    return best, idx


def _pick(idx, vals):
    out = vals[-1]
    for i in range(len(vals) - 2, -1, -1):
        out = jnp.where(idx == i, vals[i], out)
    return out


def _router_kernel(x_ref, g_ref, sh_ref, sc_ref, rw_ref, rb_ref,
                   h_ref, eidx_ref, wsel_ref, rank_ref, cnt_ref, carry_scr):
    i = pl.program_id(0)
    tm = x_ref.shape[0]

    @pl.when(i == 0)
    def _():
        carry_scr[...] = jnp.zeros_like(carry_scr)

    h = _norm_mod(x_ref[...], g_ref[...], sh_ref[0], sc_ref[0])
    h_ref[...] = h
    logits = lax.dot_general(rw_ref[...], h, (((1,), (1,)), ((), ())),
                             preferred_element_type=F32, precision=lax.Precision.HIGHEST)
    scores = jax.nn.sigmoid(logits)
    sel = scores + rb_ref[...]
    epg = EXPERTS_PER_GROUP
    sel_rows = [sel[e:e + 1, :] for e in range(N_EXPERTS)]
    sc_rows = [scores[e:e + 1, :] for e in range(N_EXPERTS)]
    gsum = []
    for g in range(N_GROUPS):
        r = sel_rows[g * epg:(g + 1) * epg]
        pair = None
        for a in range(epg):
            for b in range(a + 1, epg):
                pair = r[a] + r[b] if pair is None else jnp.maximum(pair, r[a] + r[b])
        gsum.append(pair)
    _, g_idx = _first_max(gsum)
    cand = [_pick(g_idx, [sel_rows[g * epg + j] for g in range(N_GROUPS)]) for j in range(epg)]
    cand_sc = [_pick(g_idx, [sc_rows[g * epg + j] for g in range(N_GROUPS)]) for j in range(epg)]
    _, i1 = _first_max(cand)
    _, i2 = _first_max([jnp.where(i1 == j, -jnp.inf, cand[j]) for j in range(epg)])
    w1 = _pick(i1, cand_sc)
    w2 = _pick(i2, cand_sc)
    tot = w1 + w2
    e1 = g_idx * epg + i1
    e2 = g_idx * epg + i2
    eidx_ref[0:1, :] = e1
    eidx_ref[1:2, :] = e2
    wsel_ref[0:1, :] = w1 / tot
    wsel_ref[1:2, :] = w2 / tot

    erow = lax.broadcasted_iota(jnp.int32, (N_EXPERTS, tm), 0)
    oh1 = erow == e1
    oh2 = erow == e2
    onehot = jnp.where(oh1 | oh2, 1.0, 0.0)
    si = lax.broadcasted_iota(jnp.int32, (tm, tm), 0)
    ti = lax.broadcasted_iota(jnp.int32, (tm, tm), 1)
    incl = jnp.where(si <= ti, 1.0, 0.0).astype(BF16)
    csum = jnp.dot(onehot.astype(BF16), incl, preferred_element_type=F32)
    carry = carry_scr[...]
    before = csum - onehot + carry[:, 0:1]
    r1 = jnp.sum(jnp.where(oh1, before, 0.0), axis=0, keepdims=True)
    r2 = jnp.sum(jnp.where(oh2, before, 0.0), axis=0, keepdims=True)
    rank_ref[0:1, :] = r1.astype(jnp.int32)
    rank_ref[1:2, :] = r2.astype(jnp.int32)
    new_carry = carry + csum[:, tm - 1:tm]
    carry_scr[...] = new_carry
    cnt_ref[...] = new_carry.astype(jnp.int32)


def _router(x, g, shift, scale, rw_t, rbias, seq):
    t, d = x.shape
    tm = ROW_TILE
    e = rw_t.shape[0]
    row = lambda i: (0, i)
    return pl.pallas_call(
        _router_kernel,
        out_shape=(jax.ShapeDtypeStruct((t, d), F32),
                   jax.ShapeDtypeStruct((TOP_K, t), jnp.int32),
                   jax.ShapeDtypeStruct((TOP_K, t), F32),
                   jax.ShapeDtypeStruct((TOP_K, t), jnp.int32),
                   jax.ShapeDtypeStruct((e, LANES), jnp.int32)),
        grid=(t // tm,),
        in_specs=[
            pl.BlockSpec((tm, d), lambda i: (i, 0)),
            pl.BlockSpec((1, d), lambda i: (0, 0)),
            pl.BlockSpec((1, 1, d), lambda i: (i * tm // seq, 0, 0)),
            pl.BlockSpec((1, 1, d), lambda i: (i * tm // seq, 0, 0)),
            pl.BlockSpec((e, d), lambda i: (0, 0)),
            pl.BlockSpec((e, 1), lambda i: (0, 0)),
        ],
        out_specs=(pl.BlockSpec((tm, d), lambda i: (i, 0)),
                   pl.BlockSpec((TOP_K, tm), row),
                   pl.BlockSpec((TOP_K, tm), row),
                   pl.BlockSpec((TOP_K, tm), row),
                   pl.BlockSpec((e, LANES), lambda i: (0, 0))),
        scratch_shapes=[pltpu.VMEM((e, LANES), F32)],
        compiler_params=_params(("arbitrary",)),
        name="router",
    )(x, g.reshape(1, d), shift, scale, rw_t, rbias.reshape(e, 1))


def _invert_kernel(pos_ref, fill_ref, src_ref, *, n_tok):
    def clear(p, carry):
        src_ref[p] = 0
        return carry

    for e in range(N_EXPERTS + 1):
        lax.fori_loop(fill_ref[2 * e], fill_ref[2 * e + 1], clear, 0)

    for slot in range(TOP_K):
        def place(tok, carry, slot=slot):
            src_ref[pos_ref[slot * n_tok + tok]] = tok
            return carry

        lax.fori_loop(0, n_tok, place, 0, unroll=8)


def _invert_positions(pos_flat, fill, n_rows, n_tok):
    assert pos_flat.shape[0] == TOP_K * n_tok
    return pl.pallas_call(
        functools.partial(_invert_kernel, n_tok=n_tok),
        out_shape=jax.ShapeDtypeStruct((n_rows,), jnp.int32),
        in_specs=[pl.BlockSpec(memory_space=pltpu.SMEM), pl.BlockSpec(memory_space=pltpu.SMEM)],
        out_specs=pl.BlockSpec(memory_space=pltpu.SMEM),
        name="invert_positions",
    )(pos_flat, fill)


def _expert_kernel(te_ref, tv_ref, tf_ref, tn_ref, src_ref, h_hbm, wg_hbm, wu_hbm, wd_hbm, y_ref,
                   xbuf0, xbuf1, sg, su, sd, bg, bu, bd, gsem, wsem, *, layer):
    i = pl.program_id(0)
    xbufs = (xbuf0, xbuf1)
    tm = xbuf0.shape[0]

    def gather(tile, buf):
        for r in range(tm):
            tok = src_ref[tile * tm + r]
            pltpu.make_async_copy(h_hbm.at[pl.ds(tok, 1)], xbufs[buf].at[pl.ds(r, 1)], gsem.at[buf]).start()

    def fetch(e):
        return (pltpu.make_async_copy(wg_hbm.at[layer, e], sg, wsem.at[0]),
                pltpu.make_async_copy(wu_hbm.at[layer, e], su, wsem.at[1]),
                pltpu.make_async_copy(wd_hbm.at[layer, e], sd, wsem.at[2]))

    @pl.when(i == 0)
    def _():
        for c in fetch(te_ref[0]):
            c.start(priority=WEIGHT_DMA_PRIORITY)
        gather(0, 0)

    pending = (i == 0) | (tv_ref[jnp.maximum(i - 1, 0)] > 0)
    for par in range(2):
        @pl.when(pending & (i % 2 == par))
        def _(par=par):
            pltpu.make_async_copy(h_hbm.at[pl.ds(0, tm)], xbufs[par], gsem.at[par]).wait()

    @pl.when(tf_ref[i] > 0)
    def _():
        for c in fetch(te_ref[i]):
            c.wait()
        bg[...] = sg[...].astype(BF16)
        bu[...] = su[...].astype(BF16)
        bd[...] = sd[...].astype(BF16)

        @pl.when(tn_ref[i] >= 0)
        def _():
            for c in fetch(tn_ref[i]):
                c.start(priority=WEIGHT_DMA_PRIORITY)

    for par in range(2):
        @pl.when((tv_ref[i] > 0) & (i % 2 == par))
        def _(par=par):
            gather(i + 1, 1 - par)
            x = xbufs[par][...].astype(BF16)
            a = jnp.dot(x, bg[...], preferred_element_type=F32)
            u = jnp.dot(x, bu[...], preferred_element_type=F32)
            hid = (a * jax.nn.sigmoid(a) * u).astype(BF16)
            y_ref[...] = jnp.dot(hid, bd[...], preferred_element_type=F32)

    @pl.when(tv_ref[i] == 0)
    def _():
        y_ref[...] = jnp.zeros_like(y_ref)


def _experts(tile_expert, tile_valid, tile_first, tile_next, src, h, w_gate, w_up, w_down, layer, n_rows):
    d = h.shape[1]
    f = w_gate.shape[3]
    tm = EXPERT_TILE
    grid_spec = pltpu.PrefetchScalarGridSpec(
        num_scalar_prefetch=5,
        grid=(n_rows // tm,),
        in_specs=[pl.BlockSpec(memory_space=pl.ANY)] * 4,
        out_specs=pl.BlockSpec((tm, d), lambda i, *_: (i, 0)),
        scratch_shapes=[pltpu.VMEM((tm, d), F32), pltpu.VMEM((tm, d), F32),
                        pltpu.VMEM((d, f), F32), pltpu.VMEM((d, f), F32), pltpu.VMEM((f, d), F32),
                        pltpu.VMEM((d, f), BF16), pltpu.VMEM((d, f), BF16), pltpu.VMEM((f, d), BF16),
                        pltpu.SemaphoreType.DMA((2,)), pltpu.SemaphoreType.DMA((3,))],
    )
    return pl.pallas_call(
        functools.partial(_expert_kernel, layer=layer),
        out_shape=jax.ShapeDtypeStruct((n_rows, d), F32),
        grid_spec=grid_spec,
        compiler_params=_params(("arbitrary",)),
        name="experts",
    )(tile_expert, tile_valid, tile_first, tile_next, src, h, w_gate, w_up, w_down)


def _combine_kernel(pos_ref, x_ref, w_ref, g_ref, y_ref, o_ref, buf0, buf1, sem):
    i = pl.program_id(0)
    n = pl.num_programs(0)
    tm = x_ref.shape[0]
    t = n * tm
    bufs = (buf0, buf1)

    def gather(tile, par):
        for r in range(tm):
            for slot in range(TOP_K):
                p = pos_ref[slot * t + tile * tm + r]
                pltpu.make_async_copy(y_ref.at[pl.ds(p, 1)], bufs[par].at[slot, pl.ds(r, 1)],
                                      sem.at[par]).start(priority=slot)

    def combine(par):
        w = w_ref[...]
        y = w[:, 0:1] * bufs[par][0] + w[:, 1:2] * bufs[par][1]
        o_ref[...] = x_ref[...] + g_ref[0] * y

    @pl.when(i == 0)
    def _():
        gather(0, 0)

    for par in range(2):
        @pl.when(i % 2 == par)
        def _(par=par):
            for slot in range(TOP_K):
                pltpu.make_async_copy(y_ref.at[pl.ds(0, tm)], bufs[par].at[slot], sem.at[par]).wait()

        @pl.when((i % 2 == par) & (i + 1 < n))
        def _(par=par):
            gather(i + 1, 1 - par)
            combine(par)

        @pl.when((i % 2 == par) & (i + 1 == n))
        def _(par=par):
            combine(par)


def _combine(pos_flat, x, wsel_t, gate, y, seq):
    t, d = x.shape
    tm = ROW_TILE
    assert seq % tm == 0 and t % tm == 0
    grid_spec = pltpu.PrefetchScalarGridSpec(
        num_scalar_prefetch=1,
        grid=(t // tm,),
        in_specs=[
            pl.BlockSpec((tm, d), lambda i, p: (i, 0)),
            pl.BlockSpec((tm, TOP_K), lambda i, p: (i, 0)),
            pl.BlockSpec((1, 1, d), lambda i, p: (i * tm // seq, 0, 0)),
            pl.BlockSpec(memory_space=pl.ANY),
        ],
        out_specs=pl.BlockSpec((tm, d), lambda i, p: (i, 0)),
        scratch_shapes=[pltpu.VMEM((TOP_K, tm, d), F32), pltpu.VMEM((TOP_K, tm, d), F32),
                        pltpu.SemaphoreType.DMA((2,))],
    )
    return pl.pallas_call(
        _combine_kernel,
        out_shape=jax.ShapeDtypeStruct((t, d), F32),
        grid_spec=grid_spec,
        compiler_params=_params(("arbitrary",)),
        name="combine",
    )(pos_flat, x, wsel_t, gate, y)


def _moe(x, g, shift, scale, gate, rw_t, rbias, w_gate, w_up, w_down, layer, seq):
    t, d = x.shape
    te = EXPERT_TILE
    n_rows = TOP_K * t + N_EXPERTS * te
    n_tiles = n_rows // te
    h, eidx, wsel, rank, cnt = _router(x, g, shift, scale, rw_t, rbias, seq)
    counts = cnt[:, 0]
    padded = ((counts + te - 1) // te) * te
    ends = jnp.cumsum(padded)
    starts = ends - padded
    pos = rank
    for e in range(N_EXPERTS):
        pos = pos + jnp.where(eidx == e, starts[e], 0)
    pos_flat = pos.reshape(-1)
    tile_start = jnp.arange(n_tiles, dtype=jnp.int32) * te
    tile_expert = jnp.minimum(jnp.sum(tile_start[:, None] >= ends[None, :], axis=1), N_EXPERTS - 1).astype(jnp.int32)
    tile_valid = (tile_start < ends[-1]).astype(jnp.int32)
    tile_expert = jnp.where(tile_valid > 0, tile_expert, tile_expert[jnp.maximum(ends[-1] // te - 1, 0)])
    prev_expert = jnp.concatenate([jnp.full((1,), -1, jnp.int32), tile_expert[:-1]])
    tile_first = ((tile_valid > 0) & (tile_expert != prev_expert)).astype(jnp.int32)
    later = (tile_expert[None, :] > tile_expert[:, None]) & (tile_valid[None, :] > 0)
    tile_next = jnp.min(jnp.where(later, tile_expert[None, :], N_EXPERTS), axis=1)
    tile_next = jnp.where(tile_next < N_EXPERTS, tile_next, -1).astype(jnp.int32)
    fill = jnp.stack([jnp.concatenate([starts + counts, ends[-1:]]),
                      jnp.concatenate([ends, jnp.full((1,), n_rows, ends.dtype)])], axis=1).reshape(-1).astype(jnp.int32)
    src = _invert_positions(pos_flat, fill, n_rows, t)
    y = _experts(tile_expert, tile_valid, tile_first, tile_next, src, h, w_gate, w_up, w_down, layer, n_rows)
    return _combine(pos_flat, x, wsel.T, gate, y, seq)


def kernel(x, c, positions, ada_w, ada_b, norm1_g, norm2_g, ab_w_in, q_norm_g, k_norm_g, sinks, gm_vnorm_g, gm_ws, gm_b, ab_w_out, c_w_in, c_lower_bounds, c_onorm_g, c_w_out, router_w, router_bias, moe_w_gate, moe_w_up, moe_w_down):
    bsz, seq, d = x.shape
    depth = ada_w.shape[0]
    t = bsz * seq
    xt = x.reshape(t, d)

    c_pad = jnp.concatenate([c, jnp.zeros((8 - bsz % 8, d), F32)], axis=0) if bsz % 8 else c
    mod = _ada_mod(c_pad, ada_w, ada_b)[:, :bsz]
    mod = mod.reshape(depth, bsz, 6, 1, d)
    cos_t, sin_t = _rope_tables(positions)

    sm = jax.nn.softmax(c_lower_bounds.astype(F32), axis=0)
    lower_bounds = jnp.cumsum(sm, axis=0) - sm[0:1]
    rw_t = router_w.T
    ab_in, ab_out, c_in, c_out = (w.astype(BF16) for w in (ab_w_in, ab_w_out, c_w_in, c_w_out))

    for l in range(depth):
        sh1, sc1, g1, sh2, sc2, g2 = (mod[l, :, k] for k in range(6))
        if l % 2 == 0:
            i = l // 2
            proj = _in_proj(xt, norm1_g[l], sh1, sc1, ab_in, i, seq)
            mix = _mix_even(proj, cos_t, sin_t, q_norm_g[i], k_norm_g[i], sinks[i],
                            gm_vnorm_g[i], gm_ws[i], gm_b[i], bsz, seq)
            xt = _out_proj(mix, ab_out, i, xt, g1, seq)
        else:
            j = l // 2
            proj = _in_proj(xt, norm1_g[l], sh1, sc1, c_in, j, seq)
            o = _hgrn(proj, lower_bounds[l], c_onorm_g[j], bsz, seq)
            xt = _out_proj(o, c_out, j, xt, g1, seq)
        xt = _moe(xt, norm2_g[l], sh2, sc2, g2, rw_t, router_bias,
                  moe_w_gate, moe_w_up, moe_w_down, l, seq)
    return xt.reshape(bsz, seq, d)
```

```python
import functools

import jax
import jax.numpy as jnp
from jax import lax
from jax.experimental import pallas as pl
from jax.experimental.pallas import tpu as pltpu

F32 = jnp.float32
BF16 = jnp.bfloat16

EPS = 1e-6
HEAD_DIM = 64
KV_REP = 4
WINDOW = 128
ROPE_THETA = 10000.0
B_GROUP_DIM = 128
B_CHUNK = 128
C_KDIM = 128
C_VDIM = 128
C_CHUNK = 64
N_GROUPS = 4
EXPERTS_PER_GROUP = 4
N_EXPERTS = N_GROUPS * EXPERTS_PER_GROUP
TOP_K = 2

LANES = 128
SUBLANES = 8
VMEM_LIMIT = 56 * 1024 * 1024

ROW_TILE = 512
PROJ_ROW_TILE = 1024
PROJ_COL_TILE = 2048
EXPERT_TILE = 256
WEIGHT_DMA_PRIORITY = 1
HGRN_BLOCK = 16
HGRN_MAX_BLOCK_DECAY = 100.0


def _nt_dot(a, b):
    return lax.dot_general(a, b, (((1,), (1,)), ((), ())), preferred_element_type=F32)


def _params(sem):
    return pltpu.CompilerParams(dimension_semantics=sem, vmem_limit_bytes=VMEM_LIMIT)


def _ada_kernel(c_ref, w_ref, b_ref, o_ref):
    c = c_ref[...]
    cond = c * jax.nn.sigmoid(c)
    o_ref[0] = jnp.dot(cond, w_ref[0], preferred_element_type=F32) + b_ref[0]


def _ada_mod(c_pad, ada_w, ada_b):
    depth, d, n = ada_w.shape
    tn = 1536
    rows = c_pad.shape[0]
    return pl.pallas_call(
        _ada_kernel,
        out_shape=jax.ShapeDtypeStruct((depth, rows, n), F32),
        grid=(depth, n // tn),
        in_specs=[
            pl.BlockSpec((rows, d), lambda l, j: (0, 0)),
            pl.BlockSpec((1, d, tn), lambda l, j: (l, 0, j)),
            pl.BlockSpec((1, 1, tn), lambda l, j: (l, 0, j)),
        ],
        out_specs=pl.BlockSpec((1, rows, tn), lambda l, j: (l, 0, j)),
        compiler_params=_params(("arbitrary", "arbitrary")),
        name="ada_mod",
    )(c_pad, ada_w, ada_b.reshape(depth, 1, n))


def _rope_table_kernel(pos_ref, freq_ref, sign_ref, cos_ref, sin_ref):
    ang = pos_ref[...].astype(F32) * freq_ref[...]
    cos_ref[...] = jnp.cos(ang)
    sin_ref[...] = jnp.sin(ang) * sign_ref[...]


def _rope_tables(positions):
    t = positions.size
    half = HEAD_DIM // 2
    inv_freq = ROPE_THETA ** (-jnp.arange(half, dtype=F32) / half)
    freq = jnp.tile(inv_freq, LANES // half).reshape(1, LANES)
    sign = jnp.tile(jnp.concatenate([-jnp.ones((half,), F32), jnp.ones((half,), F32)]),
                    LANES // HEAD_DIM).reshape(1, LANES)
    tm = min(1024, t)
    return pl.pallas_call(
        _rope_table_kernel,
        out_shape=(jax.ShapeDtypeStruct((t, LANES), F32), jax.ShapeDtypeStruct((t, LANES), F32)),
        grid=(t // tm,),
        in_specs=[
            pl.BlockSpec((tm, 1), lambda i: (i, 0)),
            pl.BlockSpec((1, LANES), lambda i: (0, 0)),
            pl.BlockSpec((1, LANES), lambda i: (0, 0)),
        ],
        out_specs=(pl.BlockSpec((tm, LANES), lambda i: (i, 0)),
                   pl.BlockSpec((tm, LANES), lambda i: (i, 0))),
        compiler_params=_params(("arbitrary",)),
        name="rope_tables",
    )(positions.reshape(t, 1), freq, sign)


def _norm_mod(x, g, shift, scale):
    y = x * lax.rsqrt(jnp.mean(x * x, axis=-1, keepdims=True) + EPS)
    return (y * g) * (1 + scale) + shift


def _in_proj_kernel(x_ref, g_ref, sh_ref, sc_ref, w_ref, o_ref, h_scr):
    @pl.when(pl.program_id(1) == 0)
    def _():
        h_scr[...] = _norm_mod(x_ref[...], g_ref[...], sh_ref[0], sc_ref[0]).astype(BF16)

    o_ref[...] = jnp.dot(h_scr[...], w_ref[0], preferred_element_type=F32).astype(o_ref.dtype)


def _in_proj(x, g, shift, scale, w, layer, seq):
    t, d = x.shape
    n = w.shape[2]
    tm = PROJ_ROW_TILE
    tn = max(c for c in range(LANES, PROJ_COL_TILE + 1, LANES) if n % c == 0)
    assert seq % tm == 0 and t % tm == 0
    return pl.pallas_call(
        _in_proj_kernel,
        out_shape=jax.ShapeDtypeStruct((t, n), BF16),
        grid=(t // tm, n // tn),
        in_specs=[
            pl.BlockSpec((tm, d), lambda i, j: (i, 0)),
            pl.BlockSpec((1, d), lambda i, j: (0, 0)),
            pl.BlockSpec((1, 1, d), lambda i, j: (i * tm // seq, 0, 0)),
            pl.BlockSpec((1, 1, d), lambda i, j: (i * tm // seq, 0, 0)),
            pl.BlockSpec((1, d, tn), lambda i, j: (layer, 0, j)),
        ],
        out_specs=pl.BlockSpec((tm, tn), lambda i, j: (i, j)),
        scratch_shapes=[pltpu.VMEM((tm, d), BF16)],
        compiler_params=_params(("arbitrary", "arbitrary")),
        name="in_proj",
    )(x, g.reshape(1, d), shift, scale, w)


def _out_proj_kernel(a_ref, w_ref, x_ref, g_ref, o_ref):
    y = jnp.dot(a_ref[...], w_ref[0], preferred_element_type=F32)
    o_ref[...] = x_ref[...] + g_ref[0] * y


def _out_proj(a, w, layer, x, gate, seq):
    t, k = a.shape
    n = w.shape[2]
    tm, tn = PROJ_ROW_TILE, 1024
    assert seq % tm == 0 and t % tm == 0 and n % tn == 0
    return pl.pallas_call(
        _out_proj_kernel,
        out_shape=jax.ShapeDtypeStruct((t, n), F32),
        grid=(t // tm, n // tn),
        in_specs=[
            pl.BlockSpec((tm, k), lambda i, j: (i, 0)),
            pl.BlockSpec((1, k, tn), lambda i, j: (layer, 0, j)),
            pl.BlockSpec((tm, tn), lambda i, j: (i, j)),
            pl.BlockSpec((1, 1, tn), lambda i, j: (i * tm // seq, 0, j)),
        ],
        out_specs=pl.BlockSpec((tm, tn), lambda i, j: (i, j)),
        compiler_params=_params(("arbitrary", "arbitrary")),
        name="out_proj",
    )(a, w, x, gate)


def _segment_mean_matrix():
    r = lax.broadcasted_iota(jnp.int32, (LANES, LANES), 0) // HEAD_DIM
    c = lax.broadcasted_iota(jnp.int32, (LANES, LANES), 1) // HEAD_DIM
    return jnp.where(r == c, 1.0 / HEAD_DIM, 0.0).astype(F32)


def _head_norm_rope(x, gain, cos, sin, seg_mean, low_half):
    ms = jnp.dot(x * x, seg_mean, preferred_element_type=F32)
    y = x * lax.rsqrt(ms + EPS) * gain
    partner = jnp.where(low_half, pltpu.roll(y, LANES - HEAD_DIM // 2, 1),
                        pltpu.roll(y, HEAD_DIM // 2, 1))
    return y * cos + partner * sin


def _gelu_tanh(x):
    return 0.5 * x * (1.0 + jnp.tanh(0.7978845608028654 * (x + 0.044715 * (x * x * x))))


def _mix_kernel(sinks_ref,
                q0_ref, q1_ref, kv_ref, u0_ref, u1_ref, v0_ref, v1_ref,
                cos_ref, sin_ref, qg_ref, kg_ref, vng_ref, ws_ref, bs_ref,
                o_ref, kd_scr, vd_scr):
    n = pl.program_id(1)
    w = WINDOW
    n_kv = kd_scr.shape[0]

    @pl.when(n == 0)
    def _():
        kd_scr[...] = jnp.zeros_like(kd_scr)
        vd_scr[...] = jnp.zeros_like(vd_scr)

    lane = lax.broadcasted_iota(jnp.int32, (w, LANES), 1)
    low_half = (lane % HEAD_DIM) < (HEAD_DIM // 2)
    first_head = lane < HEAD_DIM
    seg_mean = _segment_mean_matrix()
    cos = cos_ref[...]
    sin = sin_ref[...]

    kv = kv_ref[...].astype(F32)
    for c in range(n_kv // 2):
        kc = _head_norm_rope(kv[:, c * LANES:(c + 1) * LANES], kg_ref[...], cos, sin, seg_mean, low_half)
        vc = kv[:, (n_kv // 2 + c) * LANES:(n_kv // 2 + c + 1) * LANES]
        kc_sw = pltpu.roll(kc, HEAD_DIM, 1)
        vc_sw = pltpu.roll(vc, HEAD_DIM, 1)
        kd_scr[2 * c, w:, :] = jnp.where(first_head, kc, kc_sw).astype(BF16)
        kd_scr[2 * c + 1, w:, :] = jnp.where(first_head, kc_sw, kc).astype(BF16)
        vd_scr[2 * c, w:, :] = jnp.where(first_head, vc, vc_sw).astype(BF16)
        vd_scr[2 * c + 1, w:, :] = jnp.where(first_head, vc_sw, vc).astype(BF16)

    qi = lax.broadcasted_iota(jnp.int32, (w, 2 * w), 0)
    kj = lax.broadcasted_iota(jnp.int32, (w, 2 * w), 1)
    diff = qi + w - kj
    band = (diff >= 0) & (diff < w) & ((kj >= w) | (n > 0))
    band4 = jnp.concatenate([band] * KV_REP, axis=0)

    scale = HEAD_DIM ** -0.5
    ones_kv = jnp.ones((2 * w, LANES), BF16)
    q_refs = (q0_ref, q1_ref)
    chunks_per_ref = q0_ref.shape[1] // LANES
    for g in range(n_kv):
        rows = []
        for cc in range(2):
            c = 2 * g + cc
            qc = q_refs[c // chunks_per_ref][:, (c % chunks_per_ref) * LANES:(c % chunks_per_ref + 1) * LANES]
            qc = _head_norm_rope(qc.astype(F32), qg_ref[...], cos, sin, seg_mean, low_half)
            rows.append(jnp.where(first_head, qc, 0.0))
            rows.append(jnp.where(first_head, 0.0, qc))
        qm = jnp.concatenate(rows, axis=0).astype(BF16)
        s = lax.dot_general(qm, kd_scr[g], (((1,), (1,)), ((), ())),
                            preferred_element_type=F32) * scale
        s = jnp.where(band4, s, -jnp.inf)
        sink = jnp.concatenate(
            [jnp.full((w, LANES), sinks_ref[KV_REP * g + r], F32) for r in range(KV_REP)], axis=0)
        m = jnp.maximum(jnp.broadcast_to(jnp.max(s, axis=-1, keepdims=True), (KV_REP * w, LANES)), sink)
        p = jnp.exp(s - jnp.concatenate([m, m], axis=1)).astype(BF16)
        total = jnp.dot(p, ones_kv, preferred_element_type=F32) + jnp.exp(sink - m)
        o = jnp.dot(p, vd_scr[g], preferred_element_type=F32) / total
        for cc in range(2):
            oc = jnp.where(first_head, o[(2 * cc) * w:(2 * cc + 1) * w], o[(2 * cc + 1) * w:(2 * cc + 2) * w])
            c = 2 * g + cc
            o_ref[:, c * LANES:(c + 1) * LANES] = oc.astype(o_ref.dtype)

    kd_scr[:, :w, :] = kd_scr[:, w:, :]
    vd_scr[:, :w, :] = vd_scr[:, w:, :]

    a_q = 2 * q0_ref.shape[1]
    u = _gelu_tanh(jnp.concatenate([u0_ref[...], u1_ref[...]], axis=1).astype(F32))
    v = _gelu_tanh(jnp.concatenate([v0_ref[...], v1_ref[...]], axis=1).astype(F32))
    mu = jnp.mean(v, axis=-1, keepdims=True)
    vc = v - mu
    vn = (vc * lax.rsqrt(jnp.mean(vc * vc, axis=-1, keepdims=True) + EPS) * vng_ref[...]).astype(BF16)
    ti = lax.broadcasted_iota(jnp.int32, (w, w), 0)
    si = lax.broadcasted_iota(jnp.int32, (w, w), 1)
    causal = si <= ti
    bs = bs_ref[...]
    for g in range(ws_ref.shape[0]):
        wg = jnp.where(causal, ws_ref[g], 0.0).astype(BF16)
        sv = jnp.dot(wg, vn[:, g * LANES:(g + 1) * LANES], preferred_element_type=F32) + bs[:, g:g + 1]
        o_ref[:, a_q + g * LANES:a_q + (g + 1) * LANES] = (u[:, g * LANES:(g + 1) * LANES] * sv).astype(o_ref.dtype)


def _mix_even(proj, cos_t, sin_t, q_g, k_g, sinks, vnorm_g, ws, bs, bsz, seq):
    t = proj.shape[0]
    w = WINDOW
    nb = seq // w
    heads = sinks.shape[0]
    n_kv = heads // KV_REP
    a_q = heads * HEAD_DIM
    a_kv = n_kv * HEAD_DIM
    b_w = vnorm_g.shape[0]
    cw = 512
    assert a_q == 2 * cw and 2 * a_kv == cw and b_w == 2 * cw
    row = lambda b, n: b * nb + n
    gain2 = lambda g: jnp.tile(g, LANES // HEAD_DIM).reshape(1, LANES)
    col = lambda j: pl.BlockSpec((w, cw), lambda b, n, s: (row(b, n), j))
    grid_spec = pltpu.PrefetchScalarGridSpec(
        num_scalar_prefetch=1,
        grid=(bsz, nb),
        in_specs=[
            col(0), col(1), col(2), col(3), col(4), col(5), col(6),
            pl.BlockSpec((w, LANES), lambda b, n, s: (row(b, n), 0)),
            pl.BlockSpec((w, LANES), lambda b, n, s: (row(b, n), 0)),
            pl.BlockSpec((1, LANES), lambda b, n, s: (0, 0)),
            pl.BlockSpec((1, LANES), lambda b, n, s: (0, 0)),
            pl.BlockSpec((1, b_w), lambda b, n, s: (0, 0)),
            pl.BlockSpec(ws.shape, lambda b, n, s: (0, 0, 0)),
            pl.BlockSpec((w, ws.shape[0]), lambda b, n, s: (0, 0)),
        ],
        out_specs=pl.BlockSpec((w, a_q + b_w), lambda b, n, s: (row(b, n), 0)),
        scratch_shapes=[pltpu.VMEM((n_kv, 2 * w, LANES), BF16),
                        pltpu.VMEM((n_kv, 2 * w, LANES), BF16)],
    )
    return pl.pallas_call(
        _mix_kernel,
        out_shape=jax.ShapeDtypeStruct((t, a_q + b_w), BF16),
        grid_spec=grid_spec,
        compiler_params=_params(("arbitrary", "arbitrary")),
        name="mix_even",
    )(sinks, proj, proj, proj, proj, proj, proj, proj, cos_t, sin_t,
      gain2(q_g), gain2(k_g), vnorm_g.reshape(1, b_w), ws, bs.T)


def _hgrn_kernel(q_ref, f_ref, i_ref, gt_ref, lb_ref, og_ref, o_ref,
                 oi_scr, qd_scr, u_scr, d_scr):
    c = C_CHUNK
    nc = q_ref.shape[0] // c
    nv = c // SUBLANES
    lb = lb_ref[...]
    og = og_ref[...]
    t_row = lax.broadcasted_iota(jnp.int32, (c, LANES), 0)
    tt = lax.broadcasted_iota(jnp.int32, (c, c), 0)
    ss = lax.broadcasted_iota(jnp.int32, (c, c), 1)

    def inputs(ci):
        r0 = pl.multiple_of(ci * c, c)
        q = q_ref[pl.ds(r0, c), :].astype(F32)
        fl = f_ref[pl.ds(r0, c), :].astype(F32)
        v = i_ref[pl.ds(r0, c), :]
        f = lb + (1 - lb) * jax.nn.sigmoid(fl)
        k = 1 - f
        b = jnp.log2(f)
        for d in (1, 2, 4, 8, 16, 32):
            b = b + jnp.where(t_row >= d, pltpu.roll(b, d, 0), 0.0)
        return q, f, k, b, v

    def row(b, r, n):
        b3 = b.reshape(nv, SUBLANES, LANES)
        rep = jnp.broadcast_to(b3[r // SUBLANES:r // SUBLANES + 1, r % SUBLANES:r % SUBLANES + 1, :],
                               (n // SUBLANES, SUBLANES, LANES))
        return rep.reshape(n, LANES)

    def half_level(q, k, b, m):
        a_parts, b_parts = [], []
        for blk in range(c // (2 * m)):
            lo, mid, hi = blk * 2 * m, blk * 2 * m + m, (blk + 1) * 2 * m
            zero = jnp.zeros((m, LANES), BF16)
            ref = row(b, mid - 1, m)
            a_parts += [zero, (q[mid:hi] * jnp.exp2(b[mid:hi] - ref)).astype(BF16)]
            b_parts += [(k[lo:mid] * jnp.exp2(ref - b[lo:mid])).astype(BF16), zero]
        return _nt_dot(jnp.concatenate(a_parts, axis=0), jnp.concatenate(b_parts, axis=0))

    def finish(ci, q, k, b, v, scores):
        oi_scr[ci] = jnp.dot(scores.astype(BF16), v, preferred_element_type=F32)
        qd_scr[ci] = (q * jnp.exp2(b)).astype(BF16)
        b_last = b[c - 1:c, :]
        kd = (k * jnp.exp2(b_last - b)).astype(BF16)
        u_scr[ci] = lax.dot_general(v, kd, (((0,), (0,)), ((), ())), preferred_element_type=F32)
        d_scr[ci] = jnp.exp2(b_last)

    def intra_fast(ci, worst):
        q, f, k, b, v = inputs(ci)
        blk = HGRN_BLOCK
        starts = [jnp.zeros((blk, LANES), F32)] + [row(b, s - 1, blk) for s in range(blk, c, blk)]
        base = jnp.concatenate(starts, axis=0)
        s_d = _nt_dot((q * jnp.exp2(b - base)).astype(BF16), (k * jnp.exp2(base - b)).astype(BF16))
        s_16 = half_level(q, k, b, blk)
        s_32 = half_level(q, k, b, 2 * blk)
        scores = jnp.where(((tt // blk) == (ss // blk)) & (ss <= tt), s_d,
                           jnp.where((tt // (2 * blk)) == (ss // (2 * blk)), s_16, s_32))
        finish(ci, q, k, b, v, scores)
        for s in range(0, c, blk):
            worst = jnp.maximum(worst, base[s:s + 1, :] - b[s + blk - 1:s + blk, :])
        return worst

    def intra_safe(ci, carry):
        q, f, k, b, v = inputs(ci)
        scores = jnp.where(tt == ss, jnp.sum(q * k, axis=-1, keepdims=True), 0.0)
        odd = (t_row % 2) == 1
        s_1 = _nt_dot(jnp.where(odd, q * f, 0.0).astype(BF16), jnp.where(odd, 0.0, k).astype(BF16))
        scores = scores + jnp.where((tt // 2) == (ss // 2), s_1, 0.0)
        b3 = b.reshape(nv, SUBLANES, LANES)
        sub = lax.broadcasted_iota(jnp.int32, b3.shape, 1)
        small_refs = {4: jnp.broadcast_to(b3[:, 3:4, :], b3.shape).reshape(c, LANES),
                      2: jnp.where(sub < 4, jnp.broadcast_to(b3[:, 1:2, :], b3.shape),
                                   jnp.broadcast_to(b3[:, 5:6, :], b3.shape)).reshape(c, LANES)}
        for m in (32, 16, 8, 4, 2):
            if m >= SUBLANES:
                s_l = half_level(q, k, b, m)
            else:
                ref = small_refs[m]
                upper = (t_row % (2 * m)) >= m
                s_l = _nt_dot(jnp.where(upper, q * jnp.exp2(b - ref), 0.0).astype(BF16),
                              jnp.where(upper, 0.0, k * jnp.exp2(ref - b)).astype(BF16))
            scores = scores + (s_l if 2 * m == c else jnp.where((tt // (2 * m)) == (ss // (2 * m)), s_l, 0.0))
        finish(ci, q, k, b, v, scores)
        return carry

    worst = lax.fori_loop(0, nc, intra_fast, jnp.zeros((1, LANES), F32), unroll=8)

    @pl.when(jnp.max(worst) > HGRN_MAX_BLOCK_DECAY)
    def _():
        lax.fori_loop(0, nc, intra_safe, 0, unroll=2)

    def inter(ci, state_t):
        r0 = pl.multiple_of(ci * c, c)
        gate = gt_ref[pl.ds(r0, c), :].astype(F32)
        o = oi_scr[ci] + _nt_dot(qd_scr[ci], state_t.astype(BF16))
        y = o * lax.rsqrt(jnp.mean(o * o, axis=-1, keepdims=True) + EPS) * og
        o_ref[pl.ds(r0, c), :] = (y * (gate * jax.nn.sigmoid(gate))).astype(o_ref.dtype)
        return state_t * d_scr[ci] + u_scr[ci]

    lax.fori_loop(0, nc, inter, jnp.zeros((C_VDIM, C_KDIM), F32), unroll=8)


def _hgrn(proj, lb, og, bsz, seq):
    t, n4 = proj.shape
    heads = n4 // (4 * LANES)
    nc = seq // C_CHUNK
    return pl.pallas_call(
        _hgrn_kernel,
        out_shape=jax.ShapeDtypeStruct((t, heads * C_VDIM), BF16),
        grid=(bsz, heads),
        in_specs=[
            pl.BlockSpec((seq, LANES), lambda b, h: (b, h)),
            pl.BlockSpec((seq, LANES), lambda b, h: (b, heads + h)),
            pl.BlockSpec((seq, LANES), lambda b, h: (b, 2 * heads + h)),
            pl.BlockSpec((seq, LANES), lambda b, h: (b, 3 * heads + h)),
            pl.BlockSpec((1, LANES), lambda b, h: (0, h)),
            pl.BlockSpec((1, LANES), lambda b, h: (0, 0)),
        ],
        out_specs=pl.BlockSpec((seq, LANES), lambda b, h: (b, h)),
        scratch_shapes=[pltpu.VMEM((nc, C_CHUNK, C_VDIM), F32),
                        pltpu.VMEM((nc, C_CHUNK, C_KDIM), BF16),
                        pltpu.VMEM((nc, C_VDIM, C_KDIM), F32),
                        pltpu.VMEM((nc, 1, C_KDIM), F32)],
        compiler_params=_params(("arbitrary", "arbitrary")),
        name="hgrn2",
    )(proj, proj, proj, proj, lb.reshape(1, -1), og.reshape(1, LANES))


def _first_max(vals):
    best, idx = vals[0], jnp.zeros(vals[0].shape, jnp.int32)
    for i in range(1, len(vals)):
        better = vals[i] > best
        best = jnp.where(better, vals[i], best)
        idx = jnp.where(better, i, idx)
    return best, idx


def _pick(idx, vals):
    out = vals[-1]
    for i in range(len(vals) - 2, -1, -1):
        out = jnp.where(idx == i, vals[i], out)
    return out


def _router_kernel(x_ref, g_ref, sh_ref, sc_ref, rw_ref, rb_ref,
                   h_ref, eidx_ref, wsel_ref, rank_ref, cnt_ref, carry_scr):
    i = pl.program_id(0)
    tm = x_ref.shape[0]

    @pl.when(i == 0)
    def _():
        carry_scr[...] = jnp.zeros_like(carry_scr)

    h = _norm_mod(x_ref[...], g_ref[...], sh_ref[0], sc_ref[0])
    h_ref[...] = h
    rw = rw_ref[...]
    w_hi = rw.astype(BF16)
    w_lo = (rw - w_hi.astype(F32)).astype(BF16)
    h_hi = h.astype(BF16)
    h_lo = (h - h_hi.astype(F32)).astype(BF16)
    head = _nt_dot(jnp.concatenate([w_hi, w_lo], axis=0), h_hi)
    logits = head[:N_EXPERTS] + head[N_EXPERTS:] + _nt_dot(w_hi, h_lo)
    scores = jax.nn.sigmoid(logits)
    sel = scores + rb_ref[...]
    epg = EXPERTS_PER_GROUP
    sel_rows = [sel[e:e + 1, :] for e in range(N_EXPERTS)]
    sc_rows = [scores[e:e + 1, :] for e in range(N_EXPERTS)]
    gsum = []
    for g in range(N_GROUPS):
        r = sel_rows[g * epg:(g + 1) * epg]
        pair = None
        for a in range(epg):
            for b in range(a + 1, epg):
                pair = r[a] + r[b] if pair is None else jnp.maximum(pair, r[a] + r[b])
        gsum.append(pair)
    _, g_idx = _first_max(gsum)
    cand = [_pick(g_idx, [sel_rows[g * epg + j] for g in range(N_GROUPS)]) for j in range(epg)]
    cand_sc = [_pick(g_idx, [sc_rows[g * epg + j] for g in range(N_GROUPS)]) for j in range(epg)]
    _, i1 = _first_max(cand)
    _, i2 = _first_max([jnp.where(i1 == j, -jnp.inf, cand[j]) for j in range(epg)])
    w1 = _pick(i1, cand_sc)
    w2 = _pick(i2, cand_sc)
    tot = w1 + w2
    e1 = g_idx * epg + i1
    e2 = g_idx * epg + i2
    eidx_ref[0:1, :] = e1
    eidx_ref[1:2, :] = e2
    wsel_ref[0:1, :] = w1 / tot
    wsel_ref[1:2, :] = w2 / tot

    erow = lax.broadcasted_iota(jnp.int32, (N_EXPERTS, tm), 0)
    oh1 = erow == e1
    oh2 = erow == e2
    onehot = jnp.where(oh1 | oh2, 1.0, 0.0)
    si = lax.broadcasted_iota(jnp.int32, (tm, tm), 0)
    ti = lax.broadcasted_iota(jnp.int32, (tm, tm), 1)
    incl = jnp.where(si <= ti, 1.0, 0.0).astype(BF16)
    csum = jnp.dot(onehot.astype(BF16), incl, preferred_element_type=F32)
    carry = carry_scr[...]
    before = csum - onehot + carry[:, 0:1]
    r1 = jnp.sum(jnp.where(oh1, before, 0.0), axis=0, keepdims=True)
    r2 = jnp.sum(jnp.where(oh2, before, 0.0), axis=0, keepdims=True)
    rank_ref[0:1, :] = r1.astype(jnp.int32)
    rank_ref[1:2, :] = r2.astype(jnp.int32)
    new_carry = carry + csum[:, tm - 1:tm]
    carry_scr[...] = new_carry
    cnt_ref[...] = new_carry.astype(jnp.int32)


def _router(x, g, shift, scale, rw_t, rbias, seq):
    t, d = x.shape
    tm = ROW_TILE
    e = rw_t.shape[0]
    row = lambda i: (0, i)
    return pl.pallas_call(
        _router_kernel,
        out_shape=(jax.ShapeDtypeStruct((t, d), F32),
                   jax.ShapeDtypeStruct((TOP_K, t), jnp.int32),
                   jax.ShapeDtypeStruct((TOP_K, t), F32),
                   jax.ShapeDtypeStruct((TOP_K, t), jnp.int32),
                   jax.ShapeDtypeStruct((e, LANES), jnp.int32)),
        grid=(t // tm,),
        in_specs=[
            pl.BlockSpec((tm, d), lambda i: (i, 0)),
            pl.BlockSpec((1, d), lambda i: (0, 0)),
            pl.BlockSpec((1, 1, d), lambda i: (i * tm // seq, 0, 0)),
            pl.BlockSpec((1, 1, d), lambda i: (i * tm // seq, 0, 0)),
            pl.BlockSpec((e, d), lambda i: (0, 0)),
            pl.BlockSpec((e, 1), lambda i: (0, 0)),
        ],
        out_specs=(pl.BlockSpec((tm, d), lambda i: (i, 0)),
                   pl.BlockSpec((TOP_K, tm), row),
                   pl.BlockSpec((TOP_K, tm), row),
                   pl.BlockSpec((TOP_K, tm), row),
                   pl.BlockSpec((e, LANES), lambda i: (0, 0))),
        scratch_shapes=[pltpu.VMEM((e, LANES), F32)],
        compiler_params=_params(("arbitrary",)),
        name="router",
    )(x, g.reshape(1, d), shift, scale, rw_t, rbias.reshape(e, 1))


def _invert_kernel(pos_ref, fill_ref, src_ref, *, n_tok):
    def clear(p, carry):
        src_ref[p] = 0
        return carry

    for e in range(N_EXPERTS + 1):
        lax.fori_loop(fill_ref[2 * e], fill_ref[2 * e + 1], clear, 0)

    for slot in range(TOP_K):
        def place(tok, carry, slot=slot):
            src_ref[pos_ref[slot * n_tok + tok]] = tok
            return carry

        lax.fori_loop(0, n_tok, place, 0, unroll=8)


def _invert_positions(pos_flat, fill, n_rows, n_tok):
    assert pos_flat.shape[0] == TOP_K * n_tok
    return pl.pallas_call(
        functools.partial(_invert_kernel, n_tok=n_tok),
        out_shape=jax.ShapeDtypeStruct((n_rows,), jnp.int32),
        in_specs=[pl.BlockSpec(memory_space=pltpu.SMEM), pl.BlockSpec(memory_space=pltpu.SMEM)],
        out_specs=pl.BlockSpec(memory_space=pltpu.SMEM),
        name="invert_positions",
    )(pos_flat, fill)


def _expert_kernel(te_ref, tv_ref, tf_ref, tn_ref, src_hbm, h_hbm, wg_hbm, wu_hbm, wd_hbm, y_ref,
                   xbuf0, xbuf1, idx0, idx1, sg, su, sd, bg, bu, bd, gsem, isem, wsem, *, layer):
    i = pl.program_id(0)
    n = pl.num_programs(0)
    xbufs = (xbuf0, xbuf1)
    idxs = (idx0, idx1)
    tm = xbuf0.shape[0]

    def index_copy(tile, buf):
        return pltpu.make_async_copy(src_hbm.at[pl.ds(tile, 1)], idxs[buf], isem.at[buf])

    def gather(buf):
        for r in range(tm):
            tok = idxs[buf][0, r]
            pltpu.make_async_copy(h_hbm.at[pl.ds(tok, 1)], xbufs[buf].at[pl.ds(r, 1)], gsem.at[buf]).start()

    def fetch(e):
        return (pltpu.make_async_copy(wg_hbm.at[layer, e], sg, wsem.at[0]),
                pltpu.make_async_copy(wu_hbm.at[layer, e], su, wsem.at[1]),
                pltpu.make_async_copy(wd_hbm.at[layer, e], sd, wsem.at[2]))

    @pl.when(i == 0)
    def _():
        for c in fetch(te_ref[0]):
            c.start(priority=WEIGHT_DMA_PRIORITY)
        index_copy(0, 0).start()
        index_copy(1, 1).start()
        index_copy(0, 0).wait()
        gather(0)

    pending = (i == 0) | (tv_ref[jnp.maximum(i - 1, 0)] > 0)
    for par in range(2):
        @pl.when((i % 2 == par) & (i + 1 < n))
        def _(par=par):
            index_copy(i + 1, 1 - par).wait()

        @pl.when(pending & (i % 2 == par))
        def _(par=par):
            pltpu.make_async_copy(h_hbm.at[pl.ds(0, tm)], xbufs[par], gsem.at[par]).wait()

    @pl.when(tf_ref[i] > 0)
    def _():
        for c in fetch(te_ref[i]):
            c.wait()
        bg[...] = sg[...].astype(BF16)
        bu[...] = su[...].astype(BF16)
        bd[...] = sd[...].astype(BF16)

        @pl.when(tn_ref[i] >= 0)
        def _():
            for c in fetch(tn_ref[i]):
                c.start(priority=WEIGHT_DMA_PRIORITY)

    for par in range(2):
        @pl.when((tv_ref[i] > 0) & (i % 2 == par))
        def _(par=par):
            gather(1 - par)
            x = xbufs[par][...].astype(BF16)
            a = jnp.dot(x, bg[...], preferred_element_type=F32)
            u = jnp.dot(x, bu[...], preferred_element_type=F32)
            hid = (a * jax.nn.sigmoid(a) * u).astype(BF16)
            y_ref[...] = jnp.dot(hid, bd[...], preferred_element_type=F32)

    @pl.when(tv_ref[i] == 0)
    def _():
        y_ref[...] = jnp.zeros_like(y_ref)

    for par in range(2):
        @pl.when((i % 2 == par) & (i + 2 < n))
        def _(par=par):
            index_copy(i + 2, par).start()


def _experts(tile_expert, tile_valid, tile_first, tile_next, src, h, w_gate, w_up, w_down, layer, n_rows):
    d = h.shape[1]
    f = w_gate.shape[3]
    tm = EXPERT_TILE
    grid_spec = pltpu.PrefetchScalarGridSpec(
        num_scalar_prefetch=4,
        grid=(n_rows // tm,),
        in_specs=[pl.BlockSpec(memory_space=pl.ANY)] * 5,
        out_specs=pl.BlockSpec((tm, d), lambda i, *_: (i, 0)),
        scratch_shapes=[pltpu.VMEM((tm, d), F32), pltpu.VMEM((tm, d), F32),
                        pltpu.SMEM((1, tm), jnp.int32), pltpu.SMEM((1, tm), jnp.int32),
                        pltpu.VMEM((d, f), F32), pltpu.VMEM((d, f), F32), pltpu.VMEM((f, d), F32),
                        pltpu.VMEM((d, f), BF16), pltpu.VMEM((d, f), BF16), pltpu.VMEM((f, d), BF16),
                        pltpu.SemaphoreType.DMA((2,)), pltpu.SemaphoreType.DMA((2,)),
                        pltpu.SemaphoreType.DMA((3,))],
    )
    return pl.pallas_call(
        functools.partial(_expert_kernel, layer=layer),
        out_shape=jax.ShapeDtypeStruct((n_rows, d), F32),
        grid_spec=grid_spec,
        compiler_params=_params(("arbitrary",)),
        name="experts",
    )(tile_expert, tile_valid, tile_first, tile_next, src.reshape(n_rows // tm, tm), h, w_gate, w_up, w_down)


def _combine_kernel(x_ref, w_ref, g_ref, pos_hbm, y_ref, o_ref, buf0, buf1, idx0, idx1, sem, isem):
    i = pl.program_id(0)
    n = pl.num_programs(0)
    tm = x_ref.shape[0]
    bufs = (buf0, buf1)
    idxs = (idx0, idx1)

    def index_copy(tile, par):
        start = pl.multiple_of(tile * tm, tm)
        return pltpu.make_async_copy(pos_hbm.at[:, pl.ds(start, tm)], idxs[par], isem.at[par])

    def gather(par):
        for r in range(tm):
            for slot in range(TOP_K):
                p = idxs[par][slot, r]
                pltpu.make_async_copy(y_ref.at[pl.ds(p, 1)], bufs[par].at[slot, pl.ds(r, 1)],
                                      sem.at[par]).start(priority=slot)

    def combine(par):
        w = w_ref[...]
        y = w[:, 0:1] * bufs[par][0] + w[:, 1:2] * bufs[par][1]
        o_ref[...] = x_ref[...] + g_ref[0] * y

    @pl.when(i == 0)
    def _():
        index_copy(0, 0).start()
        index_copy(1, 1).start()
        index_copy(0, 0).wait()
        gather(0)

    for par in range(2):
        @pl.when(i % 2 == par)
        def _(par=par):
            for slot in range(TOP_K):
                pltpu.make_async_copy(y_ref.at[pl.ds(0, tm)], bufs[par].at[slot], sem.at[par]).wait()

        @pl.when((i % 2 == par) & (i + 1 < n))
        def _(par=par):
            index_copy(i + 1, 1 - par).wait()
            gather(1 - par)
            combine(par)

        @pl.when((i % 2 == par) & (i + 1 == n))
        def _(par=par):
            combine(par)

        @pl.when((i % 2 == par) & (i + 2 < n))
        def _(par=par):
            index_copy(i + 2, par).start()


def _combine(pos, x, wsel_t, gate, y, seq):
    t, d = x.shape
    tm = ROW_TILE
    assert seq % tm == 0 and t % tm == 0
    assert t // tm >= 2
    return pl.pallas_call(
        _combine_kernel,
        out_shape=jax.ShapeDtypeStruct((t, d), F32),
        grid=(t // tm,),
        in_specs=[
            pl.BlockSpec((tm, d), lambda i: (i, 0)),
            pl.BlockSpec((tm, TOP_K), lambda i: (i, 0)),
            pl.BlockSpec((1, 1, d), lambda i: (i * tm // seq, 0, 0)),
            pl.BlockSpec(memory_space=pl.ANY),
            pl.BlockSpec(memory_space=pl.ANY),
        ],
        out_specs=pl.BlockSpec((tm, d), lambda i: (i, 0)),
        scratch_shapes=[pltpu.VMEM((TOP_K, tm, d), F32), pltpu.VMEM((TOP_K, tm, d), F32),
                        pltpu.SMEM((TOP_K, tm), jnp.int32), pltpu.SMEM((TOP_K, tm), jnp.int32),
                        pltpu.SemaphoreType.DMA((2,)), pltpu.SemaphoreType.DMA((2,))],
        compiler_params=_params(("arbitrary",)),
        name="combine",
    )(x, wsel_t, gate, pos, y)


def _moe(x, g, shift, scale, gate, rw_t, rbias, w_gate, w_up, w_down, layer, seq):
    t, d = x.shape
    te = EXPERT_TILE
    n_rows = TOP_K * t + N_EXPERTS * te
    n_tiles = n_rows // te
    h, eidx, wsel, rank, cnt = _router(x, g, shift, scale, rw_t, rbias, seq)
    counts = cnt[:, 0]
    padded = ((counts + te - 1) // te) * te
    ends = jnp.cumsum(padded)
    starts = ends - padded
    pos = rank
    for e in range(N_EXPERTS):
        pos = pos + jnp.where(eidx == e, starts[e], 0)
    pos_flat = pos.reshape(-1)
    tile_start = jnp.arange(n_tiles, dtype=jnp.int32) * te
    tile_expert = jnp.minimum(jnp.sum(tile_start[:, None] >= ends[None, :], axis=1), N_EXPERTS - 1).astype(jnp.int32)
    tile_valid = (tile_start < ends[-1]).astype(jnp.int32)
    tile_expert = jnp.where(tile_valid > 0, tile_expert, tile_expert[jnp.maximum(ends[-1] // te - 1, 0)])
    prev_expert = jnp.concatenate([jnp.full((1,), -1, jnp.int32), tile_expert[:-1]])
    tile_first = ((tile_valid > 0) & (tile_expert != prev_expert)).astype(jnp.int32)
    later = (tile_expert[None, :] > tile_expert[:, None]) & (tile_valid[None, :] > 0)
    tile_next = jnp.min(jnp.where(later, tile_expert[None, :], N_EXPERTS), axis=1)
    tile_next = jnp.where(tile_next < N_EXPERTS, tile_next, -1).astype(jnp.int32)
    fill = jnp.stack([jnp.concatenate([starts + counts, ends[-1:]]),
                      jnp.concatenate([ends, jnp.full((1,), n_rows, ends.dtype)])], axis=1).reshape(-1).astype(jnp.int32)
    src = _invert_positions(pos_flat, fill, n_rows, t)
    y = _experts(tile_expert, tile_valid, tile_first, tile_next, src, h, w_gate, w_up, w_down, layer, n_rows)
    return _combine(pos, x, wsel.T, gate, y, seq)


def kernel(x, c, positions, ada_w, ada_b, norm1_g, norm2_g, ab_w_in, q_norm_g, k_norm_g, sinks, gm_vnorm_g, gm_ws, gm_b, ab_w_out, c_w_in, c_lower_bounds, c_onorm_g, c_w_out, router_w, router_bias, moe_w_gate, moe_w_up, moe_w_down):
    bsz, seq, d = x.shape
    depth = ada_w.shape[0]
    t = bsz * seq
    xt = x.reshape(t, d)

    c_pad = jnp.concatenate([c, jnp.zeros((8 - bsz % 8, d), F32)], axis=0) if bsz % 8 else c
    mod = _ada_mod(c_pad, ada_w, ada_b)[:, :bsz]
    mod = mod.reshape(depth, bsz, 6, 1, d)
    cos_t, sin_t = _rope_tables(positions)

    sm = jax.nn.softmax(c_lower_bounds.astype(F32), axis=0)
    lower_bounds = jnp.cumsum(sm, axis=0) - sm[0:1]
    rw_t = router_w.T
    ab_in, ab_out, c_in, c_out = (w.astype(BF16) for w in (ab_w_in, ab_w_out, c_w_in, c_w_out))

    for l in range(depth):
        sh1, sc1, g1, sh2, sc2, g2 = (mod[l, :, k] for k in range(6))
        if l % 2 == 0:
            i = l // 2
            proj = _in_proj(xt, norm1_g[l], sh1, sc1, ab_in, i, seq)
            mix = _mix_even(proj, cos_t, sin_t, q_norm_g[i], k_norm_g[i], sinks[i],
                            gm_vnorm_g[i], gm_ws[i], gm_b[i], bsz, seq)
            xt = _out_proj(mix, ab_out, i, xt, g1, seq)
        else:
            j = l // 2
            proj = _in_proj(xt, norm1_g[l], sh1, sc1, c_in, j, seq)
            o = _hgrn(proj, lower_bounds[l], c_onorm_g[j], bsz, seq)
            xt = _out_proj(o, c_out, j, xt, g1, seq)
        xt = _moe(xt, norm2_g[l], sh2, sc2, g2, rw_t, router_bias,
                  moe_w_gate, moe_w_up, moe_w_down, l, seq)
    return xt.reshape(bsz, seq, d)
```

```python
import functools

import jax
import jax.numpy as jnp
from jax import lax
from jax.experimental import pallas as pl
from jax.experimental.pallas import tpu as pltpu

F32 = jnp.float32
BF16 = jnp.bfloat16

EPS = 1e-6
HEAD_DIM = 64
KV_REP = 4
WINDOW = 128
ROPE_THETA = 10000.0
B_GROUP_DIM = 128
B_CHUNK = 128
C_KDIM = 128
C_VDIM = 128
C_CHUNK = 64
N_GROUPS = 4
EXPERTS_PER_GROUP = 4
N_EXPERTS = N_GROUPS * EXPERTS_PER_GROUP
TOP_K = 2

LANES = 128
SUBLANES = 8
VMEM_LIMIT = 56 * 1024 * 1024

ROW_TILE = 512
PROJ_ROW_TILE = 1024
PROJ_COL_TILE = 2048
EXPERT_TILE = 256
WEIGHT_DMA_PRIORITY = 1
HGRN_BLOCK = 16
HGRN_MAX_BLOCK_DECAY = 100.0


def _nt_dot(a, b):
    return lax.dot_general(a, b, (((1,), (1,)), ((), ())), preferred_element_type=F32)


def _params(sem):
    return pltpu.CompilerParams(dimension_semantics=sem, vmem_limit_bytes=VMEM_LIMIT)


def _ada_kernel(c_ref, w_ref, b_ref, o_ref):
    c = c_ref[...]
    cond = c * jax.nn.sigmoid(c)
    o_ref[0] = jnp.dot(cond, w_ref[0], preferred_element_type=F32) + b_ref[0]


def _ada_mod(c_pad, ada_w, ada_b):
    depth, d, n = ada_w.shape
    tn = 1536
    rows = c_pad.shape[0]
    return pl.pallas_call(
        _ada_kernel,
        out_shape=jax.ShapeDtypeStruct((depth, rows, n), F32),
        grid=(depth, n // tn),
        in_specs=[
            pl.BlockSpec((rows, d), lambda l, j: (0, 0)),
            pl.BlockSpec((1, d, tn), lambda l, j: (l, 0, j)),
            pl.BlockSpec((1, 1, tn), lambda l, j: (l, 0, j)),
        ],
        out_specs=pl.BlockSpec((1, rows, tn), lambda l, j: (l, 0, j)),
        compiler_params=_params(("arbitrary", "arbitrary")),
        name="ada_mod",
    )(c_pad, ada_w, ada_b.reshape(depth, 1, n))


def _rope_table_kernel(pos_ref, freq_ref, sign_ref, cos_ref, sin_ref):
    ang = pos_ref[...].astype(F32) * freq_ref[...]
    cos_ref[...] = jnp.cos(ang)
    sin_ref[...] = jnp.sin(ang) * sign_ref[...]


def _rope_tables(positions):
    t = positions.size
    half = HEAD_DIM // 2
    inv_freq = ROPE_THETA ** (-jnp.arange(half, dtype=F32) / half)
    freq = jnp.tile(inv_freq, LANES // half).reshape(1, LANES)
    sign = jnp.tile(jnp.concatenate([-jnp.ones((half,), F32), jnp.ones((half,), F32)]),
                    LANES // HEAD_DIM).reshape(1, LANES)
    tm = min(1024, t)
    return pl.pallas_call(
        _rope_table_kernel,
        out_shape=(jax.ShapeDtypeStruct((t, LANES), F32), jax.ShapeDtypeStruct((t, LANES), F32)),
        grid=(t // tm,),
        in_specs=[
            pl.BlockSpec((tm, 1), lambda i: (i, 0)),
            pl.BlockSpec((1, LANES), lambda i: (0, 0)),
            pl.BlockSpec((1, LANES), lambda i: (0, 0)),
        ],
        out_specs=(pl.BlockSpec((tm, LANES), lambda i: (i, 0)),
                   pl.BlockSpec((tm, LANES), lambda i: (i, 0))),
        compiler_params=_params(("arbitrary",)),
        name="rope_tables",
    )(positions.reshape(t, 1), freq, sign)


def _norm_mod(x, g, shift, scale):
    y = x * lax.rsqrt(jnp.mean(x * x, axis=-1, keepdims=True) + EPS)
    return (y * g) * (1 + scale) + shift


def _in_proj_kernel(x_ref, g_ref, sh_ref, sc_ref, w_ref, o_ref, h_scr):
    @pl.when(pl.program_id(1) == 0)
    def _():
        h_scr[...] = _norm_mod(x_ref[...], g_ref[...], sh_ref[0], sc_ref[0]).astype(BF16)

    o_ref[...] = jnp.dot(h_scr[...], w_ref[0], preferred_element_type=F32).astype(o_ref.dtype)


def _in_proj(x, g, shift, scale, w, layer, seq):
    t, d = x.shape
    n = w.shape[2]
    tm = PROJ_ROW_TILE
    tn = max(c for c in range(LANES, PROJ_COL_TILE + 1, LANES) if n % c == 0)
    assert seq % tm == 0 and t % tm == 0
    return pl.pallas_call(
        _in_proj_kernel,
        out_shape=jax.ShapeDtypeStruct((t, n), BF16),
        grid=(t // tm, n // tn),
        in_specs=[
            pl.BlockSpec((tm, d), lambda i, j: (i, 0)),
            pl.BlockSpec((1, d), lambda i, j: (0, 0)),
            pl.BlockSpec((1, 1, d), lambda i, j: (i * tm // seq, 0, 0)),
            pl.BlockSpec((1, 1, d), lambda i, j: (i * tm // seq, 0, 0)),
            pl.BlockSpec((1, d, tn), lambda i, j: (layer, 0, j)),
        ],
        out_specs=pl.BlockSpec((tm, tn), lambda i, j: (i, j)),
        scratch_shapes=[pltpu.VMEM((tm, d), BF16)],
        compiler_params=_params(("arbitrary", "arbitrary")),
        name="in_proj",
    )(x, g.reshape(1, d), shift, scale, w)


def _out_proj_kernel(a_ref, w_ref, x_ref, g_ref, o_ref):
    y = jnp.dot(a_ref[...], w_ref[0], preferred_element_type=F32)
    o_ref[...] = x_ref[...] + g_ref[0] * y


def _out_proj(a, w, layer, x, gate, seq):
    t, k = a.shape
    n = w.shape[2]
    tm, tn = PROJ_ROW_TILE, 1024
    assert seq % tm == 0 and t % tm == 0 and n % tn == 0
    return pl.pallas_call(
        _out_proj_kernel,
        out_shape=jax.ShapeDtypeStruct((t, n), F32),
        grid=(t // tm, n // tn),
        in_specs=[
            pl.BlockSpec((tm, k), lambda i, j: (i, 0)),
            pl.BlockSpec((1, k, tn), lambda i, j: (layer, 0, j)),
            pl.BlockSpec((tm, tn), lambda i, j: (i, j)),
            pl.BlockSpec((1, 1, tn), lambda i, j: (i * tm // seq, 0, j)),
        ],
        out_specs=pl.BlockSpec((tm, tn), lambda i, j: (i, j)),
        compiler_params=_params(("arbitrary", "arbitrary")),
        name="out_proj",
    )(a, w, x, gate)


def _segment_mean_matrix():
    r = lax.broadcasted_iota(jnp.int32, (LANES, LANES), 0) // HEAD_DIM
    c = lax.broadcasted_iota(jnp.int32, (LANES, LANES), 1) // HEAD_DIM
    return jnp.where(r == c, 1.0 / HEAD_DIM, 0.0).astype(F32)


def _head_norm_rope(x, gain, cos, sin, seg_mean, low_half):
    ms = jnp.dot(x * x, seg_mean, preferred_element_type=F32)
    y = x * lax.rsqrt(ms + EPS) * gain
    partner = jnp.where(low_half, pltpu.roll(y, LANES - HEAD_DIM // 2, 1),
                        pltpu.roll(y, HEAD_DIM // 2, 1))
    return y * cos + partner * sin


def _gelu_tanh(x):
    return 0.5 * x * (1.0 + jnp.tanh(0.7978845608028654 * (x + 0.044715 * (x * x * x))))


def _mix_kernel(sinks_ref,
                q0_ref, q1_ref, kv_ref, u0_ref, u1_ref, v0_ref, v1_ref,
                cos_ref, sin_ref, qg_ref, kg_ref, vng_ref, ws_ref, bs_ref,
                o_ref, kd_scr, vd_scr):
    n = pl.program_id(1)
    w = WINDOW
    n_kv = kd_scr.shape[0]

    @pl.when(n == 0)
    def _():
        kd_scr[...] = jnp.zeros_like(kd_scr)
        vd_scr[...] = jnp.zeros_like(vd_scr)

    lane = lax.broadcasted_iota(jnp.int32, (w, LANES), 1)
    low_half = (lane % HEAD_DIM) < (HEAD_DIM // 2)
    first_head = lane < HEAD_DIM
    seg_mean = _segment_mean_matrix()
    cos = cos_ref[...]
    sin = sin_ref[...]

    kv = kv_ref[...].astype(F32)
    for c in range(n_kv // 2):
        kc = _head_norm_rope(kv[:, c * LANES:(c + 1) * LANES], kg_ref[...], cos, sin, seg_mean, low_half)
        vc = kv[:, (n_kv // 2 + c) * LANES:(n_kv // 2 + c + 1) * LANES]
        kc_sw = pltpu.roll(kc, HEAD_DIM, 1)
        vc_sw = pltpu.roll(vc, HEAD_DIM, 1)
        kd_scr[2 * c, w:, :] = jnp.where(first_head, kc, kc_sw).astype(BF16)
        kd_scr[2 * c + 1, w:, :] = jnp.where(first_head, kc_sw, kc).astype(BF16)
        vd_scr[2 * c, w:, :] = jnp.where(first_head, vc, vc_sw).astype(BF16)
        vd_scr[2 * c + 1, w:, :] = jnp.where(first_head, vc_sw, vc).astype(BF16)

    qi = lax.broadcasted_iota(jnp.int32, (w, 2 * w), 0)
    kj = lax.broadcasted_iota(jnp.int32, (w, 2 * w), 1)
    diff = qi + w - kj
    band = (diff >= 0) & (diff < w) & ((kj >= w) | (n > 0))
    band4 = jnp.concatenate([band] * KV_REP, axis=0)

    scale = HEAD_DIM ** -0.5
    ones_kv = jnp.ones((2 * w, LANES), BF16)
    q_refs = (q0_ref, q1_ref)
    chunks_per_ref = q0_ref.shape[1] // LANES
    for g in range(n_kv):
        rows = []
        for cc in range(2):
            c = 2 * g + cc
            qc = q_refs[c // chunks_per_ref][:, (c % chunks_per_ref) * LANES:(c % chunks_per_ref + 1) * LANES]
            qc = _head_norm_rope(qc.astype(F32), qg_ref[...], cos, sin, seg_mean, low_half)
            rows.append(jnp.where(first_head, qc, 0.0))
            rows.append(jnp.where(first_head, 0.0, qc))
        qm = jnp.concatenate(rows, axis=0).astype(BF16)
        s = lax.dot_general(qm, kd_scr[g], (((1,), (1,)), ((), ())),
                            preferred_element_type=F32) * scale
        s = jnp.where(band4, s, -jnp.inf)
        sink = jnp.concatenate(
            [jnp.full((w, LANES), sinks_ref[KV_REP * g + r], F32) for r in range(KV_REP)], axis=0)
        m = jnp.maximum(jnp.broadcast_to(jnp.max(s, axis=-1, keepdims=True), (KV_REP * w, LANES)), sink)
        p = jnp.exp(s - jnp.concatenate([m, m], axis=1)).astype(BF16)
        total = jnp.dot(p, ones_kv, preferred_element_type=F32) + jnp.exp(sink - m)
        o = jnp.dot(p, vd_scr[g], preferred_element_type=F32) / total
        for cc in range(2):
            oc = jnp.where(first_head, o[(2 * cc) * w:(2 * cc + 1) * w], o[(2 * cc + 1) * w:(2 * cc + 2) * w])
            c = 2 * g + cc
            o_ref[:, c * LANES:(c + 1) * LANES] = oc.astype(o_ref.dtype)

    kd_scr[:, :w, :] = kd_scr[:, w:, :]
    vd_scr[:, :w, :] = vd_scr[:, w:, :]

    a_q = 2 * q0_ref.shape[1]
    u = _gelu_tanh(jnp.concatenate([u0_ref[...], u1_ref[...]], axis=1).astype(F32))
    v = _gelu_tanh(jnp.concatenate([v0_ref[...], v1_ref[...]], axis=1).astype(F32))
    mu = jnp.mean(v, axis=-1, keepdims=True)
    vc = v - mu
    vn = (vc * lax.rsqrt(jnp.mean(vc * vc, axis=-1, keepdims=True) + EPS) * vng_ref[...]).astype(BF16)
    ti = lax.broadcasted_iota(jnp.int32, (w, w), 0)
    si = lax.broadcasted_iota(jnp.int32, (w, w), 1)
    causal = si <= ti
    bs = bs_ref[...]
    for g in range(ws_ref.shape[0]):
        wg = jnp.where(causal, ws_ref[g], 0.0).astype(BF16)
        sv = jnp.dot(wg, vn[:, g * LANES:(g + 1) * LANES], preferred_element_type=F32) + bs[:, g:g + 1]
        o_ref[:, a_q + g * LANES:a_q + (g + 1) * LANES] = (u[:, g * LANES:(g + 1) * LANES] * sv).astype(o_ref.dtype)


def _mix_even(proj, cos_t, sin_t, q_g, k_g, sinks, vnorm_g, ws, bs, bsz, seq):
    t = proj.shape[0]
    w = WINDOW
    nb = seq // w
    heads = sinks.shape[0]
    n_kv = heads // KV_REP
    a_q = heads * HEAD_DIM
    a_kv = n_kv * HEAD_DIM
    b_w = vnorm_g.shape[0]
    cw = 512
    assert a_q == 2 * cw and 2 * a_kv == cw and b_w == 2 * cw
    row = lambda b, n: b * nb + n
    gain2 = lambda g: jnp.tile(g, LANES // HEAD_DIM).reshape(1, LANES)
    col = lambda j: pl.BlockSpec((w, cw), lambda b, n, s: (row(b, n), j))
    grid_spec = pltpu.PrefetchScalarGridSpec(
        num_scalar_prefetch=1,
        grid=(bsz, nb),
        in_specs=[
            col(0), col(1), col(2), col(3), col(4), col(5), col(6),
            pl.BlockSpec((w, LANES), lambda b, n, s: (row(b, n), 0)),
            pl.BlockSpec((w, LANES), lambda b, n, s: (row(b, n), 0)),
            pl.BlockSpec((1, LANES), lambda b, n, s: (0, 0)),
            pl.BlockSpec((1, LANES), lambda b, n, s: (0, 0)),
            pl.BlockSpec((1, b_w), lambda b, n, s: (0, 0)),
            pl.BlockSpec(ws.shape, lambda b, n, s: (0, 0, 0)),
            pl.BlockSpec((w, ws.shape[0]), lambda b, n, s: (0, 0)),
        ],
        out_specs=pl.BlockSpec((w, a_q + b_w), lambda b, n, s: (row(b, n), 0)),
        scratch_shapes=[pltpu.VMEM((n_kv, 2 * w, LANES), BF16),
                        pltpu.VMEM((n_kv, 2 * w, LANES), BF16)],
    )
    return pl.pallas_call(
        _mix_kernel,
        out_shape=jax.ShapeDtypeStruct((t, a_q + b_w), BF16),
        grid_spec=grid_spec,
        compiler_params=_params(("arbitrary", "arbitrary")),
        name="mix_even",
    )(sinks, proj, proj, proj, proj, proj, proj, proj, cos_t, sin_t,
      gain2(q_g), gain2(k_g), vnorm_g.reshape(1, b_w), ws, bs.T)


def _hgrn_kernel(q_ref, f_ref, i_ref, gt_ref, lb_ref, og_ref, o_ref,
                 oi_scr, qd_scr, u_scr, d_scr):
    c = C_CHUNK
    nc = q_ref.shape[0] // c
    nv = c // SUBLANES
    lb = lb_ref[...]
    og = og_ref[...]
    t_row = lax.broadcasted_iota(jnp.int32, (c, LANES), 0)
    tt = lax.broadcasted_iota(jnp.int32, (c, c), 0)
    ss = lax.broadcasted_iota(jnp.int32, (c, c), 1)

    def inputs(ci):
        r0 = pl.multiple_of(ci * c, c)
        q = q_ref[pl.ds(r0, c), :].astype(F32)
        fl = f_ref[pl.ds(r0, c), :].astype(F32)
        v = i_ref[pl.ds(r0, c), :]
        f = lb + (1 - lb) * jax.nn.sigmoid(fl)
        k = 1 - f
        b = jnp.log2(f)
        for d in (1, 2, 4, 8, 16, 32):
            b = b + jnp.where(t_row >= d, pltpu.roll(b, d, 0), 0.0)
        return q, f, k, b, v

    def row(b, r, n):
        b3 = b.reshape(nv, SUBLANES, LANES)
        rep = jnp.broadcast_to(b3[r // SUBLANES:r // SUBLANES + 1, r % SUBLANES:r % SUBLANES + 1, :],
                               (n // SUBLANES, SUBLANES, LANES))
        return rep.reshape(n, LANES)

    def half_level(q, k, b, m):
        a_parts, b_parts = [], []
        for blk in range(c // (2 * m)):
            lo, mid, hi = blk * 2 * m, blk * 2 * m + m, (blk + 1) * 2 * m
            zero = jnp.zeros((m, LANES), BF16)
            ref = row(b, mid - 1, m)
            a_parts += [zero, (q[mid:hi] * jnp.exp2(b[mid:hi] - ref)).astype(BF16)]
            b_parts += [(k[lo:mid] * jnp.exp2(ref - b[lo:mid])).astype(BF16), zero]
        return _nt_dot(jnp.concatenate(a_parts, axis=0), jnp.concatenate(b_parts, axis=0))

    def finish(ci, q, k, b, v, scores):
        oi_scr[ci] = jnp.dot(scores.astype(BF16), v, preferred_element_type=F32)
        qd_scr[ci] = (q * jnp.exp2(b)).astype(BF16)
        b_last = b[c - 1:c, :]
        kd = (k * jnp.exp2(b_last - b)).astype(BF16)
        u_scr[ci] = lax.dot_general(v, kd, (((0,), (0,)), ((), ())), preferred_element_type=F32)
        d_scr[ci] = jnp.exp2(b_last)

    def intra_fast(ci, worst):
        q, f, k, b, v = inputs(ci)
        blk = HGRN_BLOCK
        starts = [jnp.zeros((blk, LANES), F32)] + [row(b, s - 1, blk) for s in range(blk, c, blk)]
        base = jnp.concatenate(starts, axis=0)
        s_d = _nt_dot((q * jnp.exp2(b - base)).astype(BF16), (k * jnp.exp2(base - b)).astype(BF16))
        s_16 = half_level(q, k, b, blk)
        s_32 = half_level(q, k, b, 2 * blk)
        scores = jnp.where(((tt // blk) == (ss // blk)) & (ss <= tt), s_d,
                           jnp.where((tt // (2 * blk)) == (ss // (2 * blk)), s_16, s_32))
        finish(ci, q, k, b, v, scores)
        for s in range(0, c, blk):
            worst = jnp.maximum(worst, base[s:s + 1, :] - b[s + blk - 1:s + blk, :])
        return worst

    def intra_safe(ci, carry):
        q, f, k, b, v = inputs(ci)
        scores = jnp.where(tt == ss, jnp.sum(q * k, axis=-1, keepdims=True), 0.0)
        odd = (t_row % 2) == 1
        s_1 = _nt_dot(jnp.where(odd, q * f, 0.0).astype(BF16), jnp.where(odd, 0.0, k).astype(BF16))
        scores = scores + jnp.where((tt // 2) == (ss // 2), s_1, 0.0)
        b3 = b.reshape(nv, SUBLANES, LANES)
        sub = lax.broadcasted_iota(jnp.int32, b3.shape, 1)
        small_refs = {4: jnp.broadcast_to(b3[:, 3:4, :], b3.shape).reshape(c, LANES),
                      2: jnp.where(sub < 4, jnp.broadcast_to(b3[:, 1:2, :], b3.shape),
                                   jnp.broadcast_to(b3[:, 5:6, :], b3.shape)).reshape(c, LANES)}
        for m in (32, 16, 8, 4, 2):
            if m >= SUBLANES:
                s_l = half_level(q, k, b, m)
            else:
                ref = small_refs[m]
                upper = (t_row % (2 * m)) >= m
                s_l = _nt_dot(jnp.where(upper, q * jnp.exp2(b - ref), 0.0).astype(BF16),
                              jnp.where(upper, 0.0, k * jnp.exp2(ref - b)).astype(BF16))
            scores = scores + (s_l if 2 * m == c else jnp.where((tt // (2 * m)) == (ss // (2 * m)), s_l, 0.0))
        finish(ci, q, k, b, v, scores)
        return carry

    worst = lax.fori_loop(0, nc, intra_fast, jnp.zeros((1, LANES), F32), unroll=8)

    @pl.when(jnp.max(worst) > HGRN_MAX_BLOCK_DECAY)
    def _():
        lax.fori_loop(0, nc, intra_safe, 0, unroll=2)

    def inter(ci, state_t):
        r0 = pl.multiple_of(ci * c, c)
        gate = gt_ref[pl.ds(r0, c), :].astype(F32)
        o = oi_scr[ci] + _nt_dot(qd_scr[ci], state_t.astype(BF16))
        y = o * lax.rsqrt(jnp.mean(o * o, axis=-1, keepdims=True) + EPS) * og
        o_ref[pl.ds(r0, c), :] = (y * (gate * jax.nn.sigmoid(gate))).astype(o_ref.dtype)
        return state_t * d_scr[ci] + u_scr[ci]

    lax.fori_loop(0, nc, inter, jnp.zeros((C_VDIM, C_KDIM), F32), unroll=8)


def _hgrn(proj, lb, og, bsz, seq):
    t, n4 = proj.shape
    heads = n4 // (4 * LANES)
    nc = seq // C_CHUNK
    return pl.pallas_call(
        _hgrn_kernel,
        out_shape=jax.ShapeDtypeStruct((t, heads * C_VDIM), BF16),
        grid=(bsz, heads),
        in_specs=[
            pl.BlockSpec((seq, LANES), lambda b, h: (b, h)),
            pl.BlockSpec((seq, LANES), lambda b, h: (b, heads + h)),
            pl.BlockSpec((seq, LANES), lambda b, h: (b, 2 * heads + h)),
            pl.BlockSpec((seq, LANES), lambda b, h: (b, 3 * heads + h)),
            pl.BlockSpec((1, LANES), lambda b, h: (0, h)),
            pl.BlockSpec((1, LANES), lambda b, h: (0, 0)),
        ],
        out_specs=pl.BlockSpec((seq, LANES), lambda b, h: (b, h)),
        scratch_shapes=[pltpu.VMEM((nc, C_CHUNK, C_VDIM), F32),
                        pltpu.VMEM((nc, C_CHUNK, C_KDIM), BF16),
                        pltpu.VMEM((nc, C_VDIM, C_KDIM), F32),
                        pltpu.VMEM((nc, 1, C_KDIM), F32)],
        compiler_params=_params(("arbitrary", "arbitrary")),
        name="hgrn2",
    )(proj, proj, proj, proj, lb.reshape(1, -1), og.reshape(1, LANES))


def _first_max(vals):
    best, idx = vals[0], jnp.zeros(vals[0].shape, jnp.int32)
    for i in range(1, len(vals)):
        better = vals[i] > best
        best = jnp.where(better, vals[i], best)
        idx = jnp.where(better, i, idx)
    return best, idx


def _pick(idx, vals):
    out = vals[-1]
    for i in range(len(vals) - 2, -1, -1):
        out = jnp.where(idx == i, vals[i], out)
    return out


def _router_kernel(x_ref, g_ref, sh_ref, sc_ref, rw_ref, rb_ref,
                   h_ref, eidx_ref, wsel_ref, rank_ref, cnt_ref, carry_scr):
    i = pl.program_id(0)
    tm = x_ref.shape[0]

    @pl.when(i == 0)
    def _():
        carry_scr[...] = jnp.zeros_like(carry_scr)

    h = _norm_mod(x_ref[...], g_ref[...], sh_ref[0], sc_ref[0])
    h_ref[...] = h
    rw = rw_ref[...]
    w_hi = rw.astype(BF16)
    w_lo = (rw - w_hi.astype(F32)).astype(BF16)
    h_hi = h.astype(BF16)
    h_lo = (h - h_hi.astype(F32)).astype(BF16)
    head = _nt_dot(jnp.concatenate([w_hi, w_lo], axis=0), h_hi)
    logits = head[:N_EXPERTS] + head[N_EXPERTS:] + _nt_dot(w_hi, h_lo)
    scores = jax.nn.sigmoid(logits)
    sel = scores + rb_ref[...]
    epg = EXPERTS_PER_GROUP
    sel_rows = [sel[e:e + 1, :] for e in range(N_EXPERTS)]
    sc_rows = [scores[e:e + 1, :] for e in range(N_EXPERTS)]
    gsum = []
    for g in range(N_GROUPS):
        r = sel_rows[g * epg:(g + 1) * epg]
        pair = None
        for a in range(epg):
            for b in range(a + 1, epg):
                pair = r[a] + r[b] if pair is None else jnp.maximum(pair, r[a] + r[b])
        gsum.append(pair)
    _, g_idx = _first_max(gsum)
    cand = [_pick(g_idx, [sel_rows[g * epg + j] for g in range(N_GROUPS)]) for j in range(epg)]
    cand_sc = [_pick(g_idx, [sc_rows[g * epg + j] for g in range(N_GROUPS)]) for j in range(epg)]
    _, i1 = _first_max(cand)
    _, i2 = _first_max([jnp.where(i1 == j, -jnp.inf, cand[j]) for j in range(epg)])
    w1 = _pick(i1, cand_sc)
    w2 = _pick(i2, cand_sc)
    tot = w1 + w2
    e1 = g_idx * epg + i1
    e2 = g_idx * epg + i2
    eidx_ref[0:1, :] = e1
    eidx_ref[1:2, :] = e2
    wsel_ref[0:1, :] = w1 / tot
    wsel_ref[1:2, :] = w2 / tot

    erow = lax.broadcasted_iota(jnp.int32, (N_EXPERTS, tm), 0)
    oh1 = erow == e1
    oh2 = erow == e2
    onehot = jnp.where(oh1 | oh2, 1.0, 0.0)
    si = lax.broadcasted_iota(jnp.int32, (tm, tm), 0)
    ti = lax.broadcasted_iota(jnp.int32, (tm, tm), 1)
    incl = jnp.where(si <= ti, 1.0, 0.0).astype(BF16)
    csum = jnp.dot(onehot.astype(BF16), incl, preferred_element_type=F32)
    carry = carry_scr[...]
    before = csum - onehot + carry[:, 0:1]
    r1 = jnp.sum(jnp.where(oh1, before, 0.0), axis=0, keepdims=True)
    r2 = jnp.sum(jnp.where(oh2, before, 0.0), axis=0, keepdims=True)
    rank_ref[0:1, :] = r1.astype(jnp.int32)
    rank_ref[1:2, :] = r2.astype(jnp.int32)
    new_carry = carry + csum[:, tm - 1:tm]
    carry_scr[...] = new_carry
    cnt_ref[...] = new_carry.astype(jnp.int32)


def _router(x, g, shift, scale, rw_t, rbias, seq):
    t, d = x.shape
    tm = ROW_TILE
    e = rw_t.shape[0]
    row = lambda i: (0, i)
    return pl.pallas_call(
        _router_kernel,
        out_shape=(jax.ShapeDtypeStruct((t, d), F32),
                   jax.ShapeDtypeStruct((TOP_K, t), jnp.int32),
                   jax.ShapeDtypeStruct((TOP_K, t), F32),
                   jax.ShapeDtypeStruct((TOP_K, t), jnp.int32),
                   jax.ShapeDtypeStruct((e, LANES), jnp.int32)),
        grid=(t // tm,),
        in_specs=[
            pl.BlockSpec((tm, d), lambda i: (i, 0)),
            pl.BlockSpec((1, d), lambda i: (0, 0)),
            pl.BlockSpec((1, 1, d), lambda i: (i * tm // seq, 0, 0)),
            pl.BlockSpec((1, 1, d), lambda i: (i * tm // seq, 0, 0)),
            pl.BlockSpec((e, d), lambda i: (0, 0)),
            pl.BlockSpec((e, 1), lambda i: (0, 0)),
        ],
        out_specs=(pl.BlockSpec((tm, d), lambda i: (i, 0)),
                   pl.BlockSpec((TOP_K, tm), row),
                   pl.BlockSpec((TOP_K, tm), row),
                   pl.BlockSpec((TOP_K, tm), row),
                   pl.BlockSpec((e, LANES), lambda i: (0, 0))),
        scratch_shapes=[pltpu.VMEM((e, LANES), F32)],
        compiler_params=_params(("arbitrary",)),
        name="router",
    )(x, g.reshape(1, d), shift, scale, rw_t, rbias.reshape(e, 1))


def _invert_kernel(pos_ref, fill_ref, src_ref, *, n_tok):
    def clear(p, carry):
        src_ref[p] = 0
        return carry

    for e in range(N_EXPERTS + 1):
        lax.fori_loop(fill_ref[2 * e], fill_ref[2 * e + 1], clear, 0)

    for slot in range(TOP_K):
        def place(tok, carry, slot=slot):
            src_ref[pos_ref[slot * n_tok + tok]] = tok
            return carry

        lax.fori_loop(0, n_tok, place, 0, unroll=8)


def _invert_positions(pos_flat, fill, n_rows, n_tok):
    assert pos_flat.shape[0] == TOP_K * n_tok
    return pl.pallas_call(
        functools.partial(_invert_kernel, n_tok=n_tok),
        out_shape=jax.ShapeDtypeStruct((n_rows,), jnp.int32),
        in_specs=[pl.BlockSpec(memory_space=pltpu.SMEM), pl.BlockSpec(memory_space=pltpu.SMEM)],
        out_specs=pl.BlockSpec(memory_space=pltpu.SMEM),
        name="invert_positions",
    )(pos_flat, fill)


def _expert_kernel(te_ref, tv_ref, tf_ref, tn_ref, src_ref, h_hbm, wg_hbm, wu_hbm, wd_hbm, y_ref,
                   xbuf0, xbuf1, sg, su, sd, bg, bu, bd, gsem, wsem, *, layer):
    i = pl.program_id(0)
    xbufs = (xbuf0, xbuf1)
    tm = xbuf0.shape[0]

    def gather(tile, buf):
        for r in range(tm):
            tok = src_ref[tile * tm + r]
            pltpu.make_async_copy(h_hbm.at[pl.ds(tok, 1)], xbufs[buf].at[pl.ds(r, 1)],
                                  gsem.at[buf]).start(priority=r % 2)

    def fetch(e):
        return (pltpu.make_async_copy(wg_hbm.at[layer, e], sg, wsem.at[0]),
                pltpu.make_async_copy(wu_hbm.at[layer, e], su, wsem.at[1]),
                pltpu.make_async_copy(wd_hbm.at[layer, e], sd, wsem.at[2]))

    @pl.when(i == 0)
    def _():
        for c in fetch(te_ref[0]):
            c.start(priority=WEIGHT_DMA_PRIORITY)
        gather(0, 0)

    pending = (i == 0) | (tv_ref[jnp.maximum(i - 1, 0)] > 0)
    for par in range(2):
        @pl.when(pending & (i % 2 == par))
        def _(par=par):
            pltpu.make_async_copy(h_hbm.at[pl.ds(0, tm)], xbufs[par], gsem.at[par]).wait()

    @pl.when(tf_ref[i] > 0)
    def _():
        for c in fetch(te_ref[i]):
            c.wait()
        bg[...] = sg[...].astype(BF16)
        bu[...] = su[...].astype(BF16)
        bd[...] = sd[...].astype(BF16)

        @pl.when(tn_ref[i] >= 0)
        def _():
            for c in fetch(tn_ref[i]):
                c.start(priority=WEIGHT_DMA_PRIORITY)

    for par in range(2):
        @pl.when((tv_ref[i] > 0) & (i % 2 == par))
        def _(par=par):
            gather(i + 1, 1 - par)
            x = xbufs[par][...].astype(BF16)
            a = jnp.dot(x, bg[...], preferred_element_type=F32)
            u = jnp.dot(x, bu[...], preferred_element_type=F32)
            hid = (a * jax.nn.sigmoid(a) * u).astype(BF16)
            y_ref[...] = jnp.dot(hid, bd[...], preferred_element_type=F32)

    @pl.when(tv_ref[i] == 0)
    def _():
        y_ref[...] = jnp.zeros_like(y_ref)


def _experts(tile_expert, tile_valid, tile_first, tile_next, src, h, w_gate, w_up, w_down, layer, n_rows):
    d = h.shape[1]
    f = w_gate.shape[3]
    tm = EXPERT_TILE
    grid_spec = pltpu.PrefetchScalarGridSpec(
        num_scalar_prefetch=5,
        grid=(n_rows // tm,),
        in_specs=[pl.BlockSpec(memory_space=pl.ANY)] * 4,
        out_specs=pl.BlockSpec((tm, d), lambda i, *_: (i, 0)),
        scratch_shapes=[pltpu.VMEM((tm, d), F32), pltpu.VMEM((tm, d), F32),
                        pltpu.VMEM((d, f), F32), pltpu.VMEM((d, f), F32), pltpu.VMEM((f, d), F32),
                        pltpu.VMEM((d, f), BF16), pltpu.VMEM((d, f), BF16), pltpu.VMEM((f, d), BF16),
                        pltpu.SemaphoreType.DMA((2,)), pltpu.SemaphoreType.DMA((3,))],
    )
    return pl.pallas_call(
        functools.partial(_expert_kernel, layer=layer),
        out_shape=jax.ShapeDtypeStruct((n_rows, d), F32),
        grid_spec=grid_spec,
        compiler_params=_params(("arbitrary",)),
        name="experts",
    )(tile_expert, tile_valid, tile_first, tile_next, src, h, w_gate, w_up, w_down)


def _combine_kernel(pos_ref, x_ref, w_ref, g_ref, y_ref, o_ref, buf0, buf1, sem):
    i = pl.program_id(0)
    n = pl.num_programs(0)
    tm = x_ref.shape[0]
    t = n * tm
    bufs = (buf0, buf1)

    def gather(tile, par):
        for r in range(tm):
            for slot in range(TOP_K):
                p = pos_ref[slot * t + tile * tm + r]
                pltpu.make_async_copy(y_ref.at[pl.ds(p, 1)], bufs[par].at[slot, pl.ds(r, 1)],
                                      sem.at[par]).start(priority=slot)

    def combine(par):
        w = w_ref[...]
        y = w[:, 0:1] * bufs[par][0] + w[:, 1:2] * bufs[par][1]
        o_ref[...] = x_ref[...] + g_ref[0] * y

    @pl.when(i == 0)
    def _():
        gather(0, 0)

    for par in range(2):
        @pl.when(i % 2 == par)
        def _(par=par):
            for slot in range(TOP_K):
                pltpu.make_async_copy(y_ref.at[pl.ds(0, tm)], bufs[par].at[slot], sem.at[par]).wait()

        @pl.when((i % 2 == par) & (i + 1 < n))
        def _(par=par):
            gather(i + 1, 1 - par)
            combine(par)

        @pl.when((i % 2 == par) & (i + 1 == n))
        def _(par=par):
            combine(par)


def _combine(pos_flat, x, wsel_t, gate, y, seq):
    t, d = x.shape
    tm = ROW_TILE
    assert seq % tm == 0 and t % tm == 0
    grid_spec = pltpu.PrefetchScalarGridSpec(
        num_scalar_prefetch=1,
        grid=(t // tm,),
        in_specs=[
            pl.BlockSpec((tm, d), lambda i, p: (i, 0)),
            pl.BlockSpec((tm, TOP_K), lambda i, p: (i, 0)),
            pl.BlockSpec((1, 1, d), lambda i, p: (i * tm // seq, 0, 0)),
            pl.BlockSpec(memory_space=pl.ANY),
        ],
        out_specs=pl.BlockSpec((tm, d), lambda i, p: (i, 0)),
        scratch_shapes=[pltpu.VMEM((TOP_K, tm, d), F32), pltpu.VMEM((TOP_K, tm, d), F32),
                        pltpu.SemaphoreType.DMA((2,))],
    )
    return pl.pallas_call(
        _combine_kernel,
        out_shape=jax.ShapeDtypeStruct((t, d), F32),
        grid_spec=grid_spec,
        compiler_params=_params(("arbitrary",)),
        name="combine",
    )(pos_flat, x, wsel_t, gate, y)


def _moe(x, g, shift, scale, gate, rw_t, rbias, w_gate, w_up, w_down, layer, seq):
    t, d = x.shape
    te = EXPERT_TILE
    n_rows = TOP_K * t + N_EXPERTS * te
    n_tiles = n_rows // te
    h, eidx, wsel, rank, cnt = _router(x, g, shift, scale, rw_t, rbias, seq)
    counts = cnt[:, 0]
    padded = ((counts + te - 1) // te) * te
    ends = jnp.cumsum(padded)
    starts = ends - padded
    pos = rank
    for e in range(N_EXPERTS):
        pos = pos + jnp.where(eidx == e, starts[e], 0)
    pos_flat = pos.reshape(-1)
    tile_start = jnp.arange(n_tiles, dtype=jnp.int32) * te
    tile_expert = jnp.minimum(jnp.sum(tile_start[:, None] >= ends[None, :], axis=1), N_EXPERTS - 1).astype(jnp.int32)
    tile_valid = (tile_start < ends[-1]).astype(jnp.int32)
    tile_expert = jnp.where(tile_valid > 0, tile_expert, tile_expert[jnp.maximum(ends[-1] // te - 1, 0)])
    prev_expert = jnp.concatenate([jnp.full((1,), -1, jnp.int32), tile_expert[:-1]])
    tile_first = ((tile_valid > 0) & (tile_expert != prev_expert)).astype(jnp.int32)
    later = (tile_expert[None, :] > tile_expert[:, None]) & (tile_valid[None, :] > 0)
    tile_next = jnp.min(jnp.where(later, tile_expert[None, :], N_EXPERTS), axis=1)
    tile_next = jnp.where(tile_next < N_EXPERTS, tile_next, -1).astype(jnp.int32)
    fill = jnp.stack([jnp.concatenate([starts + counts, ends[-1:]]),
                      jnp.concatenate([ends, jnp.full((1,), n_rows, ends.dtype)])], axis=1).reshape(-1).astype(jnp.int32)
    src = _invert_positions(pos_flat, fill, n_rows, t)
    y = _experts(tile_expert, tile_valid, tile_first, tile_next, src, h, w_gate, w_up, w_down, layer, n_rows)
    return _combine(pos_flat, x, wsel.T, gate, y, seq)


def kernel(x, c, positions, ada_w, ada_b, norm1_g, norm2_g, ab_w_in, q_norm_g, k_norm_g, sinks, gm_vnorm_g, gm_ws, gm_b, ab_w_out, c_w_in, c_lower_bounds, c_onorm_g, c_w_out, router_w, router_bias, moe_w_gate, moe_w_up, moe_w_down):
    bsz, seq, d = x.shape
    depth = ada_w.shape[0]
    t = bsz * seq
    xt = x.reshape(t, d)

    c_pad = jnp.concatenate([c, jnp.zeros((8 - bsz % 8, d), F32)], axis=0) if bsz % 8 else c
    mod = _ada_mod(c_pad, ada_w, ada_b)[:, :bsz]
    mod = mod.reshape(depth, bsz, 6, 1, d)
    cos_t, sin_t = _rope_tables(positions)

    sm = jax.nn.softmax(c_lower_bounds.astype(F32), axis=0)
    lower_bounds = jnp.cumsum(sm, axis=0) - sm[0:1]
    rw_t = router_w.T
    ab_in, ab_out, c_in, c_out = (w.astype(BF16) for w in (ab_w_in, ab_w_out, c_w_in, c_w_out))

    for l in range(depth):
        sh1, sc1, g1, sh2, sc2, g2 = (mod[l, :, k] for k in range(6))
        if l % 2 == 0:
            i = l // 2
            proj = _in_proj(xt, norm1_g[l], sh1, sc1, ab_in, i, seq)
            mix = _mix_even(proj, cos_t, sin_t, q_norm_g[i], k_norm_g[i], sinks[i],
                            gm_vnorm_g[i], gm_ws[i], gm_b[i], bsz, seq)
            xt = _out_proj(mix, ab_out, i, xt, g1, seq)
        else:
            j = l // 2
            proj = _in_proj(xt, norm1_g[l], sh1, sc1, c_in, j, seq)
            o = _hgrn(proj, lower_bounds[l], c_onorm_g[j], bsz, seq)
            xt = _out_proj(o, c_out, j, xt, g1, seq)
        xt = _moe(xt, norm2_g[l], sh2, sc2, g2, rw_t, router_bias,
                  moe_w_gate, moe_w_up, moe_w_down, l, seq)
    return xt.reshape(bsz, seq, d)
```

```python
import functools

import jax
import jax.numpy as jnp
from jax import lax
from jax.experimental import pallas as pl
from jax.experimental.pallas import tpu as pltpu

F32 = jnp.float32
BF16 = jnp.bfloat16

EPS = 1e-6
HEAD_DIM = 64
KV_REP = 4
WINDOW = 128
ROPE_THETA = 10000.0
B_GROUP_DIM = 128
B_CHUNK = 128
C_KDIM = 128
C_VDIM = 128
C_CHUNK = 64
N_GROUPS = 4
EXPERTS_PER_GROUP = 4
N_EXPERTS = N_GROUPS * EXPERTS_PER_GROUP
TOP_K = 2

LANES = 128
SUBLANES = 8
VMEM_LIMIT = 56 * 1024 * 1024

ROW_TILE = 512
PROJ_ROW_TILE = 1024
PROJ_COL_TILE = 2048
EXPERT_TILE = 256
WEIGHT_DMA_PRIORITY = 1
HGRN_BLOCK = 16
HGRN_MAX_BLOCK_DECAY = 100.0


def _nt_dot(a, b):
    return lax.dot_general(a, b, (((1,), (1,)), ((), ())), preferred_element_type=F32)


def _pack_halves(x):
    n = x.shape[1] // 2
    lo = lax.bitcast_convert_type(x[:, :n].astype(BF16).astype(F32), jnp.uint32)
    hi = lax.bitcast_convert_type(x[:, n:].astype(BF16).astype(F32), jnp.uint32)
    return (lo >> 16) | hi


def _unpack_halves(w):
    lo = lax.bitcast_convert_type(w << 16, F32)
    hi = lax.bitcast_convert_type(w & jnp.uint32(0xFFFF0000), F32)
    return lo, hi


def _params(sem):
    return pltpu.CompilerParams(dimension_semantics=sem, vmem_limit_bytes=VMEM_LIMIT)


def _ada_kernel(c_ref, w_ref, b_ref, o_ref):
    c = c_ref[...]
    cond = c * jax.nn.sigmoid(c)
    o_ref[0] = jnp.dot(cond, w_ref[0], preferred_element_type=F32) + b_ref[0]


def _ada_mod(c_pad, ada_w, ada_b):
    depth, d, n = ada_w.shape
    tn = 1536
    rows = c_pad.shape[0]
    return pl.pallas_call(
        _ada_kernel,
        out_shape=jax.ShapeDtypeStruct((depth, rows, n), F32),
        grid=(depth, n // tn),
        in_specs=[
            pl.BlockSpec((rows, d), lambda l, j: (0, 0)),
            pl.BlockSpec((1, d, tn), lambda l, j: (l, 0, j)),
            pl.BlockSpec((1, 1, tn), lambda l, j: (l, 0, j)),
        ],
        out_specs=pl.BlockSpec((1, rows, tn), lambda l, j: (l, 0, j)),
        compiler_params=_params(("arbitrary", "arbitrary")),
        name="ada_mod",
    )(c_pad, ada_w, ada_b.reshape(depth, 1, n))


def _rope_table_kernel(pos_ref, freq_ref, sign_ref, cos_ref, sin_ref):
    ang = pos_ref[...].astype(F32) * freq_ref[...]
    cos_ref[...] = jnp.cos(ang)
    sin_ref[...] = jnp.sin(ang) * sign_ref[...]


def _rope_tables(positions):
    t = positions.size
    half = HEAD_DIM // 2
    inv_freq = ROPE_THETA ** (-jnp.arange(half, dtype=F32) / half)
    freq = jnp.tile(inv_freq, LANES // half).reshape(1, LANES)
    sign = jnp.tile(jnp.concatenate([-jnp.ones((half,), F32), jnp.ones((half,), F32)]),
                    LANES // HEAD_DIM).reshape(1, LANES)
    tm = min(1024, t)
    return pl.pallas_call(
        _rope_table_kernel,
        out_shape=(jax.ShapeDtypeStruct((t, LANES), F32), jax.ShapeDtypeStruct((t, LANES), F32)),
        grid=(t // tm,),
        in_specs=[
            pl.BlockSpec((tm, 1), lambda i: (i, 0)),
            pl.BlockSpec((1, LANES), lambda i: (0, 0)),
            pl.BlockSpec((1, LANES), lambda i: (0, 0)),
        ],
        out_specs=(pl.BlockSpec((tm, LANES), lambda i: (i, 0)),
                   pl.BlockSpec((tm, LANES), lambda i: (i, 0))),
        compiler_params=_params(("arbitrary",)),
        name="rope_tables",
    )(positions.reshape(t, 1), freq, sign)


def _norm_mod(x, g, shift, scale):
    y = x * lax.rsqrt(jnp.mean(x * x, axis=-1, keepdims=True) + EPS)
    return (y * g) * (1 + scale) + shift


def _in_proj_kernel(x_ref, g_ref, sh_ref, sc_ref, w_ref, o_ref, h_scr):
    @pl.when(pl.program_id(1) == 0)
    def _():
        h_scr[...] = _norm_mod(x_ref[...], g_ref[...], sh_ref[0], sc_ref[0]).astype(BF16)

    o_ref[...] = jnp.dot(h_scr[...], w_ref[0], preferred_element_type=F32).astype(o_ref.dtype)


def _in_proj(x, g, shift, scale, w, layer, seq):
    t, d = x.shape
    n = w.shape[2]
    tm = PROJ_ROW_TILE
    tn = max(c for c in range(LANES, PROJ_COL_TILE + 1, LANES) if n % c == 0)
    assert seq % tm == 0 and t % tm == 0
    return pl.pallas_call(
        _in_proj_kernel,
        out_shape=jax.ShapeDtypeStruct((t, n), BF16),
        grid=(t // tm, n // tn),
        in_specs=[
            pl.BlockSpec((tm, d), lambda i, j: (i, 0)),
            pl.BlockSpec((1, d), lambda i, j: (0, 0)),
            pl.BlockSpec((1, 1, d), lambda i, j: (i * tm // seq, 0, 0)),
            pl.BlockSpec((1, 1, d), lambda i, j: (i * tm // seq, 0, 0)),
            pl.BlockSpec((1, d, tn), lambda i, j: (layer, 0, j)),
        ],
        out_specs=pl.BlockSpec((tm, tn), lambda i, j: (i, j)),
        scratch_shapes=[pltpu.VMEM((tm, d), BF16)],
        compiler_params=_params(("arbitrary", "arbitrary")),
        name="in_proj",
    )(x, g.reshape(1, d), shift, scale, w)


def _out_proj_kernel(a_ref, w_ref, x_ref, g_ref, o_ref):
    y = jnp.dot(a_ref[...], w_ref[0], preferred_element_type=F32)
    o_ref[...] = x_ref[...] + g_ref[0] * y


def _out_proj(a, w, layer, x, gate, seq):
    t, k = a.shape
    n = w.shape[2]
    tm, tn = PROJ_ROW_TILE, 1024
    assert seq % tm == 0 and t % tm == 0 and n % tn == 0
    return pl.pallas_call(
        _out_proj_kernel,
        out_shape=jax.ShapeDtypeStruct((t, n), F32),
        grid=(t // tm, n // tn),
        in_specs=[
            pl.BlockSpec((tm, k), lambda i, j: (i, 0)),
            pl.BlockSpec((1, k, tn), lambda i, j: (layer, 0, j)),
            pl.BlockSpec((tm, tn), lambda i, j: (i, j)),
            pl.BlockSpec((1, 1, tn), lambda i, j: (i * tm // seq, 0, j)),
        ],
        out_specs=pl.BlockSpec((tm, tn), lambda i, j: (i, j)),
        compiler_params=_params(("arbitrary", "arbitrary")),
        name="out_proj",
    )(a, w, x, gate)


def _segment_mean_matrix():
    r = lax.broadcasted_iota(jnp.int32, (LANES, LANES), 0) // HEAD_DIM
    c = lax.broadcasted_iota(jnp.int32, (LANES, LANES), 1) // HEAD_DIM
    return jnp.where(r == c, 1.0 / HEAD_DIM, 0.0).astype(F32)


def _head_norm_rope(x, gain, cos, sin, seg_mean, low_half):
    ms = jnp.dot(x * x, seg_mean, preferred_element_type=F32)
    y = x * lax.rsqrt(ms + EPS) * gain
    partner = jnp.where(low_half, pltpu.roll(y, LANES - HEAD_DIM // 2, 1),
                        pltpu.roll(y, HEAD_DIM // 2, 1))
    return y * cos + partner * sin


def _gelu_tanh(x):
    return 0.5 * x * (1.0 + jnp.tanh(0.7978845608028654 * (x + 0.044715 * (x * x * x))))


def _mix_kernel(sinks_ref,
                q0_ref, q1_ref, kv_ref, u0_ref, u1_ref, v0_ref, v1_ref,
                cos_ref, sin_ref, qg_ref, kg_ref, vng_ref, ws_ref, bs_ref,
                o_ref, kd_scr, vd_scr):
    n = pl.program_id(1)
    w = WINDOW
    n_kv = kd_scr.shape[0]

    @pl.when(n == 0)
    def _():
        kd_scr[...] = jnp.zeros_like(kd_scr)
        vd_scr[...] = jnp.zeros_like(vd_scr)

    lane = lax.broadcasted_iota(jnp.int32, (w, LANES), 1)
    low_half = (lane % HEAD_DIM) < (HEAD_DIM // 2)
    first_head = lane < HEAD_DIM
    seg_mean = _segment_mean_matrix()
    cos = cos_ref[...]
    sin = sin_ref[...]

    kv = kv_ref[...].astype(F32)
    for c in range(n_kv // 2):
        kc = _head_norm_rope(kv[:, c * LANES:(c + 1) * LANES], kg_ref[...], cos, sin, seg_mean, low_half)
        vc = kv[:, (n_kv // 2 + c) * LANES:(n_kv // 2 + c + 1) * LANES]
        kc_sw = pltpu.roll(kc, HEAD_DIM, 1)
        vc_sw = pltpu.roll(vc, HEAD_DIM, 1)
        kd_scr[2 * c, w:, :] = jnp.where(first_head, kc, kc_sw).astype(BF16)
        kd_scr[2 * c + 1, w:, :] = jnp.where(first_head, kc_sw, kc).astype(BF16)
        vd_scr[2 * c, w:, :] = jnp.where(first_head, vc, vc_sw).astype(BF16)
        vd_scr[2 * c + 1, w:, :] = jnp.where(first_head, vc_sw, vc).astype(BF16)

    qi = lax.broadcasted_iota(jnp.int32, (w, 2 * w), 0)
    kj = lax.broadcasted_iota(jnp.int32, (w, 2 * w), 1)
    diff = qi + w - kj
    band = (diff >= 0) & (diff < w) & ((kj >= w) | (n > 0))
    band4 = jnp.concatenate([band] * KV_REP, axis=0)

    scale = HEAD_DIM ** -0.5
    ones_kv = jnp.ones((2 * w, LANES), BF16)
    q_refs = (q0_ref, q1_ref)
    chunks_per_ref = q0_ref.shape[1] // LANES
    for g in range(n_kv):
        rows = []
        for cc in range(2):
            c = 2 * g + cc
            qc = q_refs[c // chunks_per_ref][:, (c % chunks_per_ref) * LANES:(c % chunks_per_ref + 1) * LANES]
            qc = _head_norm_rope(qc.astype(F32), qg_ref[...], cos, sin, seg_mean, low_half)
            rows.append(jnp.where(first_head, qc, 0.0))
            rows.append(jnp.where(first_head, 0.0, qc))
        qm = jnp.concatenate(rows, axis=0).astype(BF16)
        s = lax.dot_general(qm, kd_scr[g], (((1,), (1,)), ((), ())),
                            preferred_element_type=F32) * scale
        s = jnp.where(band4, s, -jnp.inf)
        sink = jnp.concatenate(
            [jnp.full((w, LANES), sinks_ref[KV_REP * g + r], F32) for r in range(KV_REP)], axis=0)
        m = jnp.maximum(jnp.broadcast_to(jnp.max(s, axis=-1, keepdims=True), (KV_REP * w, LANES)), sink)
        p = jnp.exp(s - jnp.concatenate([m, m], axis=1)).astype(BF16)
        total = jnp.dot(p, ones_kv, preferred_element_type=F32) + jnp.exp(sink - m)
        o = jnp.dot(p, vd_scr[g], preferred_element_type=F32) / total
        for cc in range(2):
            oc = jnp.where(first_head, o[(2 * cc) * w:(2 * cc + 1) * w], o[(2 * cc + 1) * w:(2 * cc + 2) * w])
            c = 2 * g + cc
            o_ref[:, c * LANES:(c + 1) * LANES] = oc.astype(o_ref.dtype)

    kd_scr[:, :w, :] = kd_scr[:, w:, :]
    vd_scr[:, :w, :] = vd_scr[:, w:, :]

    a_q = 2 * q0_ref.shape[1]
    u = _gelu_tanh(jnp.concatenate([u0_ref[...], u1_ref[...]], axis=1).astype(F32))
    v = _gelu_tanh(jnp.concatenate([v0_ref[...], v1_ref[...]], axis=1).astype(F32))
    mu = jnp.mean(v, axis=-1, keepdims=True)
    vc = v - mu
    vn = (vc * lax.rsqrt(jnp.mean(vc * vc, axis=-1, keepdims=True) + EPS) * vng_ref[...]).astype(BF16)
    ti = lax.broadcasted_iota(jnp.int32, (w, w), 0)
    si = lax.broadcasted_iota(jnp.int32, (w, w), 1)
    causal = si <= ti
    bs = bs_ref[...]
    for g in range(ws_ref.shape[0]):
        wg = jnp.where(causal, ws_ref[g], 0.0).astype(BF16)
        sv = jnp.dot(wg, vn[:, g * LANES:(g + 1) * LANES], preferred_element_type=F32) + bs[:, g:g + 1]
        o_ref[:, a_q + g * LANES:a_q + (g + 1) * LANES] = (u[:, g * LANES:(g + 1) * LANES] * sv).astype(o_ref.dtype)


def _mix_even(proj, cos_t, sin_t, q_g, k_g, sinks, vnorm_g, ws, bs, bsz, seq):
    t = proj.shape[0]
    w = WINDOW
    nb = seq // w
    heads = sinks.shape[0]
    n_kv = heads // KV_REP
    a_q = heads * HEAD_DIM
    a_kv = n_kv * HEAD_DIM
    b_w = vnorm_g.shape[0]
    cw = 512
    assert a_q == 2 * cw and 2 * a_kv == cw and b_w == 2 * cw
    row = lambda b, n: b * nb + n
    gain2 = lambda g: jnp.tile(g, LANES // HEAD_DIM).reshape(1, LANES)
    col = lambda j: pl.BlockSpec((w, cw), lambda b, n, s: (row(b, n), j))
    grid_spec = pltpu.PrefetchScalarGridSpec(
        num_scalar_prefetch=1,
        grid=(bsz, nb),
        in_specs=[
            col(0), col(1), col(2), col(3), col(4), col(5), col(6),
            pl.BlockSpec((w, LANES), lambda b, n, s: (row(b, n), 0)),
            pl.BlockSpec((w, LANES), lambda b, n, s: (row(b, n), 0)),
            pl.BlockSpec((1, LANES), lambda b, n, s: (0, 0)),
            pl.BlockSpec((1, LANES), lambda b, n, s: (0, 0)),
            pl.BlockSpec((1, b_w), lambda b, n, s: (0, 0)),
            pl.BlockSpec(ws.shape, lambda b, n, s: (0, 0, 0)),
            pl.BlockSpec((w, ws.shape[0]), lambda b, n, s: (0, 0)),
        ],
        out_specs=pl.BlockSpec((w, a_q + b_w), lambda b, n, s: (row(b, n), 0)),
        scratch_shapes=[pltpu.VMEM((n_kv, 2 * w, LANES), BF16),
                        pltpu.VMEM((n_kv, 2 * w, LANES), BF16)],
    )
    return pl.pallas_call(
        _mix_kernel,
        out_shape=jax.ShapeDtypeStruct((t, a_q + b_w), BF16),
        grid_spec=grid_spec,
        compiler_params=_params(("arbitrary", "arbitrary")),
        name="mix_even",
    )(sinks, proj, proj, proj, proj, proj, proj, proj, cos_t, sin_t,
      gain2(q_g), gain2(k_g), vnorm_g.reshape(1, b_w), ws, bs.T)


def _hgrn_kernel(q_ref, f_ref, i_ref, gt_ref, lb_ref, og_ref, o_ref,
                 oi_scr, qd_scr, u_scr, d_scr):
    c = C_CHUNK
    nc = q_ref.shape[0] // c
    nv = c // SUBLANES
    lb = lb_ref[...]
    og = og_ref[...]
    t_row = lax.broadcasted_iota(jnp.int32, (c, LANES), 0)
    tt = lax.broadcasted_iota(jnp.int32, (c, c), 0)
    ss = lax.broadcasted_iota(jnp.int32, (c, c), 1)

    def inputs(ci):
        r0 = pl.multiple_of(ci * c, c)
        q = q_ref[pl.ds(r0, c), :].astype(F32)
        fl = f_ref[pl.ds(r0, c), :].astype(F32)
        v = i_ref[pl.ds(r0, c), :]
        f = lb + (1 - lb) * jax.nn.sigmoid(fl)
        k = 1 - f
        b = jnp.log2(f)
        for d in (1, 2, 4, 8, 16, 32):
            b = b + jnp.where(t_row >= d, pltpu.roll(b, d, 0), 0.0)
        return q, f, k, b, v

    def row(b, r, n):
        b3 = b.reshape(nv, SUBLANES, LANES)
        rep = jnp.broadcast_to(b3[r // SUBLANES:r // SUBLANES + 1, r % SUBLANES:r % SUBLANES + 1, :],
                               (n // SUBLANES, SUBLANES, LANES))
        return rep.reshape(n, LANES)

    def half_level(q, k, b, m):
        a_parts, b_parts = [], []
        for blk in range(c // (2 * m)):
            lo, mid, hi = blk * 2 * m, blk * 2 * m + m, (blk + 1) * 2 * m
            zero = jnp.zeros((m, LANES), BF16)
            ref = row(b, mid - 1, m)
            a_parts += [zero, (q[mid:hi] * jnp.exp2(b[mid:hi] - ref)).astype(BF16)]
            b_parts += [(k[lo:mid] * jnp.exp2(ref - b[lo:mid])).astype(BF16), zero]
        return _nt_dot(jnp.concatenate(a_parts, axis=0), jnp.concatenate(b_parts, axis=0))

    def finish(ci, q, k, b, v, scores):
        oi_scr[ci] = jnp.dot(scores.astype(BF16), v, preferred_element_type=F32)
        qd_scr[ci] = (q * jnp.exp2(b)).astype(BF16)
        b_last = b[c - 1:c, :]
        kd = (k * jnp.exp2(b_last - b)).astype(BF16)
        u_scr[ci] = lax.dot_general(v, kd, (((0,), (0,)), ((), ())), preferred_element_type=F32)
        d_scr[ci] = jnp.exp2(b_last)

    def intra_fast(ci, worst):
        q, f, k, b, v = inputs(ci)
        blk = HGRN_BLOCK
        starts = [jnp.zeros((blk, LANES), F32)] + [row(b, s - 1, blk) for s in range(blk, c, blk)]
        base = jnp.concatenate(starts, axis=0)
        s_d = _nt_dot((q * jnp.exp2(b - base)).astype(BF16), (k * jnp.exp2(base - b)).astype(BF16))
        s_16 = half_level(q, k, b, blk)
        s_32 = half_level(q, k, b, 2 * blk)
        scores = jnp.where(((tt // blk) == (ss // blk)) & (ss <= tt), s_d,
                           jnp.where((tt // (2 * blk)) == (ss // (2 * blk)), s_16, s_32))
        finish(ci, q, k, b, v, scores)
        for s in range(0, c, blk):
            worst = jnp.maximum(worst, base[s:s + 1, :] - b[s + blk - 1:s + blk, :])
        return worst

    def intra_safe(ci, carry):
        q, f, k, b, v = inputs(ci)
        scores = jnp.where(tt == ss, jnp.sum(q * k, axis=-1, keepdims=True), 0.0)
        odd = (t_row % 2) == 1
        s_1 = _nt_dot(jnp.where(odd, q * f, 0.0).astype(BF16), jnp.where(odd, 0.0, k).astype(BF16))
        scores = scores + jnp.where((tt // 2) == (ss // 2), s_1, 0.0)
        b3 = b.reshape(nv, SUBLANES, LANES)
        sub = lax.broadcasted_iota(jnp.int32, b3.shape, 1)
        small_refs = {4: jnp.broadcast_to(b3[:, 3:4, :], b3.shape).reshape(c, LANES),
                      2: jnp.where(sub < 4, jnp.broadcast_to(b3[:, 1:2, :], b3.shape),
                                   jnp.broadcast_to(b3[:, 5:6, :], b3.shape)).reshape(c, LANES)}
        for m in (32, 16, 8, 4, 2):
            if m >= SUBLANES:
                s_l = half_level(q, k, b, m)
            else:
                ref = small_refs[m]
                upper = (t_row % (2 * m)) >= m
                s_l = _nt_dot(jnp.where(upper, q * jnp.exp2(b - ref), 0.0).astype(BF16),
                              jnp.where(upper, 0.0, k * jnp.exp2(ref - b)).astype(BF16))
            scores = scores + (s_l if 2 * m == c else jnp.where((tt // (2 * m)) == (ss // (2 * m)), s_l, 0.0))
        finish(ci, q, k, b, v, scores)
        return carry

    worst = lax.fori_loop(0, nc, intra_fast, jnp.zeros((1, LANES), F32), unroll=8)

    @pl.when(jnp.max(worst) > HGRN_MAX_BLOCK_DECAY)
    def _():
        lax.fori_loop(0, nc, intra_safe, 0, unroll=2)

    def inter(ci, state_t):
        r0 = pl.multiple_of(ci * c, c)
        gate = gt_ref[pl.ds(r0, c), :].astype(F32)
        o = oi_scr[ci] + _nt_dot(qd_scr[ci], state_t.astype(BF16))
        y = o * lax.rsqrt(jnp.mean(o * o, axis=-1, keepdims=True) + EPS) * og
        o_ref[pl.ds(r0, c), :] = (y * (gate * jax.nn.sigmoid(gate))).astype(o_ref.dtype)
        return state_t * d_scr[ci] + u_scr[ci]

    lax.fori_loop(0, nc, inter, jnp.zeros((C_VDIM, C_KDIM), F32), unroll=8)


def _hgrn(proj, lb, og, bsz, seq):
    t, n4 = proj.shape
    heads = n4 // (4 * LANES)
    nc = seq // C_CHUNK
    return pl.pallas_call(
        _hgrn_kernel,
        out_shape=jax.ShapeDtypeStruct((t, heads * C_VDIM), BF16),
        grid=(bsz, heads),
        in_specs=[
            pl.BlockSpec((seq, LANES), lambda b, h: (b, h)),
            pl.BlockSpec((seq, LANES), lambda b, h: (b, heads + h)),
            pl.BlockSpec((seq, LANES), lambda b, h: (b, 2 * heads + h)),
            pl.BlockSpec((seq, LANES), lambda b, h: (b, 3 * heads + h)),
            pl.BlockSpec((1, LANES), lambda b, h: (0, h)),
            pl.BlockSpec((1, LANES), lambda b, h: (0, 0)),
        ],
        out_specs=pl.BlockSpec((seq, LANES), lambda b, h: (b, h)),
        scratch_shapes=[pltpu.VMEM((nc, C_CHUNK, C_VDIM), F32),
                        pltpu.VMEM((nc, C_CHUNK, C_KDIM), BF16),
                        pltpu.VMEM((nc, C_VDIM, C_KDIM), F32),
                        pltpu.VMEM((nc, 1, C_KDIM), F32)],
        compiler_params=_params(("arbitrary", "arbitrary")),
        name="hgrn2",
    )(proj, proj, proj, proj, lb.reshape(1, -1), og.reshape(1, LANES))


def _first_max(vals):
    best, idx = vals[0], jnp.zeros(vals[0].shape, jnp.int32)
    for i in range(1, len(vals)):
        better = vals[i] > best
        best = jnp.where(better, vals[i], best)
        idx = jnp.where(better, i, idx)
    return best, idx


def _pick(idx, vals):
    out = vals[-1]
    for i in range(len(vals) - 2, -1, -1):
        out = jnp.where(idx == i, vals[i], out)
    return out


def _router_kernel(x_ref, g_ref, sh_ref, sc_ref, rw_ref, rb_ref,
                   h_ref, eidx_ref, wsel_ref, rank_ref, cnt_ref, carry_scr):
    i = pl.program_id(0)
    tm = x_ref.shape[0]

    @pl.when(i == 0)
    def _():
        carry_scr[...] = jnp.zeros_like(carry_scr)

    h = _norm_mod(x_ref[...], g_ref[...], sh_ref[0], sc_ref[0])
    h_ref[...] = _pack_halves(h)
    rw = rw_ref[...]
    w_hi = rw.astype(BF16)
    w_lo = (rw - w_hi.astype(F32)).astype(BF16)
    h_hi = h.astype(BF16)
    h_lo = (h - h_hi.astype(F32)).astype(BF16)
    head = _nt_dot(jnp.concatenate([w_hi, w_lo], axis=0), h_hi)
    logits = head[:N_EXPERTS] + head[N_EXPERTS:] + _nt_dot(w_hi, h_lo)
    scores = jax.nn.sigmoid(logits)
    sel = scores + rb_ref[...]
    epg = EXPERTS_PER_GROUP
    sel_rows = [sel[e:e + 1, :] for e in range(N_EXPERTS)]
    sc_rows = [scores[e:e + 1, :] for e in range(N_EXPERTS)]
    gsum = []
    for g in range(N_GROUPS):
        r = sel_rows[g * epg:(g + 1) * epg]
        pair = None
        for a in range(epg):
            for b in range(a + 1, epg):
                pair = r[a] + r[b] if pair is None else jnp.maximum(pair, r[a] + r[b])
        gsum.append(pair)
    _, g_idx = _first_max(gsum)
    cand = [_pick(g_idx, [sel_rows[g * epg + j] for g in range(N_GROUPS)]) for j in range(epg)]
    cand_sc = [_pick(g_idx, [sc_rows[g * epg + j] for g in range(N_GROUPS)]) for j in range(epg)]
    _, i1 = _first_max(cand)
    _, i2 = _first_max([jnp.where(i1 == j, -jnp.inf, cand[j]) for j in range(epg)])
    w1 = _pick(i1, cand_sc)
    w2 = _pick(i2, cand_sc)
    tot = w1 + w2
    e1 = g_idx * epg + i1
    e2 = g_idx * epg + i2
    eidx_ref[0:1, :] = e1
    eidx_ref[1:2, :] = e2
    wsel_ref[0:1, :] = w1 / tot
    wsel_ref[1:2, :] = w2 / tot

    erow = lax.broadcasted_iota(jnp.int32, (N_EXPERTS, tm), 0)
    oh1 = erow == e1
    oh2 = erow == e2
    onehot = jnp.where(oh1 | oh2, 1.0, 0.0)
    si = lax.broadcasted_iota(jnp.int32, (tm, tm), 0)
    ti = lax.broadcasted_iota(jnp.int32, (tm, tm), 1)
    incl = jnp.where(si <= ti, 1.0, 0.0).astype(BF16)
    csum = jnp.dot(onehot.astype(BF16), incl, preferred_element_type=F32)
    carry = carry_scr[...]
    before = csum - onehot + carry[:, 0:1]
    r1 = jnp.sum(jnp.where(oh1, before, 0.0), axis=0, keepdims=True)
    r2 = jnp.sum(jnp.where(oh2, before, 0.0), axis=0, keepdims=True)
    rank_ref[0:1, :] = r1.astype(jnp.int32)
    rank_ref[1:2, :] = r2.astype(jnp.int32)
    new_carry = carry + csum[:, tm - 1:tm]
    carry_scr[...] = new_carry
    cnt_ref[...] = new_carry.astype(jnp.int32)


def _router(x, g, shift, scale, rw_t, rbias, seq):
    t, d = x.shape
    tm = ROW_TILE
    e = rw_t.shape[0]
    row = lambda i: (0, i)
    return pl.pallas_call(
        _router_kernel,
        out_shape=(jax.ShapeDtypeStruct((t, d // 2), jnp.uint32),
                   jax.ShapeDtypeStruct((TOP_K, t), jnp.int32),
                   jax.ShapeDtypeStruct((TOP_K, t), F32),
                   jax.ShapeDtypeStruct((TOP_K, t), jnp.int32),
                   jax.ShapeDtypeStruct((e, LANES), jnp.int32)),
        grid=(t // tm,),
        in_specs=[
            pl.BlockSpec((tm, d), lambda i: (i, 0)),
            pl.BlockSpec((1, d), lambda i: (0, 0)),
            pl.BlockSpec((1, 1, d), lambda i: (i * tm // seq, 0, 0)),
            pl.BlockSpec((1, 1, d), lambda i: (i * tm // seq, 0, 0)),
            pl.BlockSpec((e, d), lambda i: (0, 0)),
            pl.BlockSpec((e, 1), lambda i: (0, 0)),
        ],
        out_specs=(pl.BlockSpec((tm, d // 2), lambda i: (i, 0)),
                   pl.BlockSpec((TOP_K, tm), row),
                   pl.BlockSpec((TOP_K, tm), row),
                   pl.BlockSpec((TOP_K, tm), row),
                   pl.BlockSpec((e, LANES), lambda i: (0, 0))),
        scratch_shapes=[pltpu.VMEM((e, LANES), F32)],
        compiler_params=_params(("arbitrary",)),
        name="router",
    )(x, g.reshape(1, d), shift, scale, rw_t, rbias.reshape(e, 1))


def _invert_kernel(pos_ref, fill_ref, src_ref, *, n_tok):
    def clear(p, carry):
        src_ref[p] = 0
        return carry

    for e in range(N_EXPERTS + 1):
        lax.fori_loop(fill_ref[2 * e], fill_ref[2 * e + 1], clear, 0)

    for slot in range(TOP_K):
        def place(tok, carry, slot=slot):
            src_ref[pos_ref[slot * n_tok + tok]] = tok
            return carry

        lax.fori_loop(0, n_tok, place, 0, unroll=8)


def _invert_positions(pos_flat, fill, n_rows, n_tok):
    assert pos_flat.shape[0] == TOP_K * n_tok
    return pl.pallas_call(
        functools.partial(_invert_kernel, n_tok=n_tok),
        out_shape=jax.ShapeDtypeStruct((n_rows,), jnp.int32),
        in_specs=[pl.BlockSpec(memory_space=pltpu.SMEM), pl.BlockSpec(memory_space=pltpu.SMEM)],
        out_specs=pl.BlockSpec(memory_space=pltpu.SMEM),
        name="invert_positions",
    )(pos_flat, fill)


def _expert_kernel(te_ref, tv_ref, tf_ref, tn_ref, src_ref, h_hbm, wg_hbm, wu_hbm, wd_hbm, y_ref,
                   xbuf0, xbuf1, sg, su, sd, bg, bu, bd, gsem, wsem, *, layer):
    i = pl.program_id(0)
    xbufs = (xbuf0, xbuf1)
    tm = xbuf0.shape[0]

    def gather(tile, buf):
        for r in range(tm):
            tok = src_ref[tile * tm + r]
            pltpu.make_async_copy(h_hbm.at[pl.ds(tok, 1)], xbufs[buf].at[pl.ds(r, 1)], gsem.at[buf]).start()

    def fetch(e):
        return (pltpu.make_async_copy(wg_hbm.at[layer, e], sg, wsem.at[0]),
                pltpu.make_async_copy(wu_hbm.at[layer, e], su, wsem.at[1]),
                pltpu.make_async_copy(wd_hbm.at[layer, e], sd, wsem.at[2]))

    @pl.when(i == 0)
    def _():
        for c in fetch(te_ref[0]):
            c.start(priority=WEIGHT_DMA_PRIORITY)
        gather(0, 0)

    pending = (i == 0) | (tv_ref[jnp.maximum(i - 1, 0)] > 0)
    for par in range(2):
        @pl.when(pending & (i % 2 == par))
        def _(par=par):
            pltpu.make_async_copy(h_hbm.at[pl.ds(0, tm)], xbufs[par], gsem.at[par]).wait()

    @pl.when(tf_ref[i] > 0)
    def _():
        for c in fetch(te_ref[i]):
            c.wait()
        bg[...] = sg[...].astype(BF16)
        bu[...] = su[...].astype(BF16)
        bd[...] = sd[...].astype(BF16)

        @pl.when(tn_ref[i] >= 0)
        def _():
            for c in fetch(tn_ref[i]):
                c.start(priority=WEIGHT_DMA_PRIORITY)

    for par in range(2):
        @pl.when((tv_ref[i] > 0) & (i % 2 == par))
        def _(par=par):
            gather(i + 1, 1 - par)
            x_lo, x_hi = (v.astype(BF16) for v in _unpack_halves(xbufs[par][...]))
            half = x_lo.shape[1]
            a = (jnp.dot(x_lo, bg[:half, :], preferred_element_type=F32)
                 + jnp.dot(x_hi, bg[half:, :], preferred_element_type=F32))
            u = (jnp.dot(x_lo, bu[:half, :], preferred_element_type=F32)
                 + jnp.dot(x_hi, bu[half:, :], preferred_element_type=F32))
            hid = (a * jax.nn.sigmoid(a) * u).astype(BF16)
            y_ref[...] = _pack_halves(jnp.dot(hid, bd[...], preferred_element_type=F32))

    @pl.when(tv_ref[i] == 0)
    def _():
        y_ref[...] = jnp.zeros_like(y_ref)


def _experts(tile_expert, tile_valid, tile_first, tile_next, src, h, w_gate, w_up, w_down, layer, n_rows):
    dp = h.shape[1]
    d = 2 * dp
    f = w_gate.shape[3]
    tm = EXPERT_TILE
    grid_spec = pltpu.PrefetchScalarGridSpec(
        num_scalar_prefetch=5,
        grid=(n_rows // tm,),
        in_specs=[pl.BlockSpec(memory_space=pl.ANY)] * 4,
        out_specs=pl.BlockSpec((tm, dp), lambda i, *_: (i, 0)),
        scratch_shapes=[pltpu.VMEM((tm, dp), jnp.uint32), pltpu.VMEM((tm, dp), jnp.uint32),
                        pltpu.VMEM((d, f), F32), pltpu.VMEM((d, f), F32), pltpu.VMEM((f, d), F32),
                        pltpu.VMEM((d, f), BF16), pltpu.VMEM((d, f), BF16), pltpu.VMEM((f, d), BF16),
                        pltpu.SemaphoreType.DMA((2,)), pltpu.SemaphoreType.DMA((3,))],
    )
    return pl.pallas_call(
        functools.partial(_expert_kernel, layer=layer),
        out_shape=jax.ShapeDtypeStruct((n_rows, dp), jnp.uint32),
        grid_spec=grid_spec,
        compiler_params=_params(("arbitrary",)),
        name="experts",
    )(tile_expert, tile_valid, tile_first, tile_next, src, h, w_gate, w_up, w_down)


def _combine_kernel(pos_ref, x_ref, w_ref, g_ref, y_ref, o_ref, buf0, buf1, sem):
    i = pl.program_id(0)
    n = pl.num_programs(0)
    tm = x_ref.shape[0]
    t = n * tm
    bufs = (buf0, buf1)

    def gather(tile, par):
        for r in range(tm):
            for slot in range(TOP_K):
                p = pos_ref[slot * t + tile * tm + r]
                pltpu.make_async_copy(y_ref.at[pl.ds(p, 1)], bufs[par].at[slot, pl.ds(r, 1)],
                                      sem.at[par]).start(priority=slot)

    def combine(par):
        w = w_ref[...]
        lo0, hi0 = _unpack_halves(bufs[par][0])
        lo1, hi1 = _unpack_halves(bufs[par][1])
        half = lo0.shape[1]
        g = g_ref[0]
        o_ref[:, :half] = x_ref[:, :half] + g[:, :half] * (w[:, 0:1] * lo0 + w[:, 1:2] * lo1)
        o_ref[:, half:] = x_ref[:, half:] + g[:, half:] * (w[:, 0:1] * hi0 + w[:, 1:2] * hi1)

    @pl.when(i == 0)
    def _():
        gather(0, 0)

    for par in range(2):
        @pl.when(i % 2 == par)
        def _(par=par):
            for slot in range(TOP_K):
                pltpu.make_async_copy(y_ref.at[pl.ds(0, tm)], bufs[par].at[slot], sem.at[par]).wait()

        @pl.when((i % 2 == par) & (i + 1 < n))
        def _(par=par):
            gather(i + 1, 1 - par)
            combine(par)

        @pl.when((i % 2 == par) & (i + 1 == n))
        def _(par=par):
            combine(par)


def _combine(pos_flat, x, wsel_t, gate, y, seq):
    t, d = x.shape
    tm = ROW_TILE
    assert seq % tm == 0 and t % tm == 0
    grid_spec = pltpu.PrefetchScalarGridSpec(
        num_scalar_prefetch=1,
        grid=(t // tm,),
        in_specs=[
            pl.BlockSpec((tm, d), lambda i, p: (i, 0)),
            pl.BlockSpec((tm, TOP_K), lambda i, p: (i, 0)),
            pl.BlockSpec((1, 1, d), lambda i, p: (i * tm // seq, 0, 0)),
            pl.BlockSpec(memory_space=pl.ANY),
        ],
        out_specs=pl.BlockSpec((tm, d), lambda i, p: (i, 0)),
        scratch_shapes=[pltpu.VMEM((TOP_K, tm, d // 2), jnp.uint32), pltpu.VMEM((TOP_K, tm, d // 2), jnp.uint32),
                        pltpu.SemaphoreType.DMA((2,))],
    )
    return pl.pallas_call(
        _combine_kernel,
        out_shape=jax.ShapeDtypeStruct((t, d), F32),
        grid_spec=grid_spec,
        compiler_params=_params(("arbitrary",)),
        name="combine",
    )(pos_flat, x, wsel_t, gate, y)


def _moe(x, g, shift, scale, gate, rw_t, rbias, w_gate, w_up, w_down, layer, seq):
    t, d = x.shape
    te = EXPERT_TILE
    n_rows = TOP_K * t + N_EXPERTS * te
    n_tiles = n_rows // te
    h, eidx, wsel, rank, cnt = _router(x, g, shift, scale, rw_t, rbias, seq)
    counts = cnt[:, 0]
    padded = ((counts + te - 1) // te) * te
    ends = jnp.cumsum(padded)
    starts = ends - padded
    pos = rank
    for e in range(N_EXPERTS):
        pos = pos + jnp.where(eidx == e, starts[e], 0)
    pos_flat = pos.reshape(-1)
    tile_start = jnp.arange(n_tiles, dtype=jnp.int32) * te
    tile_expert = jnp.minimum(jnp.sum(tile_start[:, None] >= ends[None, :], axis=1), N_EXPERTS - 1).astype(jnp.int32)
    tile_valid = (tile_start < ends[-1]).astype(jnp.int32)
    tile_expert = jnp.where(tile_valid > 0, tile_expert, tile_expert[jnp.maximum(ends[-1] // te - 1, 0)])
    prev_expert = jnp.concatenate([jnp.full((1,), -1, jnp.int32), tile_expert[:-1]])
    tile_first = ((tile_valid > 0) & (tile_expert != prev_expert)).astype(jnp.int32)
    later = (tile_expert[None, :] > tile_expert[:, None]) & (tile_valid[None, :] > 0)
    tile_next = jnp.min(jnp.where(later, tile_expert[None, :], N_EXPERTS), axis=1)
    tile_next = jnp.where(tile_next < N_EXPERTS, tile_next, -1).astype(jnp.int32)
    fill = jnp.stack([jnp.concatenate([starts + counts, ends[-1:]]),
                      jnp.concatenate([ends, jnp.full((1,), n_rows, ends.dtype)])], axis=1).reshape(-1).astype(jnp.int32)
    src = _invert_positions(pos_flat, fill, n_rows, t)
    y = _experts(tile_expert, tile_valid, tile_first, tile_next, src, h, w_gate, w_up, w_down, layer, n_rows)
    return _combine(pos_flat, x, wsel.T, gate, y, seq)


def kernel(x, c, positions, ada_w, ada_b, norm1_g, norm2_g, ab_w_in, q_norm_g, k_norm_g, sinks, gm_vnorm_g, gm_ws, gm_b, ab_w_out, c_w_in, c_lower_bounds, c_onorm_g, c_w_out, router_w, router_bias, moe_w_gate, moe_w_up, moe_w_down):
    bsz, seq, d = x.shape
    depth = ada_w.shape[0]
    t = bsz * seq
    xt = x.reshape(t, d)

    c_pad = jnp.concatenate([c, jnp.zeros((8 - bsz % 8, d), F32)], axis=0) if bsz % 8 else c
    mod = _ada_mod(c_pad, ada_w, ada_b)[:, :bsz]
    mod = mod.reshape(depth, bsz, 6, 1, d)
    cos_t, sin_t = _rope_tables(positions)

    sm = jax.nn.softmax(c_lower_bounds.astype(F32), axis=0)
    lower_bounds = jnp.cumsum(sm, axis=0) - sm[0:1]
    rw_t = router_w.T
    ab_in, ab_out, c_in, c_out = (w.astype(BF16) for w in (ab_w_in, ab_w_out, c_w_in, c_w_out))

    for l in range(depth):
        sh1, sc1, g1, sh2, sc2, g2 = (mod[l, :, k] for k in range(6))
        if l % 2 == 0:
            i = l // 2
            proj = _in_proj(xt, norm1_g[l], sh1, sc1, ab_in, i, seq)
            mix = _mix_even(proj, cos_t, sin_t, q_norm_g[i], k_norm_g[i], sinks[i],
                            gm_vnorm_g[i], gm_ws[i], gm_b[i], bsz, seq)
            xt = _out_proj(mix, ab_out, i, xt, g1, seq)
        else:
            j = l // 2
            proj = _in_proj(xt, norm1_g[l], sh1, sc1, c_in, j, seq)
            o = _hgrn(proj, lower_bounds[l], c_onorm_g[j], bsz, seq)
            xt = _out_proj(o, c_out, j, xt, g1, seq)
        xt = _moe(xt, norm2_g[l], sh2, sc2, g2, rw_t, router_bias,
                  moe_w_gate, moe_w_up, moe_w_down, l, seq)
    return xt.reshape(bsz, seq, d)
```

```python
import functools

import jax
import jax.numpy as jnp
from jax import lax
from jax.experimental import pallas as pl
from jax.experimental.pallas import tpu as pltpu

F32 = jnp.float32
BF16 = jnp.bfloat16

EPS = 1e-6
HEAD_DIM = 64
KV_REP = 4
WINDOW = 128
ROPE_THETA = 10000.0
B_GROUP_DIM = 128
B_CHUNK = 128
C_KDIM = 128
C_VDIM = 128
C_CHUNK = 64
N_GROUPS = 4
EXPERTS_PER_GROUP = 4
N_EXPERTS = N_GROUPS * EXPERTS_PER_GROUP
TOP_K = 2

LANES = 128
SUBLANES = 8
VMEM_LIMIT = 56 * 1024 * 1024

ROW_TILE = 512
PROJ_ROW_TILE = 1024
PROJ_COL_TILE = 2048
EXPERT_TILE = 256
WEIGHT_DMA_PRIORITY = 1
HGRN_BLOCK = 16
HGRN_MAX_BLOCK_DECAY = 100.0


def _nt_dot(a, b):
    return lax.dot_general(a, b, (((1,), (1,)), ((), ())), preferred_element_type=F32)


def _pack_halves(x):
    n = x.shape[1] // 2
    return pltpu.pack_elementwise([x[:, :n], x[:, n:]], packed_dtype=BF16)


def _unpack_halves(w):
    return tuple(pltpu.unpack_elementwise(w, index=k, packed_dtype=BF16, unpacked_dtype=F32) for k in range(2))


def _params(sem):
    return pltpu.CompilerParams(dimension_semantics=sem, vmem_limit_bytes=VMEM_LIMIT)


def _ada_kernel(c_ref, w_ref, b_ref, o_ref):
    c = c_ref[...]
    cond = c * jax.nn.sigmoid(c)
    o_ref[0] = jnp.dot(cond, w_ref[0], preferred_element_type=F32) + b_ref[0]


def _ada_mod(c_pad, ada_w, ada_b):
    depth, d, n = ada_w.shape
    tn = 1536
    rows = c_pad.shape[0]
    return pl.pallas_call(
        _ada_kernel,
        out_shape=jax.ShapeDtypeStruct((depth, rows, n), F32),
        grid=(depth, n // tn),
        in_specs=[
            pl.BlockSpec((rows, d), lambda l, j: (0, 0)),
            pl.BlockSpec((1, d, tn), lambda l, j: (l, 0, j)),
            pl.BlockSpec((1, 1, tn), lambda l, j: (l, 0, j)),
        ],
        out_specs=pl.BlockSpec((1, rows, tn), lambda l, j: (l, 0, j)),
        compiler_params=_params(("arbitrary", "arbitrary")),
        name="ada_mod",
    )(c_pad, ada_w, ada_b.reshape(depth, 1, n))


def _rope_table_kernel(pos_ref, freq_ref, sign_ref, cos_ref, sin_ref):
    ang = pos_ref[...].astype(F32) * freq_ref[...]
    cos_ref[...] = jnp.cos(ang)
    sin_ref[...] = jnp.sin(ang) * sign_ref[...]


def _rope_tables(positions):
    t = positions.size
    half = HEAD_DIM // 2
    inv_freq = ROPE_THETA ** (-jnp.arange(half, dtype=F32) / half)
    freq = jnp.tile(inv_freq, LANES // half).reshape(1, LANES)
    sign = jnp.tile(jnp.concatenate([-jnp.ones((half,), F32), jnp.ones((half,), F32)]),
                    LANES // HEAD_DIM).reshape(1, LANES)
    tm = min(1024, t)
    return pl.pallas_call(
        _rope_table_kernel,
        out_shape=(jax.ShapeDtypeStruct((t, LANES), F32), jax.ShapeDtypeStruct((t, LANES), F32)),
        grid=(t // tm,),
        in_specs=[
            pl.BlockSpec((tm, 1), lambda i: (i, 0)),
            pl.BlockSpec((1, LANES), lambda i: (0, 0)),
            pl.BlockSpec((1, LANES), lambda i: (0, 0)),
        ],
        out_specs=(pl.BlockSpec((tm, LANES), lambda i: (i, 0)),
                   pl.BlockSpec((tm, LANES), lambda i: (i, 0))),
        compiler_params=_params(("arbitrary",)),
        name="rope_tables",
    )(positions.reshape(t, 1), freq, sign)


def _norm_mod(x, g, shift, scale):
    y = x * lax.rsqrt(jnp.mean(x * x, axis=-1, keepdims=True) + EPS)
    return (y * g) * (1 + scale) + shift


def _in_proj_kernel(x_ref, g_ref, sh_ref, sc_ref, w_ref, o_ref, h_scr):
    @pl.when(pl.program_id(1) == 0)
    def _():
        h_scr[...] = _norm_mod(x_ref[...], g_ref[...], sh_ref[0], sc_ref[0]).astype(BF16)

    o_ref[...] = jnp.dot(h_scr[...], w_ref[0], preferred_element_type=F32).astype(o_ref.dtype)


def _matmul_kernel(h_ref, w_ref, o_ref):
    o_ref[...] = jnp.dot(h_ref[...], w_ref[0], preferred_element_type=F32).astype(o_ref.dtype)


def _in_proj_normed(h, w, layer):
    t, d = h.shape
    n = w.shape[2]
    tm = PROJ_ROW_TILE
    tn = max(c for c in range(LANES, PROJ_COL_TILE + 1, LANES) if n % c == 0)
    assert t % tm == 0
    return pl.pallas_call(
        _matmul_kernel,
        out_shape=jax.ShapeDtypeStruct((t, n), BF16),
        grid=(t // tm, n // tn),
        in_specs=[
            pl.BlockSpec((tm, d), lambda i, j: (i, 0)),
            pl.BlockSpec((1, d, tn), lambda i, j: (layer, 0, j)),
        ],
        out_specs=pl.BlockSpec((tm, tn), lambda i, j: (i, j)),
        compiler_params=_params(("arbitrary", "arbitrary")),
        name="in_proj",
    )(h, w)


def _in_proj(x, g, shift, scale, w, layer, seq):
    t, d = x.shape
    n = w.shape[2]
    tm = PROJ_ROW_TILE
    tn = max(c for c in range(LANES, PROJ_COL_TILE + 1, LANES) if n % c == 0)
    assert seq % tm == 0 and t % tm == 0
    return pl.pallas_call(
        _in_proj_kernel,
        out_shape=jax.ShapeDtypeStruct((t, n), BF16),
        grid=(t // tm, n // tn),
        in_specs=[
            pl.BlockSpec((tm, d), lambda i, j: (i, 0)),
            pl.BlockSpec((1, d), lambda i, j: (0, 0)),
            pl.BlockSpec((1, 1, d), lambda i, j: (i * tm // seq, 0, 0)),
            pl.BlockSpec((1, 1, d), lambda i, j: (i * tm // seq, 0, 0)),
            pl.BlockSpec((1, d, tn), lambda i, j: (layer, 0, j)),
        ],
        out_specs=pl.BlockSpec((tm, tn), lambda i, j: (i, j)),
        scratch_shapes=[pltpu.VMEM((tm, d), BF16)],
        compiler_params=_params(("arbitrary", "arbitrary")),
        name="in_proj",
    )(x, g.reshape(1, d), shift, scale, w)


def _out_proj_kernel(a_ref, w_ref, x_ref, g_ref, o_ref):
    y = jnp.dot(a_ref[...], w_ref[0], preferred_element_type=F32)
    o_ref[...] = x_ref[...] + g_ref[0] * y


def _out_proj(a, w, layer, x, gate, seq):
    t, k = a.shape
    n = w.shape[2]
    tm, tn = PROJ_ROW_TILE, 1024
    assert seq % tm == 0 and t % tm == 0 and n % tn == 0
    return pl.pallas_call(
        _out_proj_kernel,
        out_shape=jax.ShapeDtypeStruct((t, n), F32),
        grid=(t // tm, n // tn),
        in_specs=[
            pl.BlockSpec((tm, k), lambda i, j: (i, 0)),
            pl.BlockSpec((1, k, tn), lambda i, j: (layer, 0, j)),
            pl.BlockSpec((tm, tn), lambda i, j: (i, j)),
            pl.BlockSpec((1, 1, tn), lambda i, j: (i * tm // seq, 0, j)),
        ],
        out_specs=pl.BlockSpec((tm, tn), lambda i, j: (i, j)),
        compiler_params=_params(("arbitrary", "arbitrary")),
        name="out_proj",
    )(a, w, x, gate)


def _segment_mean_matrix():
    r = lax.broadcasted_iota(jnp.int32, (LANES, LANES), 0) // HEAD_DIM
    c = lax.broadcasted_iota(jnp.int32, (LANES, LANES), 1) // HEAD_DIM
    return jnp.where(r == c, 1.0 / HEAD_DIM, 0.0).astype(F32)


def _head_norm_rope(x, gain, cos, sin, seg_mean, low_half):
    ms = jnp.dot(x * x, seg_mean, preferred_element_type=F32)
    y = x * lax.rsqrt(ms + EPS) * gain
    partner = jnp.where(low_half, pltpu.roll(y, LANES - HEAD_DIM // 2, 1),
                        pltpu.roll(y, HEAD_DIM // 2, 1))
    return y * cos + partner * sin


def _gelu_tanh(x):
    return 0.5 * x * (1.0 + jnp.tanh(0.7978845608028654 * (x + 0.044715 * (x * x * x))))


def _mix_kernel(sinks_ref,
                q0_ref, q1_ref, kv_ref, u0_ref, u1_ref, v0_ref, v1_ref,
                cos_ref, sin_ref, qg_ref, kg_ref, vng_ref, ws_ref, bs_ref,
                o_ref, kd_scr, vd_scr):
    n = pl.program_id(1)
    w = WINDOW
    n_kv = kd_scr.shape[0]

    @pl.when(n == 0)
    def _():
        kd_scr[...] = jnp.zeros_like(kd_scr)
        vd_scr[...] = jnp.zeros_like(vd_scr)

    lane = lax.broadcasted_iota(jnp.int32, (w, LANES), 1)
    low_half = (lane % HEAD_DIM) < (HEAD_DIM // 2)
    first_head = lane < HEAD_DIM
    seg_mean = _segment_mean_matrix()
    cos = cos_ref[...]
    sin = sin_ref[...]

    kv = kv_ref[...].astype(F32)
    for c in range(n_kv // 2):
        kc = _head_norm_rope(kv[:, c * LANES:(c + 1) * LANES], kg_ref[...], cos, sin, seg_mean, low_half)
        vc = kv[:, (n_kv // 2 + c) * LANES:(n_kv // 2 + c + 1) * LANES]
        kc_sw = pltpu.roll(kc, HEAD_DIM, 1)
        vc_sw = pltpu.roll(vc, HEAD_DIM, 1)
        kd_scr[2 * c, w:, :] = jnp.where(first_head, kc, kc_sw).astype(BF16)
        kd_scr[2 * c + 1, w:, :] = jnp.where(first_head, kc_sw, kc).astype(BF16)
        vd_scr[2 * c, w:, :] = jnp.where(first_head, vc, vc_sw).astype(BF16)
        vd_scr[2 * c + 1, w:, :] = jnp.where(first_head, vc_sw, vc).astype(BF16)

    qi = lax.broadcasted_iota(jnp.int32, (w, 2 * w), 0)
    kj = lax.broadcasted_iota(jnp.int32, (w, 2 * w), 1)
    diff = qi + w - kj
    band = (diff >= 0) & (diff < w) & ((kj >= w) | (n > 0))
    band4 = jnp.concatenate([band] * KV_REP, axis=0)

    scale = HEAD_DIM ** -0.5
    ones_kv = jnp.ones((2 * w, LANES), BF16)
    q_refs = (q0_ref, q1_ref)
    chunks_per_ref = q0_ref.shape[1] // LANES
    for g in range(n_kv):
        rows = []
        for cc in range(2):
            c = 2 * g + cc
            qc = q_refs[c // chunks_per_ref][:, (c % chunks_per_ref) * LANES:(c % chunks_per_ref + 1) * LANES]
            qc = _head_norm_rope(qc.astype(F32), qg_ref[...], cos, sin, seg_mean, low_half)
            rows.append(jnp.where(first_head, qc, 0.0))
            rows.append(jnp.where(first_head, 0.0, qc))
        qm = jnp.concatenate(rows, axis=0).astype(BF16)
        s = lax.dot_general(qm, kd_scr[g], (((1,), (1,)), ((), ())),
                            preferred_element_type=F32) * scale
        s = jnp.where(band4, s, -jnp.inf)
        sink = jnp.concatenate(
            [jnp.full((w, LANES), sinks_ref[KV_REP * g + r], F32) for r in range(KV_REP)], axis=0)
        m = jnp.maximum(jnp.broadcast_to(jnp.max(s, axis=-1, keepdims=True), (KV_REP * w, LANES)), sink)
        p = jnp.exp(s - jnp.concatenate([m, m], axis=1)).astype(BF16)
        total = jnp.dot(p, ones_kv, preferred_element_type=F32) + jnp.exp(sink - m)
        o = jnp.dot(p, vd_scr[g], preferred_element_type=F32) / total
        for cc in range(2):
            oc = jnp.where(first_head, o[(2 * cc) * w:(2 * cc + 1) * w], o[(2 * cc + 1) * w:(2 * cc + 2) * w])
            c = 2 * g + cc
            o_ref[:, c * LANES:(c + 1) * LANES] = oc.astype(o_ref.dtype)

    kd_scr[:, :w, :] = kd_scr[:, w:, :]
    vd_scr[:, :w, :] = vd_scr[:, w:, :]

    a_q = 2 * q0_ref.shape[1]
    u = _gelu_tanh(jnp.concatenate([u0_ref[...], u1_ref[...]], axis=1).astype(F32))
    v = _gelu_tanh(jnp.concatenate([v0_ref[...], v1_ref[...]], axis=1).astype(F32))
    mu = jnp.mean(v, axis=-1, keepdims=True)
    vc = v - mu
    vn = (vc * lax.rsqrt(jnp.mean(vc * vc, axis=-1, keepdims=True) + EPS) * vng_ref[...]).astype(BF16)
    ti = lax.broadcasted_iota(jnp.int32, (w, w), 0)
    si = lax.broadcasted_iota(jnp.int32, (w, w), 1)
    causal = si <= ti
    bs = bs_ref[...]
    for g in range(ws_ref.shape[0]):
        wg = jnp.where(causal, ws_ref[g], 0.0).astype(BF16)
        sv = jnp.dot(wg, vn[:, g * LANES:(g + 1) * LANES], preferred_element_type=F32) + bs[:, g:g + 1]
        o_ref[:, a_q + g * LANES:a_q + (g + 1) * LANES] = (u[:, g * LANES:(g + 1) * LANES] * sv).astype(o_ref.dtype)


def _mix_even(proj, cos_t, sin_t, q_g, k_g, sinks, vnorm_g, ws, bs, bsz, seq):
    t = proj.shape[0]
    w = WINDOW
    nb = seq // w
    heads = sinks.shape[0]
    n_kv = heads // KV_REP
    a_q = heads * HEAD_DIM
    a_kv = n_kv * HEAD_DIM
    b_w = vnorm_g.shape[0]
    cw = 512
    assert a_q == 2 * cw and 2 * a_kv == cw and b_w == 2 * cw
    row = lambda b, n: b * nb + n
    gain2 = lambda g: jnp.tile(g, LANES // HEAD_DIM).reshape(1, LANES)
    col = lambda j: pl.BlockSpec((w, cw), lambda b, n, s: (row(b, n), j))
    grid_spec = pltpu.PrefetchScalarGridSpec(
        num_scalar_prefetch=1,
        grid=(bsz, nb),
        in_specs=[
            col(0), col(1), col(2), col(3), col(4), col(5), col(6),
            pl.BlockSpec((w, LANES), lambda b, n, s: (row(b, n), 0)),
            pl.BlockSpec((w, LANES), lambda b, n, s: (row(b, n), 0)),
            pl.BlockSpec((1, LANES), lambda b, n, s: (0, 0)),
            pl.BlockSpec((1, LANES), lambda b, n, s: (0, 0)),
            pl.BlockSpec((1, b_w), lambda b, n, s: (0, 0)),
            pl.BlockSpec(ws.shape, lambda b, n, s: (0, 0, 0)),
            pl.BlockSpec((w, ws.shape[0]), lambda b, n, s: (0, 0)),
        ],
        out_specs=pl.BlockSpec((w, a_q + b_w), lambda b, n, s: (row(b, n), 0)),
        scratch_shapes=[pltpu.VMEM((n_kv, 2 * w, LANES), BF16),
                        pltpu.VMEM((n_kv, 2 * w, LANES), BF16)],
    )
    return pl.pallas_call(
        _mix_kernel,
        out_shape=jax.ShapeDtypeStruct((t, a_q + b_w), BF16),
        grid_spec=grid_spec,
        compiler_params=_params(("arbitrary", "arbitrary")),
        name="mix_even",
    )(sinks, proj, proj, proj, proj, proj, proj, proj, cos_t, sin_t,
      gain2(q_g), gain2(k_g), vnorm_g.reshape(1, b_w), ws, bs.T)


def _hgrn_kernel(q_ref, f_ref, i_ref, gt_ref, lb_ref, og_ref, o_ref,
                 oi_scr, qd_scr, u_scr, d_scr):
    c = C_CHUNK
    nc = q_ref.shape[0] // c
    nv = c // SUBLANES
    lb = lb_ref[...]
    og = og_ref[...]
    t_row = lax.broadcasted_iota(jnp.int32, (c, LANES), 0)
    tt = lax.broadcasted_iota(jnp.int32, (c, c), 0)
    ss = lax.broadcasted_iota(jnp.int32, (c, c), 1)

    def inputs(ci):
        r0 = pl.multiple_of(ci * c, c)
        q = q_ref[pl.ds(r0, c), :].astype(F32)
        fl = f_ref[pl.ds(r0, c), :].astype(F32)
        v = i_ref[pl.ds(r0, c), :]
        f = lb + (1 - lb) * jax.nn.sigmoid(fl)
        k = 1 - f
        b = jnp.log2(f)
        for d in (1, 2, 4, 8, 16, 32):
            b = b + jnp.where(t_row >= d, pltpu.roll(b, d, 0), 0.0)
        return q, f, k, b, v

    def row(b, r, n):
        b3 = b.reshape(nv, SUBLANES, LANES)
        rep = jnp.broadcast_to(b3[r // SUBLANES:r // SUBLANES + 1, r % SUBLANES:r % SUBLANES + 1, :],
                               (n // SUBLANES, SUBLANES, LANES))
        return rep.reshape(n, LANES)

    def half_level(q, k, b, m):
        a_parts, b_parts = [], []
        for blk in range(c // (2 * m)):
            lo, mid, hi = blk * 2 * m, blk * 2 * m + m, (blk + 1) * 2 * m
            zero = jnp.zeros((m, LANES), BF16)
            ref = row(b, mid - 1, m)
            a_parts += [zero, (q[mid:hi] * jnp.exp2(b[mid:hi] - ref)).astype(BF16)]
            b_parts += [(k[lo:mid] * jnp.exp2(ref - b[lo:mid])).astype(BF16), zero]
        return _nt_dot(jnp.concatenate(a_parts, axis=0), jnp.concatenate(b_parts, axis=0))

    def finish(ci, q, k, b, v, scores):
        oi_scr[ci] = jnp.dot(scores.astype(BF16), v, preferred_element_type=F32)
        qd_scr[ci] = (q * jnp.exp2(b)).astype(BF16)
        b_last = b[c - 1:c, :]
        kd = (k * jnp.exp2(b_last - b)).astype(BF16)
        u_scr[ci] = lax.dot_general(v, kd, (((0,), (0,)), ((), ())), preferred_element_type=F32)
        d_scr[ci] = jnp.exp2(b_last)

    def intra_fast(ci, worst):
        q, f, k, b, v = inputs(ci)
        blk = HGRN_BLOCK
        starts = [jnp.zeros((blk, LANES), F32)] + [row(b, s - 1, blk) for s in range(blk, c, blk)]
        base = jnp.concatenate(starts, axis=0)
        s_d = _nt_dot((q * jnp.exp2(b - base)).astype(BF16), (k * jnp.exp2(base - b)).astype(BF16))
        s_16 = half_level(q, k, b, blk)
        s_32 = half_level(q, k, b, 2 * blk)
        scores = jnp.where(((tt // blk) == (ss // blk)) & (ss <= tt), s_d,
                           jnp.where((tt // (2 * blk)) == (ss // (2 * blk)), s_16, s_32))
        finish(ci, q, k, b, v, scores)
        for s in range(0, c, blk):
            worst = jnp.maximum(worst, base[s:s + 1, :] - b[s + blk - 1:s + blk, :])
        return worst

    def intra_safe(ci, carry):
        q, f, k, b, v = inputs(ci)
        scores = jnp.where(tt == ss, jnp.sum(q * k, axis=-1, keepdims=True), 0.0)
        odd = (t_row % 2) == 1
        s_1 = _nt_dot(jnp.where(odd, q * f, 0.0).astype(BF16), jnp.where(odd, 0.0, k).astype(BF16))
        scores = scores + jnp.where((tt // 2) == (ss // 2), s_1, 0.0)
        b3 = b.reshape(nv, SUBLANES, LANES)
        sub = lax.broadcasted_iota(jnp.int32, b3.shape, 1)
        small_refs = {4: jnp.broadcast_to(b3[:, 3:4, :], b3.shape).reshape(c, LANES),
                      2: jnp.where(sub < 4, jnp.broadcast_to(b3[:, 1:2, :], b3.shape),
                                   jnp.broadcast_to(b3[:, 5:6, :], b3.shape)).reshape(c, LANES)}
        for m in (32, 16, 8, 4, 2):
            if m >= SUBLANES:
                s_l = half_level(q, k, b, m)
            else:
                ref = small_refs[m]
                upper = (t_row % (2 * m)) >= m
                s_l = _nt_dot(jnp.where(upper, q * jnp.exp2(b - ref), 0.0).astype(BF16),
                              jnp.where(upper, 0.0, k * jnp.exp2(ref - b)).astype(BF16))
            scores = scores + (s_l if 2 * m == c else jnp.where((tt // (2 * m)) == (ss // (2 * m)), s_l, 0.0))
        finish(ci, q, k, b, v, scores)
        return carry

    worst = lax.fori_loop(0, nc, intra_fast, jnp.zeros((1, LANES), F32), unroll=8)

    @pl.when(jnp.max(worst) > HGRN_MAX_BLOCK_DECAY)
    def _():
        lax.fori_loop(0, nc, intra_safe, 0, unroll=2)

    def inter(ci, state_t):
        r0 = pl.multiple_of(ci * c, c)
        gate = gt_ref[pl.ds(r0, c), :].astype(F32)
        o = oi_scr[ci] + _nt_dot(qd_scr[ci], state_t.astype(BF16))
        y = o * lax.rsqrt(jnp.mean(o * o, axis=-1, keepdims=True) + EPS) * og
        o_ref[pl.ds(r0, c), :] = (y * (gate * jax.nn.sigmoid(gate))).astype(o_ref.dtype)
        return state_t * d_scr[ci] + u_scr[ci]

    lax.fori_loop(0, nc, inter, jnp.zeros((C_VDIM, C_KDIM), F32), unroll=8)


def _hgrn(proj, lb, og, bsz, seq):
    t, n4 = proj.shape
    heads = n4 // (4 * LANES)
    nc = seq // C_CHUNK
    return pl.pallas_call(
        _hgrn_kernel,
        out_shape=jax.ShapeDtypeStruct((t, heads * C_VDIM), BF16),
        grid=(bsz, heads),
        in_specs=[
            pl.BlockSpec((seq, LANES), lambda b, h: (b, h)),
            pl.BlockSpec((seq, LANES), lambda b, h: (b, heads + h)),
            pl.BlockSpec((seq, LANES), lambda b, h: (b, 2 * heads + h)),
            pl.BlockSpec((seq, LANES), lambda b, h: (b, 3 * heads + h)),
            pl.BlockSpec((1, LANES), lambda b, h: (0, h)),
            pl.BlockSpec((1, LANES), lambda b, h: (0, 0)),
        ],
        out_specs=pl.BlockSpec((seq, LANES), lambda b, h: (b, h)),
        scratch_shapes=[pltpu.VMEM((nc, C_CHUNK, C_VDIM), F32),
                        pltpu.VMEM((nc, C_CHUNK, C_KDIM), BF16),
                        pltpu.VMEM((nc, C_VDIM, C_KDIM), F32),
                        pltpu.VMEM((nc, 1, C_KDIM), F32)],
        compiler_params=_params(("arbitrary", "arbitrary")),
        name="hgrn2",
    )(proj, proj, proj, proj, lb.reshape(1, -1), og.reshape(1, LANES))


def _first_max(vals):
    best, idx = vals[0], jnp.zeros(vals[0].shape, jnp.int32)
    for i in range(1, len(vals)):
        better = vals[i] > best
        best = jnp.where(better, vals[i], best)
        idx = jnp.where(better, i, idx)
    return best, idx


def _pick(idx, vals):
    out = vals[-1]
    for i in range(len(vals) - 2, -1, -1):
        out = jnp.where(idx == i, vals[i], out)
    return out


def _router_kernel(x_ref, g_ref, sh_ref, sc_ref, rw_ref, rb_ref,
                   h_ref, eidx_ref, wsel_ref, rank_ref, cnt_ref, carry_scr):
    i = pl.program_id(0)
    tm = x_ref.shape[0]

    @pl.when(i == 0)
    def _():
        carry_scr[...] = jnp.zeros_like(carry_scr)

    h = _norm_mod(x_ref[...], g_ref[...], sh_ref[0], sc_ref[0])
    h_ref[...] = _pack_halves(h)
    rw = rw_ref[...]
    w_hi = rw.astype(BF16)
    w_lo = (rw - w_hi.astype(F32)).astype(BF16)
    h_hi = h.astype(BF16)
    h_lo = (h - h_hi.astype(F32)).astype(BF16)
    head = _nt_dot(jnp.concatenate([w_hi, w_lo], axis=0), h_hi)
    logits = head[:N_EXPERTS] + head[N_EXPERTS:] + _nt_dot(w_hi, h_lo)
    scores = jax.nn.sigmoid(logits)
    sel = scores + rb_ref[...]
    epg = EXPERTS_PER_GROUP
    sel_rows = [sel[e:e + 1, :] for e in range(N_EXPERTS)]
    sc_rows = [scores[e:e + 1, :] for e in range(N_EXPERTS)]
    gsum = []
    for g in range(N_GROUPS):
        r = sel_rows[g * epg:(g + 1) * epg]
        pair = None
        for a in range(epg):
            for b in range(a + 1, epg):
                pair = r[a] + r[b] if pair is None else jnp.maximum(pair, r[a] + r[b])
        gsum.append(pair)
    _, g_idx = _first_max(gsum)
    cand = [_pick(g_idx, [sel_rows[g * epg + j] for g in range(N_GROUPS)]) for j in range(epg)]
    cand_sc = [_pick(g_idx, [sc_rows[g * epg + j] for g in range(N_GROUPS)]) for j in range(epg)]
    _, i1 = _first_max(cand)
    _, i2 = _first_max([jnp.where(i1 == j, -jnp.inf, cand[j]) for j in range(epg)])
    w1 = _pick(i1, cand_sc)
    w2 = _pick(i2, cand_sc)
    tot = w1 + w2
    e1 = g_idx * epg + i1
    e2 = g_idx * epg + i2
    eidx_ref[0:1, :] = e1
    eidx_ref[1:2, :] = e2
    wsel_ref[0:1, :] = w1 / tot
    wsel_ref[1:2, :] = w2 / tot

    erow = lax.broadcasted_iota(jnp.int32, (N_EXPERTS, tm), 0)
    oh1 = erow == e1
    oh2 = erow == e2
    onehot = jnp.where(oh1 | oh2, 1.0, 0.0)
    si = lax.broadcasted_iota(jnp.int32, (tm, tm), 0)
    ti = lax.broadcasted_iota(jnp.int32, (tm, tm), 1)
    incl = jnp.where(si <= ti, 1.0, 0.0).astype(BF16)
    csum = jnp.dot(onehot.astype(BF16), incl, preferred_element_type=F32)
    carry = carry_scr[...]
    before = csum - onehot + carry[:, 0:1]
    r1 = jnp.sum(jnp.where(oh1, before, 0.0), axis=0, keepdims=True)
    r2 = jnp.sum(jnp.where(oh2, before, 0.0), axis=0, keepdims=True)
    rank_ref[0:1, :] = r1.astype(jnp.int32)
    rank_ref[1:2, :] = r2.astype(jnp.int32)
    new_carry = carry + csum[:, tm - 1:tm]
    carry_scr[...] = new_carry
    cnt_ref[...] = new_carry.astype(jnp.int32)


def _router(x, g, shift, scale, rw_t, rbias, seq):
    t, d = x.shape
    tm = ROW_TILE
    e = rw_t.shape[0]
    row = lambda i: (0, i)
    return pl.pallas_call(
        _router_kernel,
        out_shape=(jax.ShapeDtypeStruct((t, d // 2), jnp.uint32),
                   jax.ShapeDtypeStruct((TOP_K, t), jnp.int32),
                   jax.ShapeDtypeStruct((TOP_K, t), F32),
                   jax.ShapeDtypeStruct((TOP_K, t), jnp.int32),
                   jax.ShapeDtypeStruct((e, LANES), jnp.int32)),
        grid=(t // tm,),
        in_specs=[
            pl.BlockSpec((tm, d), lambda i: (i, 0)),
            pl.BlockSpec((1, d), lambda i: (0, 0)),
            pl.BlockSpec((1, 1, d), lambda i: (i * tm // seq, 0, 0)),
            pl.BlockSpec((1, 1, d), lambda i: (i * tm // seq, 0, 0)),
            pl.BlockSpec((e, d), lambda i: (0, 0)),
            pl.BlockSpec((e, 1), lambda i: (0, 0)),
        ],
        out_specs=(pl.BlockSpec((tm, d // 2), lambda i: (i, 0)),
                   pl.BlockSpec((TOP_K, tm), row),
                   pl.BlockSpec((TOP_K, tm), row),
                   pl.BlockSpec((TOP_K, tm), row),
                   pl.BlockSpec((e, LANES), lambda i: (0, 0))),
        scratch_shapes=[pltpu.VMEM((e, LANES), F32)],
        compiler_params=_params(("arbitrary",)),
        name="router",
    )(x, g.reshape(1, d), shift, scale, rw_t, rbias.reshape(e, 1))


def _invert_kernel(pos_ref, fill_ref, src_ref, *, n_tok):
    def clear(p, carry):
        src_ref[p] = 0
        return carry

    for e in range(N_EXPERTS + 1):
        lax.fori_loop(fill_ref[2 * e], fill_ref[2 * e + 1], clear, 0)

    for slot in range(TOP_K):
        def place(tok, carry, slot=slot):
            src_ref[pos_ref[slot * n_tok + tok]] = tok
            return carry

        lax.fori_loop(0, n_tok, place, 0, unroll=8)


def _invert_positions(pos_flat, fill, n_rows, n_tok):
    assert pos_flat.shape[0] == TOP_K * n_tok
    return pl.pallas_call(
        functools.partial(_invert_kernel, n_tok=n_tok),
        out_shape=jax.ShapeDtypeStruct((n_rows,), jnp.int32),
        in_specs=[pl.BlockSpec(memory_space=pltpu.SMEM), pl.BlockSpec(memory_space=pltpu.SMEM)],
        out_specs=pl.BlockSpec(memory_space=pltpu.SMEM),
        name="invert_positions",
    )(pos_flat, fill)


def _expert_kernel(te_ref, tv_ref, tf_ref, tn_ref, src_ref, h_hbm, wg_hbm, wu_hbm, wd_hbm, y_ref,
                   xbuf0, xbuf1, sg, su, sd, bg, bu, bd, gsem, wsem, *, layer):
    i = pl.program_id(0)
    xbufs = (xbuf0, xbuf1)
    tm = xbuf0.shape[0]

    def gather(tile, buf):
        for r in range(tm):
            tok = src_ref[tile * tm + r]
            pltpu.make_async_copy(h_hbm.at[pl.ds(tok, 1)], xbufs[buf].at[pl.ds(r, 1)], gsem.at[buf]).start()

    def fetch(e):
        return (pltpu.make_async_copy(wg_hbm.at[layer, e], sg, wsem.at[0]),
                pltpu.make_async_copy(wu_hbm.at[layer, e], su, wsem.at[1]),
                pltpu.make_async_copy(wd_hbm.at[layer, e], sd, wsem.at[2]))

    @pl.when(i == 0)
    def _():
        for c in fetch(te_ref[0]):
            c.start(priority=WEIGHT_DMA_PRIORITY)
        gather(0, 0)

    pending = (i == 0) | (tv_ref[jnp.maximum(i - 1, 0)] > 0)
    for par in range(2):
        @pl.when(pending & (i % 2 == par))
        def _(par=par):
            pltpu.make_async_copy(h_hbm.at[pl.ds(0, tm)], xbufs[par], gsem.at[par]).wait()

    @pl.when(tf_ref[i] > 0)
    def _():
        for c in fetch(te_ref[i]):
            c.wait()
        bg[...] = sg[...].astype(BF16)
        bu[...] = su[...].astype(BF16)
        bd[...] = sd[...].astype(BF16)

        @pl.when(tn_ref[i] >= 0)
        def _():
            for c in fetch(tn_ref[i]):
                c.start(priority=WEIGHT_DMA_PRIORITY)

    for par in range(2):
        @pl.when((tv_ref[i] > 0) & (i % 2 == par))
        def _(par=par):
            gather(i + 1, 1 - par)
            x_lo, x_hi = (v.astype(BF16) for v in _unpack_halves(xbufs[par][...]))
            half = x_lo.shape[1]
            a = (jnp.dot(x_lo, bg[:half, :], preferred_element_type=F32)
                 + jnp.dot(x_hi, bg[half:, :], preferred_element_type=F32))
            u = (jnp.dot(x_lo, bu[:half, :], preferred_element_type=F32)
                 + jnp.dot(x_hi, bu[half:, :], preferred_element_type=F32))
            hid = (a * jax.nn.sigmoid(a) * u).astype(BF16)
            y_ref[...] = _pack_halves(jnp.dot(hid, bd[...], preferred_element_type=F32))

    @pl.when(tv_ref[i] == 0)
    def _():
        y_ref[...] = jnp.zeros_like(y_ref)


def _experts(tile_expert, tile_valid, tile_first, tile_next, src, h, w_gate, w_up, w_down, layer, n_rows):
    dp = h.shape[1]
    d = 2 * dp
    f = w_gate.shape[3]
    tm = EXPERT_TILE
    grid_spec = pltpu.PrefetchScalarGridSpec(
        num_scalar_prefetch=5,
        grid=(n_rows // tm,),
        in_specs=[pl.BlockSpec(memory_space=pl.ANY)] * 4,
        out_specs=pl.BlockSpec((tm, dp), lambda i, *_: (i, 0)),
        scratch_shapes=[pltpu.VMEM((tm, dp), jnp.uint32), pltpu.VMEM((tm, dp), jnp.uint32),
                        pltpu.VMEM((d, f), F32), pltpu.VMEM((d, f), F32), pltpu.VMEM((f, d), F32),
                        pltpu.VMEM((d, f), BF16), pltpu.VMEM((d, f), BF16), pltpu.VMEM((f, d), BF16),
                        pltpu.SemaphoreType.DMA((2,)), pltpu.SemaphoreType.DMA((3,))],
    )
    return pl.pallas_call(
        functools.partial(_expert_kernel, layer=layer),
        out_shape=jax.ShapeDtypeStruct((n_rows, dp), jnp.uint32),
        grid_spec=grid_spec,
        compiler_params=_params(("arbitrary",)),
        name="experts",
    )(tile_expert, tile_valid, tile_first, tile_next, src, h, w_gate, w_up, w_down)


def _combine_kernel(pos_ref, x_ref, w_ref, g_ref, y_ref, *rest, with_next):
    if with_next:
        ng_ref, nsh_ref, nsc_ref, o_ref, hn_ref, buf0, buf1, sem = rest
    else:
        o_ref, buf0, buf1, sem = rest
    i = pl.program_id(0)
    n = pl.num_programs(0)
    tm = x_ref.shape[0]
    t = n * tm
    bufs = (buf0, buf1)

    def gather(tile, par):
        for r in range(tm):
            for slot in range(TOP_K):
                p = pos_ref[slot * t + tile * tm + r]
                pltpu.make_async_copy(y_ref.at[pl.ds(p, 1)], bufs[par].at[slot, pl.ds(r, 1)],
                                      sem.at[par]).start(priority=slot)

    def combine(par):
        w = w_ref[...]
        lo0, hi0 = _unpack_halves(bufs[par][0])
        lo1, hi1 = _unpack_halves(bufs[par][1])
        half = lo0.shape[1]
        g = g_ref[0]
        x_lo = x_ref[:, :half] + g[:, :half] * (w[:, 0:1] * lo0 + w[:, 1:2] * lo1)
        x_hi = x_ref[:, half:] + g[:, half:] * (w[:, 0:1] * hi0 + w[:, 1:2] * hi1)
        o_ref[:, :half] = x_lo
        o_ref[:, half:] = x_hi
        if with_next:
            x_new = jnp.concatenate([x_lo, x_hi], axis=1)
            hn_ref[...] = _norm_mod(x_new, ng_ref[...], nsh_ref[0], nsc_ref[0]).astype(BF16)

    @pl.when(i == 0)
    def _():
        gather(0, 0)

    for par in range(2):
        @pl.when(i % 2 == par)
        def _(par=par):
            for slot in range(TOP_K):
                pltpu.make_async_copy(y_ref.at[pl.ds(0, tm)], bufs[par].at[slot], sem.at[par]).wait()

        @pl.when((i % 2 == par) & (i + 1 < n))
        def _(par=par):
            gather(i + 1, 1 - par)
            combine(par)

        @pl.when((i % 2 == par) & (i + 1 == n))
        def _(par=par):
            combine(par)


def _combine(pos_flat, x, wsel_t, gate, y, seq, next_norm=None):
    t, d = x.shape
    tm = ROW_TILE
    assert seq % tm == 0 and t % tm == 0
    row = pl.BlockSpec((tm, d), lambda i, p: (i, 0))
    per_batch = pl.BlockSpec((1, 1, d), lambda i, p: (i * tm // seq, 0, 0))
    in_specs = [row, pl.BlockSpec((tm, TOP_K), lambda i, p: (i, 0)), per_batch, pl.BlockSpec(memory_space=pl.ANY)]
    operands = [pos_flat, x, wsel_t, gate, y]
    out_shape = jax.ShapeDtypeStruct((t, d), F32)
    out_specs = row
    if next_norm is not None:
        in_specs += [pl.BlockSpec((1, d), lambda i, p: (0, 0)), per_batch, per_batch]
        operands += [next_norm[0].reshape(1, d), next_norm[1], next_norm[2]]
        out_shape = (out_shape, jax.ShapeDtypeStruct((t, d), BF16))
        out_specs = (row, row)
    grid_spec = pltpu.PrefetchScalarGridSpec(
        num_scalar_prefetch=1,
        grid=(t // tm,),
        in_specs=in_specs,
        out_specs=out_specs,
        scratch_shapes=[pltpu.VMEM((TOP_K, tm, d // 2), jnp.uint32), pltpu.VMEM((TOP_K, tm, d // 2), jnp.uint32),
                        pltpu.SemaphoreType.DMA((2,))],
    )
    return pl.pallas_call(
        functools.partial(_combine_kernel, with_next=next_norm is not None),
        out_shape=out_shape,
        grid_spec=grid_spec,
        compiler_params=_params(("arbitrary",)),
        name="combine",
    )(*operands)


def _moe(x, g, shift, scale, gate, rw_t, rbias, w_gate, w_up, w_down, layer, seq, next_norm):
    t, d = x.shape
    te = EXPERT_TILE
    n_rows = TOP_K * t + N_EXPERTS * te
    n_tiles = n_rows // te
    h, eidx, wsel, rank, cnt = _router(x, g, shift, scale, rw_t, rbias, seq)
    counts = cnt[:, 0]
    padded = ((counts + te - 1) // te) * te
    ends = jnp.cumsum(padded)
    starts = ends - padded
    pos = rank
    for e in range(N_EXPERTS):
        pos = pos + jnp.where(eidx == e, starts[e], 0)
    pos_flat = pos.reshape(-1)
    tile_start = jnp.arange(n_tiles, dtype=jnp.int32) * te
    tile_expert = jnp.minimum(jnp.sum(tile_start[:, None] >= ends[None, :], axis=1), N_EXPERTS - 1).astype(jnp.int32)
    tile_valid = (tile_start < ends[-1]).astype(jnp.int32)
    tile_expert = jnp.where(tile_valid > 0, tile_expert, tile_expert[jnp.maximum(ends[-1] // te - 1, 0)])
    prev_expert = jnp.concatenate([jnp.full((1,), -1, jnp.int32), tile_expert[:-1]])
    tile_first = ((tile_valid > 0) & (tile_expert != prev_expert)).astype(jnp.int32)
    later = (tile_expert[None, :] > tile_expert[:, None]) & (tile_valid[None, :] > 0)
    tile_next = jnp.min(jnp.where(later, tile_expert[None, :], N_EXPERTS), axis=1)
    tile_next = jnp.where(tile_next < N_EXPERTS, tile_next, -1).astype(jnp.int32)
    fill = jnp.stack([jnp.concatenate([starts + counts, ends[-1:]]),
                      jnp.concatenate([ends, jnp.full((1,), n_rows, ends.dtype)])], axis=1).reshape(-1).astype(jnp.int32)
    src = _invert_positions(pos_flat, fill, n_rows, t)
    y = _experts(tile_expert, tile_valid, tile_first, tile_next, src, h, w_gate, w_up, w_down, layer, n_rows)
    return _combine(pos_flat, x, wsel.T, gate, y, seq, next_norm)


def kernel(x, c, positions, ada_w, ada_b, norm1_g, norm2_g, ab_w_in, q_norm_g, k_norm_g, sinks, gm_vnorm_g, gm_ws, gm_b, ab_w_out, c_w_in, c_lower_bounds, c_onorm_g, c_w_out, router_w, router_bias, moe_w_gate, moe_w_up, moe_w_down):
    bsz, seq, d = x.shape
    depth = ada_w.shape[0]
    t = bsz * seq
    xt = x.reshape(t, d)

    c_pad = jnp.concatenate([c, jnp.zeros((8 - bsz % 8, d), F32)], axis=0) if bsz % 8 else c
    mod = _ada_mod(c_pad, ada_w, ada_b)[:, :bsz]
    mod = mod.reshape(depth, bsz, 6, 1, d)
    cos_t, sin_t = _rope_tables(positions)

    sm = jax.nn.softmax(c_lower_bounds.astype(F32), axis=0)
    lower_bounds = jnp.cumsum(sm, axis=0) - sm[0:1]
    rw_t = router_w.T
    ab_in, ab_out, c_in, c_out = (w.astype(BF16) for w in (ab_w_in, ab_w_out, c_w_in, c_w_out))

    h1 = None
    for l in range(depth):
        sh1, sc1, g1, sh2, sc2, g2 = (mod[l, :, k] for k in range(6))
        w_in, idx = (ab_in, l // 2) if l % 2 == 0 else (c_in, l // 2)
        proj = _in_proj(xt, norm1_g[l], sh1, sc1, w_in, idx, seq) if h1 is None else _in_proj_normed(h1, w_in, idx)
        if l % 2 == 0:
            i = l // 2
            mix = _mix_even(proj, cos_t, sin_t, q_norm_g[i], k_norm_g[i], sinks[i],
                            gm_vnorm_g[i], gm_ws[i], gm_b[i], bsz, seq)
            xt = _out_proj(mix, ab_out, i, xt, g1, seq)
        else:
            j = l // 2
            o = _hgrn(proj, lower_bounds[l], c_onorm_g[j], bsz, seq)
            xt = _out_proj(o, c_out, j, xt, g1, seq)
        next_norm = (norm1_g[l + 1], mod[l + 1, :, 0], mod[l + 1, :, 1]) if l + 1 < depth else None
        out = _moe(xt, norm2_g[l], sh2, sc2, g2, rw_t, router_bias,
                   moe_w_gate, moe_w_up, moe_w_down, l, seq, next_norm)
        xt, h1 = out if next_norm is not None else (out, None)
    return xt.reshape(bsz, seq, d)
```

```python
import functools

import jax
import jax.numpy as jnp
from jax import lax
from jax.experimental import pallas as pl
from jax.experimental.pallas import tpu as pltpu

F32 = jnp.float32
BF16 = jnp.bfloat16

EPS = 1e-6
HEAD_DIM = 64
KV_REP = 4
WINDOW = 128
ROPE_THETA = 10000.0
B_GROUP_DIM = 128
B_CHUNK = 128
C_KDIM = 128
C_VDIM = 128
C_CHUNK = 64
N_GROUPS = 4
EXPERTS_PER_GROUP = 4
N_EXPERTS = N_GROUPS * EXPERTS_PER_GROUP
TOP_K = 2

LANES = 128
SUBLANES = 8
VMEM_LIMIT = 56 * 1024 * 1024

ROW_TILE = 512
PROJ_ROW_TILE = 1024
PROJ_COL_TILE = 2048
EXPERT_TILE = 256
GATHER_AHEAD = 2
WEIGHT_DMA_PRIORITY = 1
HGRN_BLOCK = 16
HGRN_MAX_BLOCK_DECAY = 100.0


def _nt_dot(a, b):
    return lax.dot_general(a, b, (((1,), (1,)), ((), ())), preferred_element_type=F32)


def _params(sem):
    return pltpu.CompilerParams(dimension_semantics=sem, vmem_limit_bytes=VMEM_LIMIT)


def _ada_kernel(c_ref, w_ref, b_ref, o_ref):
    c = c_ref[...]
    cond = c * jax.nn.sigmoid(c)
    o_ref[0] = jnp.dot(cond, w_ref[0], preferred_element_type=F32) + b_ref[0]


def _ada_mod(c_pad, ada_w, ada_b):
    depth, d, n = ada_w.shape
    tn = 1536
    rows = c_pad.shape[0]
    return pl.pallas_call(
        _ada_kernel,
        out_shape=jax.ShapeDtypeStruct((depth, rows, n), F32),
        grid=(depth, n // tn),
        in_specs=[
            pl.BlockSpec((rows, d), lambda l, j: (0, 0)),
            pl.BlockSpec((1, d, tn), lambda l, j: (l, 0, j)),
            pl.BlockSpec((1, 1, tn), lambda l, j: (l, 0, j)),
        ],
        out_specs=pl.BlockSpec((1, rows, tn), lambda l, j: (l, 0, j)),
        compiler_params=_params(("arbitrary", "arbitrary")),
        name="ada_mod",
    )(c_pad, ada_w, ada_b.reshape(depth, 1, n))


def _rope_table_kernel(pos_ref, freq_ref, sign_ref, cos_ref, sin_ref):
    ang = pos_ref[...].astype(F32) * freq_ref[...]
    cos_ref[...] = jnp.cos(ang)
    sin_ref[...] = jnp.sin(ang) * sign_ref[...]


def _rope_tables(positions):
    t = positions.size
    half = HEAD_DIM // 2
    inv_freq = ROPE_THETA ** (-jnp.arange(half, dtype=F32) / half)
    freq = jnp.tile(inv_freq, LANES // half).reshape(1, LANES)
    sign = jnp.tile(jnp.concatenate([-jnp.ones((half,), F32), jnp.ones((half,), F32)]),
                    LANES // HEAD_DIM).reshape(1, LANES)
    tm = min(1024, t)
    return pl.pallas_call(
        _rope_table_kernel,
        out_shape=(jax.ShapeDtypeStruct((t, LANES), F32), jax.ShapeDtypeStruct((t, LANES), F32)),
        grid=(t // tm,),
        in_specs=[
            pl.BlockSpec((tm, 1), lambda i: (i, 0)),
            pl.BlockSpec((1, LANES), lambda i: (0, 0)),
            pl.BlockSpec((1, LANES), lambda i: (0, 0)),
        ],
        out_specs=(pl.BlockSpec((tm, LANES), lambda i: (i, 0)),
                   pl.BlockSpec((tm, LANES), lambda i: (i, 0))),
        compiler_params=_params(("arbitrary",)),
        name="rope_tables",
    )(positions.reshape(t, 1), freq, sign)


def _norm_mod(x, g, shift, scale):
    y = x * lax.rsqrt(jnp.mean(x * x, axis=-1, keepdims=True) + EPS)
    return (y * g) * (1 + scale) + shift


def _in_proj_kernel(x_ref, g_ref, sh_ref, sc_ref, w_ref, o_ref, h_scr):
    @pl.when(pl.program_id(1) == 0)
    def _():
        h_scr[...] = _norm_mod(x_ref[...], g_ref[...], sh_ref[0], sc_ref[0]).astype(BF16)

    o_ref[...] = jnp.dot(h_scr[...], w_ref[0], preferred_element_type=F32).astype(o_ref.dtype)


def _matmul_kernel(h_ref, w_ref, o_ref):
    o_ref[...] = jnp.dot(h_ref[...], w_ref[0], preferred_element_type=F32).astype(o_ref.dtype)


def _in_proj_normed(h, w, layer):
    t, d = h.shape
    n = w.shape[2]
    tm = PROJ_ROW_TILE
    tn = max(c for c in range(LANES, PROJ_COL_TILE + 1, LANES) if n % c == 0)
    assert t % tm == 0
    return pl.pallas_call(
        _matmul_kernel,
        out_shape=jax.ShapeDtypeStruct((t, n), BF16),
        grid=(t // tm, n // tn),
        in_specs=[
            pl.BlockSpec((tm, d), lambda i, j: (i, 0)),
            pl.BlockSpec((1, d, tn), lambda i, j: (layer, 0, j)),
        ],
        out_specs=pl.BlockSpec((tm, tn), lambda i, j: (i, j)),
        compiler_params=_params(("arbitrary", "arbitrary")),
        name="in_proj",
    )(h, w)


def _in_proj(x, g, shift, scale, w, layer, seq):
    t, d = x.shape
    n = w.shape[2]
    tm = PROJ_ROW_TILE
    tn = max(c for c in range(LANES, PROJ_COL_TILE + 1, LANES) if n % c == 0)
    assert seq % tm == 0 and t % tm == 0
    return pl.pallas_call(
        _in_proj_kernel,
        out_shape=jax.ShapeDtypeStruct((t, n), BF16),
        grid=(t // tm, n // tn),
        in_specs=[
            pl.BlockSpec((tm, d), lambda i, j: (i, 0)),
            pl.BlockSpec((1, d), lambda i, j: (0, 0)),
            pl.BlockSpec((1, 1, d), lambda i, j: (i * tm // seq, 0, 0)),
            pl.BlockSpec((1, 1, d), lambda i, j: (i * tm // seq, 0, 0)),
            pl.BlockSpec((1, d, tn), lambda i, j: (layer, 0, j)),
        ],
        out_specs=pl.BlockSpec((tm, tn), lambda i, j: (i, j)),
        scratch_shapes=[pltpu.VMEM((tm, d), BF16)],
        compiler_params=_params(("arbitrary", "arbitrary")),
        name="in_proj",
    )(x, g.reshape(1, d), shift, scale, w)


def _out_proj_kernel(a_ref, w_ref, x_ref, g_ref, o_ref):
    y = jnp.dot(a_ref[...], w_ref[0], preferred_element_type=F32)
    o_ref[...] = x_ref[...] + g_ref[0] * y


def _out_proj(a, w, layer, x, gate, seq):
    t, k = a.shape
    n = w.shape[2]
    tm, tn = PROJ_ROW_TILE, 1024
    assert seq % tm == 0 and t % tm == 0 and n % tn == 0
    return pl.pallas_call(
        _out_proj_kernel,
        out_shape=jax.ShapeDtypeStruct((t, n), F32),
        grid=(t // tm, n // tn),
        in_specs=[
            pl.BlockSpec((tm, k), lambda i, j: (i, 0)),
            pl.BlockSpec((1, k, tn), lambda i, j: (layer, 0, j)),
            pl.BlockSpec((tm, tn), lambda i, j: (i, j)),
            pl.BlockSpec((1, 1, tn), lambda i, j: (i * tm // seq, 0, j)),
        ],
        out_specs=pl.BlockSpec((tm, tn), lambda i, j: (i, j)),
        compiler_params=_params(("arbitrary", "arbitrary")),
        name="out_proj",
    )(a, w, x, gate)


def _segment_mean_matrix():
    r = lax.broadcasted_iota(jnp.int32, (LANES, LANES), 0) // HEAD_DIM
    c = lax.broadcasted_iota(jnp.int32, (LANES, LANES), 1) // HEAD_DIM
    return jnp.where(r == c, 1.0 / HEAD_DIM, 0.0).astype(F32)


def _head_norm_rope(x, gain, cos, sin, seg_mean, low_half):
    ms = jnp.dot(x * x, seg_mean, preferred_element_type=F32)
    y = x * lax.rsqrt(ms + EPS) * gain
    partner = jnp.where(low_half, pltpu.roll(y, LANES - HEAD_DIM // 2, 1),
                        pltpu.roll(y, HEAD_DIM // 2, 1))
    return y * cos + partner * sin


def _gelu_tanh(x):
    return 0.5 * x * (1.0 + jnp.tanh(0.7978845608028654 * (x + 0.044715 * (x * x * x))))


def _mix_kernel(sinks_ref,
                q0_ref, q1_ref, kv_ref, u0_ref, u1_ref, v0_ref, v1_ref,
                cos_ref, sin_ref, qg_ref, kg_ref, vng_ref, ws_ref, bs_ref,
                o_ref, kd_scr, vd_scr):
    n = pl.program_id(1)
    w = WINDOW
    n_kv = kd_scr.shape[0]

    @pl.when(n == 0)
    def _():
        kd_scr[...] = jnp.zeros_like(kd_scr)
        vd_scr[...] = jnp.zeros_like(vd_scr)

    lane = lax.broadcasted_iota(jnp.int32, (w, LANES), 1)
    low_half = (lane % HEAD_DIM) < (HEAD_DIM // 2)
    first_head = lane < HEAD_DIM
    seg_mean = _segment_mean_matrix()
    cos = cos_ref[...]
    sin = sin_ref[...]

    kv = kv_ref[...].astype(F32)
    for c in range(n_kv // 2):
        kc = _head_norm_rope(kv[:, c * LANES:(c + 1) * LANES], kg_ref[...], cos, sin, seg_mean, low_half)
        vc = kv[:, (n_kv // 2 + c) * LANES:(n_kv // 2 + c + 1) * LANES]
        kc_sw = pltpu.roll(kc, HEAD_DIM, 1)
        vc_sw = pltpu.roll(vc, HEAD_DIM, 1)
        kd_scr[2 * c, w:, :] = jnp.where(first_head, kc, kc_sw).astype(BF16)
        kd_scr[2 * c + 1, w:, :] = jnp.where(first_head, kc_sw, kc).astype(BF16)
        vd_scr[2 * c, w:, :] = jnp.where(first_head, vc, vc_sw).astype(BF16)
        vd_scr[2 * c + 1, w:, :] = jnp.where(first_head, vc_sw, vc).astype(BF16)

    qi = lax.broadcasted_iota(jnp.int32, (w, 2 * w), 0)
    kj = lax.broadcasted_iota(jnp.int32, (w, 2 * w), 1)
    diff = qi + w - kj
    band = (diff >= 0) & (diff < w) & ((kj >= w) | (n > 0))
    band4 = jnp.concatenate([band] * KV_REP, axis=0)

    scale = HEAD_DIM ** -0.5
    ones_kv = jnp.ones((2 * w, LANES), BF16)
    q_refs = (q0_ref, q1_ref)
    chunks_per_ref = q0_ref.shape[1] // LANES
    for g in range(n_kv):
        rows = []
        for cc in range(2):
            c = 2 * g + cc
            qc = q_refs[c // chunks_per_ref][:, (c % chunks_per_ref) * LANES:(c % chunks_per_ref + 1) * LANES]
            qc = _head_norm_rope(qc.astype(F32), qg_ref[...], cos, sin, seg_mean, low_half)
            rows.append(jnp.where(first_head, qc, 0.0))
            rows.append(jnp.where(first_head, 0.0, qc))
        qm = jnp.concatenate(rows, axis=0).astype(BF16)
        s = lax.dot_general(qm, kd_scr[g], (((1,), (1,)), ((), ())),
                            preferred_element_type=F32) * scale
        s = jnp.where(band4, s, -jnp.inf)
        sink = jnp.concatenate(
            [jnp.full((w, LANES), sinks_ref[KV_REP * g + r], F32) for r in range(KV_REP)], axis=0)
        m = jnp.maximum(jnp.broadcast_to(jnp.max(s, axis=-1, keepdims=True), (KV_REP * w, LANES)), sink)
        p = jnp.exp(s - jnp.concatenate([m, m], axis=1)).astype(BF16)
        total = jnp.dot(p, ones_kv, preferred_element_type=F32) + jnp.exp(sink - m)
        o = jnp.dot(p, vd_scr[g], preferred_element_type=F32) / total
        for cc in range(2):
            oc = jnp.where(first_head, o[(2 * cc) * w:(2 * cc + 1) * w], o[(2 * cc + 1) * w:(2 * cc + 2) * w])
            c = 2 * g + cc
            o_ref[:, c * LANES:(c + 1) * LANES] = oc.astype(o_ref.dtype)

    kd_scr[:, :w, :] = kd_scr[:, w:, :]
    vd_scr[:, :w, :] = vd_scr[:, w:, :]

    a_q = 2 * q0_ref.shape[1]
    u = _gelu_tanh(jnp.concatenate([u0_ref[...], u1_ref[...]], axis=1).astype(F32))
    v = _gelu_tanh(jnp.concatenate([v0_ref[...], v1_ref[...]], axis=1).astype(F32))
    mu = jnp.mean(v, axis=-1, keepdims=True)
    vc = v - mu
    vn = (vc * lax.rsqrt(jnp.mean(vc * vc, axis=-1, keepdims=True) + EPS) * vng_ref[...]).astype(BF16)
    ti = lax.broadcasted_iota(jnp.int32, (w, w), 0)
    si = lax.broadcasted_iota(jnp.int32, (w, w), 1)
    causal = si <= ti
    bs = bs_ref[...]
    for g in range(ws_ref.shape[0]):
        wg = jnp.where(causal, ws_ref[g], 0.0).astype(BF16)
        sv = jnp.dot(wg, vn[:, g * LANES:(g + 1) * LANES], preferred_element_type=F32) + bs[:, g:g + 1]
        o_ref[:, a_q + g * LANES:a_q + (g + 1) * LANES] = (u[:, g * LANES:(g + 1) * LANES] * sv).astype(o_ref.dtype)


def _mix_even(proj, cos_t, sin_t, q_g, k_g, sinks, vnorm_g, ws, bs, bsz, seq):
    t = proj.shape[0]
    w = WINDOW
    nb = seq // w
    heads = sinks.shape[0]
    n_kv = heads // KV_REP
    a_q = heads * HEAD_DIM
    a_kv = n_kv * HEAD_DIM
    b_w = vnorm_g.shape[0]
    cw = 512
    assert a_q == 2 * cw and 2 * a_kv == cw and b_w == 2 * cw
    row = lambda b, n: b * nb + n
    gain2 = lambda g: jnp.tile(g, LANES // HEAD_DIM).reshape(1, LANES)
    col = lambda j: pl.BlockSpec((w, cw), lambda b, n, s: (row(b, n), j))
    grid_spec = pltpu.PrefetchScalarGridSpec(
        num_scalar_prefetch=1,
        grid=(bsz, nb),
        in_specs=[
            col(0), col(1), col(2), col(3), col(4), col(5), col(6),
            pl.BlockSpec((w, LANES), lambda b, n, s: (row(b, n), 0)),
            pl.BlockSpec((w, LANES), lambda b, n, s: (row(b, n), 0)),
            pl.BlockSpec((1, LANES), lambda b, n, s: (0, 0)),
            pl.BlockSpec((1, LANES), lambda b, n, s: (0, 0)),
            pl.BlockSpec((1, b_w), lambda b, n, s: (0, 0)),
            pl.BlockSpec(ws.shape, lambda b, n, s: (0, 0, 0)),
            pl.BlockSpec((w, ws.shape[0]), lambda b, n, s: (0, 0)),
        ],
        out_specs=pl.BlockSpec((w, a_q + b_w), lambda b, n, s: (row(b, n), 0)),
        scratch_shapes=[pltpu.VMEM((n_kv, 2 * w, LANES), BF16),
                        pltpu.VMEM((n_kv, 2 * w, LANES), BF16)],
    )
    return pl.pallas_call(
        _mix_kernel,
        out_shape=jax.ShapeDtypeStruct((t, a_q + b_w), BF16),
        grid_spec=grid_spec,
        compiler_params=_params(("arbitrary", "arbitrary")),
        name="mix_even",
    )(sinks, proj, proj, proj, proj, proj, proj, proj, cos_t, sin_t,
      gain2(q_g), gain2(k_g), vnorm_g.reshape(1, b_w), ws, bs.T)


def _hgrn_kernel(q_ref, f_ref, i_ref, gt_ref, lb_ref, og_ref, o_ref,
                 oi_scr, qd_scr, u_scr, d_scr):
    c = C_CHUNK
    nc = q_ref.shape[0] // c
    nv = c // SUBLANES
    lb = lb_ref[...]
    og = og_ref[...]
    t_row = lax.broadcasted_iota(jnp.int32, (c, LANES), 0)
    tt = lax.broadcasted_iota(jnp.int32, (c, c), 0)
    ss = lax.broadcasted_iota(jnp.int32, (c, c), 1)

    def inputs(ci):
        r0 = pl.multiple_of(ci * c, c)
        q = q_ref[pl.ds(r0, c), :].astype(F32)
        fl = f_ref[pl.ds(r0, c), :].astype(F32)
        v = i_ref[pl.ds(r0, c), :]
        f = lb + (1 - lb) * jax.nn.sigmoid(fl)
        k = 1 - f
        b = jnp.log2(f)
        for d in (1, 2, 4, 8, 16, 32):
            b = b + jnp.where(t_row >= d, pltpu.roll(b, d, 0), 0.0)
        return q, f, k, b, v

    def row(b, r, n):
        b3 = b.reshape(nv, SUBLANES, LANES)
        rep = jnp.broadcast_to(b3[r // SUBLANES:r // SUBLANES + 1, r % SUBLANES:r % SUBLANES + 1, :],
                               (n // SUBLANES, SUBLANES, LANES))
        return rep.reshape(n, LANES)

    def half_level(q, k, b, m):
        a_parts, b_parts = [], []
        for blk in range(c // (2 * m)):
            lo, mid, hi = blk * 2 * m, blk * 2 * m + m, (blk + 1) * 2 * m
            zero = jnp.zeros((m, LANES), BF16)
            ref = row(b, mid - 1, m)
            a_parts += [zero, (q[mid:hi] * jnp.exp2(b[mid:hi] - ref)).astype(BF16)]
            b_parts += [(k[lo:mid] * jnp.exp2(ref - b[lo:mid])).astype(BF16), zero]
        return _nt_dot(jnp.concatenate(a_parts, axis=0), jnp.concatenate(b_parts, axis=0))

    def finish(ci, q, k, b, v, scores):
        oi_scr[ci] = jnp.dot(scores.astype(BF16), v, preferred_element_type=F32)
        qd_scr[ci] = (q * jnp.exp2(b)).astype(BF16)
        b_last = b[c - 1:c, :]
        kd = (k * jnp.exp2(b_last - b)).astype(BF16)
        u_scr[ci] = lax.dot_general(v, kd, (((0,), (0,)), ((), ())), preferred_element_type=F32)
        d_scr[ci] = jnp.exp2(b_last)

    def intra_fast(ci, worst):
        q, f, k, b, v = inputs(ci)
        blk = HGRN_BLOCK
        starts = [jnp.zeros((blk, LANES), F32)] + [row(b, s - 1, blk) for s in range(blk, c, blk)]
        base = jnp.concatenate(starts, axis=0)
        s_d = _nt_dot((q * jnp.exp2(b - base)).astype(BF16), (k * jnp.exp2(base - b)).astype(BF16))
        s_16 = half_level(q, k, b, blk)
        s_32 = half_level(q, k, b, 2 * blk)
        scores = jnp.where(((tt // blk) == (ss // blk)) & (ss <= tt), s_d,
                           jnp.where((tt // (2 * blk)) == (ss // (2 * blk)), s_16, s_32))
        finish(ci, q, k, b, v, scores)
        for s in range(0, c, blk):
            worst = jnp.maximum(worst, base[s:s + 1, :] - b[s + blk - 1:s + blk, :])
        return worst

    def intra_safe(ci, carry):
        q, f, k, b, v = inputs(ci)
        scores = jnp.where(tt == ss, jnp.sum(q * k, axis=-1, keepdims=True), 0.0)
        odd = (t_row % 2) == 1
        s_1 = _nt_dot(jnp.where(odd, q * f, 0.0).astype(BF16), jnp.where(odd, 0.0, k).astype(BF16))
        scores = scores + jnp.where((tt // 2) == (ss // 2), s_1, 0.0)
        b3 = b.reshape(nv, SUBLANES, LANES)
        sub = lax.broadcasted_iota(jnp.int32, b3.shape, 1)
        small_refs = {4: jnp.broadcast_to(b3[:, 3:4, :], b3.shape).reshape(c, LANES),
                      2: jnp.where(sub < 4, jnp.broadcast_to(b3[:, 1:2, :], b3.shape),
                                   jnp.broadcast_to(b3[:, 5:6, :], b3.shape)).reshape(c, LANES)}
        for m in (32, 16, 8, 4, 2):
            if m >= SUBLANES:
                s_l = half_level(q, k, b, m)
            else:
                ref = small_refs[m]
                upper = (t_row % (2 * m)) >= m
                s_l = _nt_dot(jnp.where(upper, q * jnp.exp2(b - ref), 0.0).astype(BF16),
                              jnp.where(upper, 0.0, k * jnp.exp2(ref - b)).astype(BF16))
            scores = scores + (s_l if 2 * m == c else jnp.where((tt // (2 * m)) == (ss // (2 * m)), s_l, 0.0))
        finish(ci, q, k, b, v, scores)
        return carry

    worst = lax.fori_loop(0, nc, intra_fast, jnp.zeros((1, LANES), F32), unroll=8)

    @pl.when(jnp.max(worst) > HGRN_MAX_BLOCK_DECAY)
    def _():
        lax.fori_loop(0, nc, intra_safe, 0, unroll=2)

    def inter(ci, state_t):
        r0 = pl.multiple_of(ci * c, c)
        gate = gt_ref[pl.ds(r0, c), :].astype(F32)
        o = oi_scr[ci] + _nt_dot(qd_scr[ci], state_t.astype(BF16))
        y = o * lax.rsqrt(jnp.mean(o * o, axis=-1, keepdims=True) + EPS) * og
        o_ref[pl.ds(r0, c), :] = (y * (gate * jax.nn.sigmoid(gate))).astype(o_ref.dtype)
        return state_t * d_scr[ci] + u_scr[ci]

    lax.fori_loop(0, nc, inter, jnp.zeros((C_VDIM, C_KDIM), F32), unroll=8)


def _hgrn(proj, lb, og, bsz, seq):
    t, n4 = proj.shape
    heads = n4 // (4 * LANES)
    nc = seq // C_CHUNK
    return pl.pallas_call(
        _hgrn_kernel,
        out_shape=jax.ShapeDtypeStruct((t, heads * C_VDIM), BF16),
        grid=(bsz, heads),
        in_specs=[
            pl.BlockSpec((seq, LANES), lambda b, h: (b, h)),
            pl.BlockSpec((seq, LANES), lambda b, h: (b, heads + h)),
            pl.BlockSpec((seq, LANES), lambda b, h: (b, 2 * heads + h)),
            pl.BlockSpec((seq, LANES), lambda b, h: (b, 3 * heads + h)),
            pl.BlockSpec((1, LANES), lambda b, h: (0, h)),
            pl.BlockSpec((1, LANES), lambda b, h: (0, 0)),
        ],
        out_specs=pl.BlockSpec((seq, LANES), lambda b, h: (b, h)),
        scratch_shapes=[pltpu.VMEM((nc, C_CHUNK, C_VDIM), F32),
                        pltpu.VMEM((nc, C_CHUNK, C_KDIM), BF16),
                        pltpu.VMEM((nc, C_VDIM, C_KDIM), F32),
                        pltpu.VMEM((nc, 1, C_KDIM), F32)],
        compiler_params=_params(("arbitrary", "arbitrary")),
        name="hgrn2",
    )(proj, proj, proj, proj, lb.reshape(1, -1), og.reshape(1, LANES))


def _first_max(vals):
    best, idx = vals[0], jnp.zeros(vals[0].shape, jnp.int32)
    for i in range(1, len(vals)):
        better = vals[i] > best
        best = jnp.where(better, vals[i], best)
        idx = jnp.where(better, i, idx)
    return best, idx


def _pick(idx, vals):
    out = vals[-1]
    for i in range(len(vals) - 2, -1, -1):
        out = jnp.where(idx == i, vals[i], out)
    return out


def _router_kernel(x_ref, g_ref, sh_ref, sc_ref, rw_ref, rb_ref,
                   h_ref, eidx_ref, wsel_ref, rank_ref, cnt_ref, carry_scr):
    i = pl.program_id(0)
    tm = x_ref.shape[0]

    @pl.when(i == 0)
    def _():
        carry_scr[...] = jnp.zeros_like(carry_scr)

    h = _norm_mod(x_ref[...], g_ref[...], sh_ref[0], sc_ref[0])
    h_ref[...] = h
    rw = rw_ref[...]
    w_hi = rw.astype(BF16)
    w_lo = (rw - w_hi.astype(F32)).astype(BF16)
    h_hi = h.astype(BF16)
    h_lo = (h - h_hi.astype(F32)).astype(BF16)
    head = _nt_dot(jnp.concatenate([w_hi, w_lo], axis=0), h_hi)
    logits = head[:N_EXPERTS] + head[N_EXPERTS:] + _nt_dot(w_hi, h_lo)
    scores = jax.nn.sigmoid(logits)
    sel = scores + rb_ref[...]
    epg = EXPERTS_PER_GROUP
    sel_rows = [sel[e:e + 1, :] for e in range(N_EXPERTS)]
    sc_rows = [scores[e:e + 1, :] for e in range(N_EXPERTS)]
    gsum = []
    for g in range(N_GROUPS):
        r = sel_rows[g * epg:(g + 1) * epg]
        pair = None
        for a in range(epg):
            for b in range(a + 1, epg):
                pair = r[a] + r[b] if pair is None else jnp.maximum(pair, r[a] + r[b])
        gsum.append(pair)
    _, g_idx = _first_max(gsum)
    cand = [_pick(g_idx, [sel_rows[g * epg + j] for g in range(N_GROUPS)]) for j in range(epg)]
    cand_sc = [_pick(g_idx, [sc_rows[g * epg + j] for g in range(N_GROUPS)]) for j in range(epg)]
    _, i1 = _first_max(cand)
    _, i2 = _first_max([jnp.where(i1 == j, -jnp.inf, cand[j]) for j in range(epg)])
    w1 = _pick(i1, cand_sc)
    w2 = _pick(i2, cand_sc)
    tot = w1 + w2
    e1 = g_idx * epg + i1
    e2 = g_idx * epg + i2
    eidx_ref[0:1, :] = e1
    eidx_ref[1:2, :] = e2
    wsel_ref[0:1, :] = w1 / tot
    wsel_ref[1:2, :] = w2 / tot

    erow = lax.broadcasted_iota(jnp.int32, (N_EXPERTS, tm), 0)
    oh1 = erow == e1
    oh2 = erow == e2
    onehot = jnp.where(oh1 | oh2, 1.0, 0.0)
    si = lax.broadcasted_iota(jnp.int32, (tm, tm), 0)
    ti = lax.broadcasted_iota(jnp.int32, (tm, tm), 1)
    incl = jnp.where(si <= ti, 1.0, 0.0).astype(BF16)
    csum = jnp.dot(onehot.astype(BF16), incl, preferred_element_type=F32)
    carry = carry_scr[...]
    before = csum - onehot + carry[:, 0:1]
    r1 = jnp.sum(jnp.where(oh1, before, 0.0), axis=0, keepdims=True)
    r2 = jnp.sum(jnp.where(oh2, before, 0.0), axis=0, keepdims=True)
    rank_ref[0:1, :] = r1.astype(jnp.int32)
    rank_ref[1:2, :] = r2.astype(jnp.int32)
    new_carry = carry + csum[:, tm - 1:tm]
    carry_scr[...] = new_carry
    cnt_ref[...] = new_carry.astype(jnp.int32)


def _router(x, g, shift, scale, rw_t, rbias, seq):
    t, d = x.shape
    tm = ROW_TILE
    e = rw_t.shape[0]
    row = lambda i: (0, i)
    return pl.pallas_call(
        _router_kernel,
        out_shape=(jax.ShapeDtypeStruct((t, d), F32),
                   jax.ShapeDtypeStruct((TOP_K, t), jnp.int32),
                   jax.ShapeDtypeStruct((TOP_K, t), F32),
                   jax.ShapeDtypeStruct((TOP_K, t), jnp.int32),
                   jax.ShapeDtypeStruct((e, LANES), jnp.int32)),
        grid=(t // tm,),
        in_specs=[
            pl.BlockSpec((tm, d), lambda i: (i, 0)),
            pl.BlockSpec((1, d), lambda i: (0, 0)),
            pl.BlockSpec((1, 1, d), lambda i: (i * tm // seq, 0, 0)),
            pl.BlockSpec((1, 1, d), lambda i: (i * tm // seq, 0, 0)),
            pl.BlockSpec((e, d), lambda i: (0, 0)),
            pl.BlockSpec((e, 1), lambda i: (0, 0)),
        ],
        out_specs=(pl.BlockSpec((tm, d), lambda i: (i, 0)),
                   pl.BlockSpec((TOP_K, tm), row),
                   pl.BlockSpec((TOP_K, tm), row),
                   pl.BlockSpec((TOP_K, tm), row),
                   pl.BlockSpec((e, LANES), lambda i: (0, 0))),
        scratch_shapes=[pltpu.VMEM((e, LANES), F32)],
        compiler_params=_params(("arbitrary",)),
        name="router",
    )(x, g.reshape(1, d), shift, scale, rw_t, rbias.reshape(e, 1))


def _invert_kernel(pos_ref, fill_ref, src_ref, *, n_tok):
    def clear(p, carry):
        src_ref[p] = 0
        return carry

    for e in range(N_EXPERTS + 1):
        lax.fori_loop(fill_ref[2 * e], fill_ref[2 * e + 1], clear, 0)

    for slot in range(TOP_K):
        def place(tok, carry, slot=slot):
            src_ref[pos_ref[slot * n_tok + tok]] = tok
            return carry

        lax.fori_loop(0, n_tok, place, 0, unroll=8)


def _invert_positions(pos_flat, fill, n_rows, n_tok):
    assert pos_flat.shape[0] == TOP_K * n_tok
    return pl.pallas_call(
        functools.partial(_invert_kernel, n_tok=n_tok),
        out_shape=jax.ShapeDtypeStruct((n_rows,), jnp.int32),
        in_specs=[pl.BlockSpec(memory_space=pltpu.SMEM), pl.BlockSpec(memory_space=pltpu.SMEM)],
        out_specs=pl.BlockSpec(memory_space=pltpu.SMEM),
        name="invert_positions",
    )(pos_flat, fill)


def _expert_kernel(te_ref, tv_ref, tf_ref, tn_ref, src_ref, h_hbm, wg_hbm, wu_hbm, wd_hbm, y_ref,
                   xbuf0, xbuf1, xbuf2, sg, su, sd, bg, bu, bd, gsem, wsem, *, layer):
    i = pl.program_id(0)
    xbufs = (xbuf0, xbuf1, xbuf2)
    nbuf = len(xbufs)
    ahead = nbuf - 1
    tm = xbuf0.shape[0]

    def gather(tile, buf):
        for r in range(tm):
            tok = src_ref[tile * tm + r]
            pltpu.make_async_copy(h_hbm.at[pl.ds(tok, 1)], xbufs[buf].at[pl.ds(r, 1)], gsem.at[buf]).start()

    def fetch(e):
        return (pltpu.make_async_copy(wg_hbm.at[layer, e], sg, wsem.at[0]),
                pltpu.make_async_copy(wu_hbm.at[layer, e], su, wsem.at[1]),
                pltpu.make_async_copy(wd_hbm.at[layer, e], sd, wsem.at[2]))

    @pl.when(i == 0)
    def _():
        for c in fetch(te_ref[0]):
            c.start(priority=WEIGHT_DMA_PRIORITY)
        for k in range(ahead):
            gather(k, k)

    pending = (i < ahead) | (tv_ref[jnp.maximum(i - ahead, 0)] > 0)
    for buf in range(nbuf):
        @pl.when(pending & (i % nbuf == buf))
        def _(buf=buf):
            pltpu.make_async_copy(h_hbm.at[pl.ds(0, tm)], xbufs[buf], gsem.at[buf]).wait()

    @pl.when(tf_ref[i] > 0)
    def _():
        for c in fetch(te_ref[i]):
            c.wait()
        bg[...] = sg[...].astype(BF16)
        bu[...] = su[...].astype(BF16)
        bd[...] = sd[...].astype(BF16)

        @pl.when(tn_ref[i] >= 0)
        def _():
            for c in fetch(tn_ref[i]):
                c.start(priority=WEIGHT_DMA_PRIORITY)

    for buf in range(nbuf):
        @pl.when((tv_ref[i] > 0) & (i % nbuf == buf))
        def _(buf=buf):
            gather(i + ahead, (buf + ahead) % nbuf)
            x = xbufs[buf][...].astype(BF16)
            a = jnp.dot(x, bg[...], preferred_element_type=F32)
            u = jnp.dot(x, bu[...], preferred_element_type=F32)
            hid = (a * jax.nn.sigmoid(a) * u).astype(BF16)
            y_ref[...] = jnp.dot(hid, bd[...], preferred_element_type=F32)

    @pl.when(tv_ref[i] == 0)
    def _():
        y_ref[...] = jnp.zeros_like(y_ref)


def _experts(tile_expert, tile_valid, tile_first, tile_next, src, h, w_gate, w_up, w_down, layer, n_rows):
    d = h.shape[1]
    f = w_gate.shape[3]
    tm = EXPERT_TILE
    grid_spec = pltpu.PrefetchScalarGridSpec(
        num_scalar_prefetch=5,
        grid=(n_rows // tm,),
        in_specs=[pl.BlockSpec(memory_space=pl.ANY)] * 4,
        out_specs=pl.BlockSpec((tm, d), lambda i, *_: (i, 0)),
        scratch_shapes=[pltpu.VMEM((tm, d), F32)] * (GATHER_AHEAD + 1) + [
                        pltpu.VMEM((d, f), F32), pltpu.VMEM((d, f), F32), pltpu.VMEM((f, d), F32),
                        pltpu.VMEM((d, f), BF16), pltpu.VMEM((d, f), BF16), pltpu.VMEM((f, d), BF16),
                        pltpu.SemaphoreType.DMA((GATHER_AHEAD + 1,)), pltpu.SemaphoreType.DMA((3,))],
    )
    return pl.pallas_call(
        functools.partial(_expert_kernel, layer=layer),
        out_shape=jax.ShapeDtypeStruct((n_rows, d), F32),
        grid_spec=grid_spec,
        compiler_params=_params(("arbitrary",)),
        name="experts",
    )(tile_expert, tile_valid, tile_first, tile_next, src, h, w_gate, w_up, w_down)


def _combine_kernel(pos_ref, x_ref, w_ref, g_ref, y_ref, *rest, with_next):
    if with_next:
        ng_ref, nsh_ref, nsc_ref, o_ref, hn_ref, buf0, buf1, sem = rest
    else:
        o_ref, buf0, buf1, sem = rest
    i = pl.program_id(0)
    n = pl.num_programs(0)
    tm = x_ref.shape[0]
    t = n * tm
    bufs = (buf0, buf1)

    def gather(tile, par):
        for r in range(tm):
            for slot in range(TOP_K):
                p = pos_ref[slot * t + tile * tm + r]
                pltpu.make_async_copy(y_ref.at[pl.ds(p, 1)], bufs[par].at[slot, pl.ds(r, 1)],
                                      sem.at[par]).start(priority=slot)

    def combine(par):
        w = w_ref[...]
        y = w[:, 0:1] * bufs[par][0] + w[:, 1:2] * bufs[par][1]
        x_new = x_ref[...] + g_ref[0] * y
        o_ref[...] = x_new
        if with_next:
            hn_ref[...] = _norm_mod(x_new, ng_ref[...], nsh_ref[0], nsc_ref[0]).astype(BF16)

    @pl.when(i == 0)
    def _():
        gather(0, 0)

    for par in range(2):
        @pl.when(i % 2 == par)
        def _(par=par):
            for slot in range(TOP_K):
                pltpu.make_async_copy(y_ref.at[pl.ds(0, tm)], bufs[par].at[slot], sem.at[par]).wait()

        @pl.when((i % 2 == par) & (i + 1 < n))
        def _(par=par):
            gather(i + 1, 1 - par)
            combine(par)

        @pl.when((i % 2 == par) & (i + 1 == n))
        def _(par=par):
            combine(par)


def _combine(pos_flat, x, wsel_t, gate, y, seq, next_norm=None):
    t, d = x.shape
    tm = ROW_TILE
    assert seq % tm == 0 and t % tm == 0
    row = pl.BlockSpec((tm, d), lambda i, p: (i, 0))
    per_batch = pl.BlockSpec((1, 1, d), lambda i, p: (i * tm // seq, 0, 0))
    in_specs = [row, pl.BlockSpec((tm, TOP_K), lambda i, p: (i, 0)), per_batch, pl.BlockSpec(memory_space=pl.ANY)]
    operands = [pos_flat, x, wsel_t, gate, y]
    out_shape = jax.ShapeDtypeStruct((t, d), F32)
    out_specs = row
    if next_norm is not None:
        in_specs += [pl.BlockSpec((1, d), lambda i, p: (0, 0)), per_batch, per_batch]
        operands += [next_norm[0].reshape(1, d), next_norm[1], next_norm[2]]
        out_shape = (out_shape, jax.ShapeDtypeStruct((t, d), BF16))
        out_specs = (row, row)
    grid_spec = pltpu.PrefetchScalarGridSpec(
        num_scalar_prefetch=1,
        grid=(t // tm,),
        in_specs=in_specs,
        out_specs=out_specs,
        scratch_shapes=[pltpu.VMEM((TOP_K, tm, d), F32), pltpu.VMEM((TOP_K, tm, d), F32),
                        pltpu.SemaphoreType.DMA((2,))],
    )
    return pl.pallas_call(
        functools.partial(_combine_kernel, with_next=next_norm is not None),
        out_shape=out_shape,
        grid_spec=grid_spec,
        compiler_params=_params(("arbitrary",)),
        name="combine",
    )(*operands)


def _moe(x, g, shift, scale, gate, rw_t, rbias, w_gate, w_up, w_down, layer, seq, next_norm):
    t, d = x.shape
    te = EXPERT_TILE
    n_rows = TOP_K * t + (N_EXPERTS + GATHER_AHEAD - 1) * te
    n_tiles = n_rows // te
    h, eidx, wsel, rank, cnt = _router(x, g, shift, scale, rw_t, rbias, seq)
    counts = cnt[:, 0]
    padded = ((counts + te - 1) // te) * te
    ends = jnp.cumsum(padded)
    starts = ends - padded
    pos = rank
    for e in range(N_EXPERTS):
        pos = pos + jnp.where(eidx == e, starts[e], 0)
    pos_flat = pos.reshape(-1)
    tile_start = jnp.arange(n_tiles, dtype=jnp.int32) * te
    tile_expert = jnp.minimum(jnp.sum(tile_start[:, None] >= ends[None, :], axis=1), N_EXPERTS - 1).astype(jnp.int32)
    tile_valid = (tile_start < ends[-1]).astype(jnp.int32)
    tile_expert = jnp.where(tile_valid > 0, tile_expert, tile_expert[jnp.maximum(ends[-1] // te - 1, 0)])
    prev_expert = jnp.concatenate([jnp.full((1,), -1, jnp.int32), tile_expert[:-1]])
    tile_first = ((tile_valid > 0) & (tile_expert != prev_expert)).astype(jnp.int32)
    later = (tile_expert[None, :] > tile_expert[:, None]) & (tile_valid[None, :] > 0)
    tile_next = jnp.min(jnp.where(later, tile_expert[None, :], N_EXPERTS), axis=1)
    tile_next = jnp.where(tile_next < N_EXPERTS, tile_next, -1).astype(jnp.int32)
    fill = jnp.stack([jnp.concatenate([starts + counts, ends[-1:]]),
                      jnp.concatenate([ends, jnp.full((1,), n_rows, ends.dtype)])], axis=1).reshape(-1).astype(jnp.int32)
    src = _invert_positions(pos_flat, fill, n_rows, t)
    y = _experts(tile_expert, tile_valid, tile_first, tile_next, src, h, w_gate, w_up, w_down, layer, n_rows)
    return _combine(pos_flat, x, wsel.T, gate, y, seq, next_norm)


def kernel(x, c, positions, ada_w, ada_b, norm1_g, norm2_g, ab_w_in, q_norm_g, k_norm_g, sinks, gm_vnorm_g, gm_ws, gm_b, ab_w_out, c_w_in, c_lower_bounds, c_onorm_g, c_w_out, router_w, router_bias, moe_w_gate, moe_w_up, moe_w_down):
    bsz, seq, d = x.shape
    depth = ada_w.shape[0]
    t = bsz * seq
    xt = x.reshape(t, d)

    c_pad = jnp.concatenate([c, jnp.zeros((8 - bsz % 8, d), F32)], axis=0) if bsz % 8 else c
    mod = _ada_mod(c_pad, ada_w, ada_b)[:, :bsz]
    mod = mod.reshape(depth, bsz, 6, 1, d)
    cos_t, sin_t = _rope_tables(positions)

    sm = jax.nn.softmax(c_lower_bounds.astype(F32), axis=0)
    lower_bounds = jnp.cumsum(sm, axis=0) - sm[0:1]
    rw_t = router_w.T
    ab_in, ab_out, c_in, c_out = (w.astype(BF16) for w in (ab_w_in, ab_w_out, c_w_in, c_w_out))

    h1 = None
    for l in range(depth):
        sh1, sc1, g1, sh2, sc2, g2 = (mod[l, :, k] for k in range(6))
        w_in, idx = (ab_in, l // 2) if l % 2 == 0 else (c_in, l // 2)
        proj = _in_proj(xt, norm1_g[l], sh1, sc1, w_in, idx, seq) if h1 is None else _in_proj_normed(h1, w_in, idx)
        if l % 2 == 0:
            i = l // 2
            mix = _mix_even(proj, cos_t, sin_t, q_norm_g[i], k_norm_g[i], sinks[i],
                            gm_vnorm_g[i], gm_ws[i], gm_b[i], bsz, seq)
            xt = _out_proj(mix, ab_out, i, xt, g1, seq)
        else:
            j = l // 2
            o = _hgrn(proj, lower_bounds[l], c_onorm_g[j], bsz, seq)
            xt = _out_proj(o, c_out, j, xt, g1, seq)
        next_norm = (norm1_g[l + 1], mod[l + 1, :, 0], mod[l + 1, :, 1]) if l + 1 < depth else None
        out = _moe(xt, norm2_g[l], sh2, sc2, g2, rw_t, router_bias,
                   moe_w_gate, moe_w_up, moe_w_down, l, seq, next_norm)
        xt, h1 = out if next_norm is not None else (out, None)
    return xt.reshape(bsz, seq, d)
```

```python
import functools

import jax
import jax.numpy as jnp
from jax import lax
from jax.experimental import pallas as pl
from jax.experimental.pallas import tpu as pltpu

F32 = jnp.float32
BF16 = jnp.bfloat16

EPS = 1e-6
HEAD_DIM = 64
KV_REP = 4
WINDOW = 128
ROPE_THETA = 10000.0
B_GROUP_DIM = 128
B_CHUNK = 128
C_KDIM = 128
C_VDIM = 128
C_CHUNK = 64
N_GROUPS = 4
EXPERTS_PER_GROUP = 4
N_EXPERTS = N_GROUPS * EXPERTS_PER_GROUP
TOP_K = 2

LANES = 128
SUBLANES = 8
VMEM_LIMIT = 56 * 1024 * 1024

ROW_TILE = 512
PROJ_ROW_TILE = 1024
PROJ_COL_TILE = 2048
EXPERT_TILE = 256
WEIGHT_DMA_PRIORITY = 1
HGRN_BLOCK = 16
HGRN_MAX_BLOCK_DECAY = 100.0


def _nt_dot(a, b):
    return lax.dot_general(a, b, (((1,), (1,)), ((), ())), preferred_element_type=F32)


def _params(sem):
    return pltpu.CompilerParams(dimension_semantics=sem, vmem_limit_bytes=VMEM_LIMIT)


def _ada_kernel(c_ref, w_ref, b_ref, o_ref):
    c = c_ref[...]
    cond = c * jax.nn.sigmoid(c)
    o_ref[0] = jnp.dot(cond, w_ref[0], preferred_element_type=F32) + b_ref[0]


def _ada_mod(c_pad, ada_w, ada_b):
    depth, d, n = ada_w.shape
    tn = 1536
    rows = c_pad.shape[0]
    return pl.pallas_call(
        _ada_kernel,
        out_shape=jax.ShapeDtypeStruct((depth, rows, n), F32),
        grid=(depth, n // tn),
        in_specs=[
            pl.BlockSpec((rows, d), lambda l, j: (0, 0)),
            pl.BlockSpec((1, d, tn), lambda l, j: (l, 0, j)),
            pl.BlockSpec((1, 1, tn), lambda l, j: (l, 0, j)),
        ],
        out_specs=pl.BlockSpec((1, rows, tn), lambda l, j: (l, 0, j)),
        compiler_params=_params(("arbitrary", "arbitrary")),
        name="ada_mod",
    )(c_pad, ada_w, ada_b.reshape(depth, 1, n))


def _rope_table_kernel(pos_ref, freq_ref, sign_ref, cos_ref, sin_ref):
    ang = pos_ref[...].astype(F32) * freq_ref[...]
    cos_ref[...] = jnp.cos(ang)
    sin_ref[...] = jnp.sin(ang) * sign_ref[...]


def _rope_tables(positions):
    t = positions.size
    half = HEAD_DIM // 2
    inv_freq = ROPE_THETA ** (-jnp.arange(half, dtype=F32) / half)
    freq = jnp.tile(inv_freq, LANES // half).reshape(1, LANES)
    sign = jnp.tile(jnp.concatenate([-jnp.ones((half,), F32), jnp.ones((half,), F32)]),
                    LANES // HEAD_DIM).reshape(1, LANES)
    tm = min(1024, t)
    return pl.pallas_call(
        _rope_table_kernel,
        out_shape=(jax.ShapeDtypeStruct((t, LANES), F32), jax.ShapeDtypeStruct((t, LANES), F32)),
        grid=(t // tm,),
        in_specs=[
            pl.BlockSpec((tm, 1), lambda i: (i, 0)),
            pl.BlockSpec((1, LANES), lambda i: (0, 0)),
            pl.BlockSpec((1, LANES), lambda i: (0, 0)),
        ],
        out_specs=(pl.BlockSpec((tm, LANES), lambda i: (i, 0)),
                   pl.BlockSpec((tm, LANES), lambda i: (i, 0))),
        compiler_params=_params(("arbitrary",)),
        name="rope_tables",
    )(positions.reshape(t, 1), freq, sign)


def _norm_mod(x, g, shift, scale):
    y = x * lax.rsqrt(jnp.mean(x * x, axis=-1, keepdims=True) + EPS)
    return (y * g) * (1 + scale) + shift


def _in_proj_kernel(x_ref, g_ref, sh_ref, sc_ref, w_ref, o_ref, h_scr):
    @pl.when(pl.program_id(1) == 0)
    def _():
        h_scr[...] = _norm_mod(x_ref[...], g_ref[...], sh_ref[0], sc_ref[0]).astype(BF16)

    o_ref[...] = jnp.dot(h_scr[...], w_ref[0], preferred_element_type=F32).astype(o_ref.dtype)


def _matmul_kernel(h_ref, w_ref, o_ref):
    o_ref[...] = jnp.dot(h_ref[...], w_ref[0], preferred_element_type=F32).astype(o_ref.dtype)


def _in_proj_normed(h, w, layer):
    t, d = h.shape
    n = w.shape[2]
    tm = PROJ_ROW_TILE
    tn = max(c for c in range(LANES, PROJ_COL_TILE + 1, LANES) if n % c == 0)
    assert t % tm == 0
    return pl.pallas_call(
        _matmul_kernel,
        out_shape=jax.ShapeDtypeStruct((t, n), BF16),
        grid=(t // tm, n // tn),
        in_specs=[
            pl.BlockSpec((tm, d), lambda i, j: (i, 0)),
            pl.BlockSpec((1, d, tn), lambda i, j: (layer, 0, j)),
        ],
        out_specs=pl.BlockSpec((tm, tn), lambda i, j: (i, j)),
        compiler_params=_params(("arbitrary", "arbitrary")),
        name="in_proj",
    )(h, w)


def _in_proj(x, g, shift, scale, w, layer, seq):
    t, d = x.shape
    n = w.shape[2]
    tm = PROJ_ROW_TILE
    tn = max(c for c in range(LANES, PROJ_COL_TILE + 1, LANES) if n % c == 0)
    assert seq % tm == 0 and t % tm == 0
    return pl.pallas_call(
        _in_proj_kernel,
        out_shape=jax.ShapeDtypeStruct((t, n), BF16),
        grid=(t // tm, n // tn),
        in_specs=[
            pl.BlockSpec((tm, d), lambda i, j: (i, 0)),
            pl.BlockSpec((1, d), lambda i, j: (0, 0)),
            pl.BlockSpec((1, 1, d), lambda i, j: (i * tm // seq, 0, 0)),
            pl.BlockSpec((1, 1, d), lambda i, j: (i * tm // seq, 0, 0)),
            pl.BlockSpec((1, d, tn), lambda i, j: (layer, 0, j)),
        ],
        out_specs=pl.BlockSpec((tm, tn), lambda i, j: (i, j)),
        scratch_shapes=[pltpu.VMEM((tm, d), BF16)],
        compiler_params=_params(("arbitrary", "arbitrary")),
        name="in_proj",
    )(x, g.reshape(1, d), shift, scale, w)


def _out_proj_kernel(a_ref, w_ref, x_ref, g_ref, o_ref):
    y = jnp.dot(a_ref[...], w_ref[0], preferred_element_type=F32)
    o_ref[...] = x_ref[...] + g_ref[0] * y


def _out_proj(a, w, layer, x, gate, seq):
    t, k = a.shape
    n = w.shape[2]
    tm, tn = PROJ_ROW_TILE, 1024
    assert seq % tm == 0 and t % tm == 0 and n % tn == 0
    return pl.pallas_call(
        _out_proj_kernel,
        out_shape=jax.ShapeDtypeStruct((t, n), F32),
        grid=(t // tm, n // tn),
        in_specs=[
            pl.BlockSpec((tm, k), lambda i, j: (i, 0)),
            pl.BlockSpec((1, k, tn), lambda i, j: (layer, 0, j)),
            pl.BlockSpec((tm, tn), lambda i, j: (i, j)),
            pl.BlockSpec((1, 1, tn), lambda i, j: (i * tm // seq, 0, j)),
        ],
        out_specs=pl.BlockSpec((tm, tn), lambda i, j: (i, j)),
        compiler_params=_params(("arbitrary", "arbitrary")),
        name="out_proj",
    )(a, w, x, gate)


def _segment_mean_matrix():
    r = lax.broadcasted_iota(jnp.int32, (LANES, LANES), 0) // HEAD_DIM
    c = lax.broadcasted_iota(jnp.int32, (LANES, LANES), 1) // HEAD_DIM
    return jnp.where(r == c, 1.0 / HEAD_DIM, 0.0).astype(F32)


def _head_norm_rope(x, gain, cos, sin, seg_mean, low_half):
    ms = jnp.dot(x * x, seg_mean, preferred_element_type=F32)
    y = x * lax.rsqrt(ms + EPS) * gain
    partner = jnp.where(low_half, pltpu.roll(y, LANES - HEAD_DIM // 2, 1),
                        pltpu.roll(y, HEAD_DIM // 2, 1))
    return y * cos + partner * sin


def _gelu_tanh(x):
    return 0.5 * x * (1.0 + jnp.tanh(0.7978845608028654 * (x + 0.044715 * (x * x * x))))


def _mix_kernel(sinks_ref,
                q0_ref, q1_ref, kv_ref, u0_ref, u1_ref, v0_ref, v1_ref,
                cos_ref, sin_ref, qg_ref, kg_ref, vng_ref, ws_ref, bs_ref,
                o_ref, kd_scr, vd_scr):
    n = pl.program_id(1)
    w = WINDOW
    n_kv = kd_scr.shape[0]

    @pl.when(n == 0)
    def _():
        kd_scr[...] = jnp.zeros_like(kd_scr)
        vd_scr[...] = jnp.zeros_like(vd_scr)

    lane = lax.broadcasted_iota(jnp.int32, (w, LANES), 1)
    low_half = (lane % HEAD_DIM) < (HEAD_DIM // 2)
    first_head = lane < HEAD_DIM
    seg_mean = _segment_mean_matrix()
    cos = cos_ref[...]
    sin = sin_ref[...]

    kv = kv_ref[...].astype(F32)
    for c in range(n_kv // 2):
        kc = _head_norm_rope(kv[:, c * LANES:(c + 1) * LANES], kg_ref[...], cos, sin, seg_mean, low_half)
        vc = kv[:, (n_kv // 2 + c) * LANES:(n_kv // 2 + c + 1) * LANES]
        kc_sw = pltpu.roll(kc, HEAD_DIM, 1)
        vc_sw = pltpu.roll(vc, HEAD_DIM, 1)
        kd_scr[2 * c, w:, :] = jnp.where(first_head, kc, kc_sw).astype(BF16)
        kd_scr[2 * c + 1, w:, :] = jnp.where(first_head, kc_sw, kc).astype(BF16)
        vd_scr[2 * c, w:, :] = jnp.where(first_head, vc, vc_sw).astype(BF16)
        vd_scr[2 * c + 1, w:, :] = jnp.where(first_head, vc_sw, vc).astype(BF16)

    qi = lax.broadcasted_iota(jnp.int32, (w, 2 * w), 0)
    kj = lax.broadcasted_iota(jnp.int32, (w, 2 * w), 1)
    diff = qi + w - kj
    band = (diff >= 0) & (diff < w) & ((kj >= w) | (n > 0))
    band4 = jnp.concatenate([band] * KV_REP, axis=0)

    scale = HEAD_DIM ** -0.5
    ones_kv = jnp.ones((2 * w, LANES), BF16)
    q_refs = (q0_ref, q1_ref)
    chunks_per_ref = q0_ref.shape[1] // LANES
    for g in range(n_kv):
        rows = []
        for cc in range(2):
            c = 2 * g + cc
            qc = q_refs[c // chunks_per_ref][:, (c % chunks_per_ref) * LANES:(c % chunks_per_ref + 1) * LANES]
            qc = _head_norm_rope(qc.astype(F32), qg_ref[...], cos, sin, seg_mean, low_half)
            rows.append(jnp.where(first_head, qc, 0.0))
            rows.append(jnp.where(first_head, 0.0, qc))
        qm = jnp.concatenate(rows, axis=0).astype(BF16)
        s = lax.dot_general(qm, kd_scr[g], (((1,), (1,)), ((), ())),
                            preferred_element_type=F32) * scale
        s = jnp.where(band4, s, -jnp.inf)
        sink = jnp.concatenate(
            [jnp.full((w, LANES), sinks_ref[KV_REP * g + r], F32) for r in range(KV_REP)], axis=0)
        m = jnp.maximum(jnp.broadcast_to(jnp.max(s, axis=-1, keepdims=True), (KV_REP * w, LANES)), sink)
        p = jnp.exp(s - jnp.concatenate([m, m], axis=1)).astype(BF16)
        total = jnp.dot(p, ones_kv, preferred_element_type=F32) + jnp.exp(sink - m)
        o = jnp.dot(p, vd_scr[g], preferred_element_type=F32) / total
        for cc in range(2):
            oc = jnp.where(first_head, o[(2 * cc) * w:(2 * cc + 1) * w], o[(2 * cc + 1) * w:(2 * cc + 2) * w])
            c = 2 * g + cc
            o_ref[:, c * LANES:(c + 1) * LANES] = oc.astype(o_ref.dtype)

    kd_scr[:, :w, :] = kd_scr[:, w:, :]
    vd_scr[:, :w, :] = vd_scr[:, w:, :]

    a_q = 2 * q0_ref.shape[1]
    u = _gelu_tanh(jnp.concatenate([u0_ref[...], u1_ref[...]], axis=1).astype(F32))
    v = _gelu_tanh(jnp.concatenate([v0_ref[...], v1_ref[...]], axis=1).astype(F32))
    mu = jnp.mean(v, axis=-1, keepdims=True)
    vc = v - mu
    vn = (vc * lax.rsqrt(jnp.mean(vc * vc, axis=-1, keepdims=True) + EPS) * vng_ref[...]).astype(BF16)
    ti = lax.broadcasted_iota(jnp.int32, (w, w), 0)
    si = lax.broadcasted_iota(jnp.int32, (w, w), 1)
    causal = si <= ti
    bs = bs_ref[...]
    for g in range(ws_ref.shape[0]):
        wg = jnp.where(causal, ws_ref[g], 0.0).astype(BF16)
        sv = jnp.dot(wg, vn[:, g * LANES:(g + 1) * LANES], preferred_element_type=F32) + bs[:, g:g + 1]
        o_ref[:, a_q + g * LANES:a_q + (g + 1) * LANES] = (u[:, g * LANES:(g + 1) * LANES] * sv).astype(o_ref.dtype)


def _mix_even(proj, cos_t, sin_t, q_g, k_g, sinks, vnorm_g, ws, bs, bsz, seq):
    t = proj.shape[0]
    w = WINDOW
    nb = seq // w
    heads = sinks.shape[0]
    n_kv = heads // KV_REP
    a_q = heads * HEAD_DIM
    a_kv = n_kv * HEAD_DIM
    b_w = vnorm_g.shape[0]
    cw = 512
    assert a_q == 2 * cw and 2 * a_kv == cw and b_w == 2 * cw
    row = lambda b, n: b * nb + n
    gain2 = lambda g: jnp.tile(g, LANES // HEAD_DIM).reshape(1, LANES)
    col = lambda j: pl.BlockSpec((w, cw), lambda b, n, s: (row(b, n), j))
    grid_spec = pltpu.PrefetchScalarGridSpec(
        num_scalar_prefetch=1,
        grid=(bsz, nb),
        in_specs=[
            col(0), col(1), col(2), col(3), col(4), col(5), col(6),
            pl.BlockSpec((w, LANES), lambda b, n, s: (row(b, n), 0)),
            pl.BlockSpec((w, LANES), lambda b, n, s: (row(b, n), 0)),
            pl.BlockSpec((1, LANES), lambda b, n, s: (0, 0)),
            pl.BlockSpec((1, LANES), lambda b, n, s: (0, 0)),
            pl.BlockSpec((1, b_w), lambda b, n, s: (0, 0)),
            pl.BlockSpec(ws.shape, lambda b, n, s: (0, 0, 0)),
            pl.BlockSpec((w, ws.shape[0]), lambda b, n, s: (0, 0)),
        ],
        out_specs=pl.BlockSpec((w, a_q + b_w), lambda b, n, s: (row(b, n), 0)),
        scratch_shapes=[pltpu.VMEM((n_kv, 2 * w, LANES), BF16),
                        pltpu.VMEM((n_kv, 2 * w, LANES), BF16)],
    )
    return pl.pallas_call(
        _mix_kernel,
        out_shape=jax.ShapeDtypeStruct((t, a_q + b_w), BF16),
        grid_spec=grid_spec,
        compiler_params=_params(("arbitrary", "arbitrary")),
        name="mix_even",
    )(sinks, proj, proj, proj, proj, proj, proj, proj, cos_t, sin_t,
      gain2(q_g), gain2(k_g), vnorm_g.reshape(1, b_w), ws, bs.T)


def _hgrn_kernel(q_ref, f_ref, i_ref, gt_ref, lb_ref, og_ref, o_ref,
                 oi_scr, qd_scr, u_scr, d_scr):
    c = C_CHUNK
    nc = q_ref.shape[0] // c
    nv = c // SUBLANES
    lb = lb_ref[...]
    og = og_ref[...]
    t_row = lax.broadcasted_iota(jnp.int32, (c, LANES), 0)
    tt = lax.broadcasted_iota(jnp.int32, (c, c), 0)
    ss = lax.broadcasted_iota(jnp.int32, (c, c), 1)

    def inputs(ci):
        r0 = pl.multiple_of(ci * c, c)
        q = q_ref[pl.ds(r0, c), :].astype(F32)
        fl = f_ref[pl.ds(r0, c), :].astype(F32)
        v = i_ref[pl.ds(r0, c), :]
        f = lb + (1 - lb) * jax.nn.sigmoid(fl)
        k = 1 - f
        b = jnp.log2(f)
        for d in (1, 2, 4, 8, 16, 32):
            b = b + jnp.where(t_row >= d, pltpu.roll(b, d, 0), 0.0)
        return q, f, k, b, v

    def row(b, r, n):
        b3 = b.reshape(nv, SUBLANES, LANES)
        rep = jnp.broadcast_to(b3[r // SUBLANES:r // SUBLANES + 1, r % SUBLANES:r % SUBLANES + 1, :],
                               (n // SUBLANES, SUBLANES, LANES))
        return rep.reshape(n, LANES)

    def half_level(q, k, b, m):
        a_parts, b_parts = [], []
        for blk in range(c // (2 * m)):
            lo, mid, hi = blk * 2 * m, blk * 2 * m + m, (blk + 1) * 2 * m
            zero = jnp.zeros((m, LANES), BF16)
            ref = row(b, mid - 1, m)
            a_parts += [zero, (q[mid:hi] * jnp.exp2(b[mid:hi] - ref)).astype(BF16)]
            b_parts += [(k[lo:mid] * jnp.exp2(ref - b[lo:mid])).astype(BF16), zero]
        return _nt_dot(jnp.concatenate(a_parts, axis=0), jnp.concatenate(b_parts, axis=0))

    def finish(ci, q, k, b, v, scores):
        oi_scr[ci] = jnp.dot(scores.astype(BF16), v, preferred_element_type=F32)
        qd_scr[ci] = (q * jnp.exp2(b)).astype(BF16)
        b_last = b[c - 1:c, :]
        kd = (k * jnp.exp2(b_last - b)).astype(BF16)
        u_scr[ci] = lax.dot_general(v, kd, (((0,), (0,)), ((), ())), preferred_element_type=F32)
        d_scr[ci] = jnp.exp2(b_last)

    def intra_fast(ci, worst):
        q, f, k, b, v = inputs(ci)
        blk = HGRN_BLOCK
        starts = [jnp.zeros((blk, LANES), F32)] + [row(b, s - 1, blk) for s in range(blk, c, blk)]
        base = jnp.concatenate(starts, axis=0)
        s_d = _nt_dot((q * jnp.exp2(b - base)).astype(BF16), (k * jnp.exp2(base - b)).astype(BF16))
        s_16 = half_level(q, k, b, blk)
        s_32 = half_level(q, k, b, 2 * blk)
        scores = jnp.where(((tt // blk) == (ss // blk)) & (ss <= tt), s_d,
                           jnp.where((tt // (2 * blk)) == (ss // (2 * blk)), s_16, s_32))
        finish(ci, q, k, b, v, scores)
        for s in range(0, c, blk):
            worst = jnp.maximum(worst, base[s:s + 1, :] - b[s + blk - 1:s + blk, :])
        return worst

    def intra_safe(ci, carry):
        q, f, k, b, v = inputs(ci)
        scores = jnp.where(tt == ss, jnp.sum(q * k, axis=-1, keepdims=True), 0.0)
        odd = (t_row % 2) == 1
        s_1 = _nt_dot(jnp.where(odd, q * f, 0.0).astype(BF16), jnp.where(odd, 0.0, k).astype(BF16))
        scores = scores + jnp.where((tt // 2) == (ss // 2), s_1, 0.0)
        b3 = b.reshape(nv, SUBLANES, LANES)
        sub = lax.broadcasted_iota(jnp.int32, b3.shape, 1)
        small_refs = {4: jnp.broadcast_to(b3[:, 3:4, :], b3.shape).reshape(c, LANES),
                      2: jnp.where(sub < 4, jnp.broadcast_to(b3[:, 1:2, :], b3.shape),
                                   jnp.broadcast_to(b3[:, 5:6, :], b3.shape)).reshape(c, LANES)}
        for m in (32, 16, 8, 4, 2):
            if m >= SUBLANES:
                s_l = half_level(q, k, b, m)
            else:
                ref = small_refs[m]
                upper = (t_row % (2 * m)) >= m
                s_l = _nt_dot(jnp.where(upper, q * jnp.exp2(b - ref), 0.0).astype(BF16),
                              jnp.where(upper, 0.0, k * jnp.exp2(ref - b)).astype(BF16))
            scores = scores + (s_l if 2 * m == c else jnp.where((tt // (2 * m)) == (ss // (2 * m)), s_l, 0.0))
        finish(ci, q, k, b, v, scores)
        return carry

    worst = lax.fori_loop(0, nc, intra_fast, jnp.zeros((1, LANES), F32), unroll=8)

    @pl.when(jnp.max(worst) > HGRN_MAX_BLOCK_DECAY)
    def _():
        lax.fori_loop(0, nc, intra_safe, 0, unroll=2)

    def inter(ci, state_t):
        r0 = pl.multiple_of(ci * c, c)
        gate = gt_ref[pl.ds(r0, c), :].astype(F32)
        o = oi_scr[ci] + _nt_dot(qd_scr[ci], state_t.astype(BF16))
        y = o * lax.rsqrt(jnp.mean(o * o, axis=-1, keepdims=True) + EPS) * og
        o_ref[pl.ds(r0, c), :] = (y * (gate * jax.nn.sigmoid(gate))).astype(o_ref.dtype)
        return state_t * d_scr[ci] + u_scr[ci]

    lax.fori_loop(0, nc, inter, jnp.zeros((C_VDIM, C_KDIM), F32), unroll=8)


def _hgrn(proj, lb, og, bsz, seq):
    t, n4 = proj.shape
    heads = n4 // (4 * LANES)
    nc = seq // C_CHUNK
    return pl.pallas_call(
        _hgrn_kernel,
        out_shape=jax.ShapeDtypeStruct((t, heads * C_VDIM), BF16),
        grid=(bsz, heads),
        in_specs=[
            pl.BlockSpec((seq, LANES), lambda b, h: (b, h)),
            pl.BlockSpec((seq, LANES), lambda b, h: (b, heads + h)),
            pl.BlockSpec((seq, LANES), lambda b, h: (b, 2 * heads + h)),
            pl.BlockSpec((seq, LANES), lambda b, h: (b, 3 * heads + h)),
            pl.BlockSpec((1, LANES), lambda b, h: (0, h)),
            pl.BlockSpec((1, LANES), lambda b, h: (0, 0)),
        ],
        out_specs=pl.BlockSpec((seq, LANES), lambda b, h: (b, h)),
        scratch_shapes=[pltpu.VMEM((nc, C_CHUNK, C_VDIM), F32),
                        pltpu.VMEM((nc, C_CHUNK, C_KDIM), BF16),
                        pltpu.VMEM((nc, C_VDIM, C_KDIM), F32),
                        pltpu.VMEM((nc, 1, C_KDIM), F32)],
        compiler_params=_params(("arbitrary", "arbitrary")),
        name="hgrn2",
    )(proj, proj, proj, proj, lb.reshape(1, -1), og.reshape(1, LANES))


def _first_max(vals):
    best, idx = vals[0], jnp.zeros(vals[0].shape, jnp.int32)
    for i in range(1, len(vals)):
        better = vals[i] > best
        best = jnp.where(better, vals[i], best)
        idx = jnp.where(better, i, idx)
    return best, idx


def _pick(idx, vals):
    out = vals[-1]
    for i in range(len(vals) - 2, -1, -1):
        out = jnp.where(idx == i, vals[i], out)
    return out


def _router_kernel(x_ref, g_ref, sh_ref, sc_ref, rw_ref, rb_ref,
                   h_ref, eidx_ref, wsel_ref, rank_ref, cnt_ref, carry_scr):
    i = pl.program_id(0)
    tm = x_ref.shape[0]

    @pl.when(i == 0)
    def _():
        carry_scr[...] = jnp.zeros_like(carry_scr)

    h = _norm_mod(x_ref[...], g_ref[...], sh_ref[0], sc_ref[0])
    h_ref[...] = h
    rw = rw_ref[...]
    w_hi = rw.astype(BF16)
    w_lo = (rw - w_hi.astype(F32)).astype(BF16)
    h_hi = h.astype(BF16)
    h_lo = (h - h_hi.astype(F32)).astype(BF16)
    head = _nt_dot(jnp.concatenate([w_hi, w_lo], axis=0), h_hi)
    logits = head[:N_EXPERTS] + head[N_EXPERTS:] + _nt_dot(w_hi, h_lo)
    scores = jax.nn.sigmoid(logits)
    sel = scores + rb_ref[...]
    epg = EXPERTS_PER_GROUP
    sel_rows = [sel[e:e + 1, :] for e in range(N_EXPERTS)]
    sc_rows = [scores[e:e + 1, :] for e in range(N_EXPERTS)]
    gsum = []
    for g in range(N_GROUPS):
        r = sel_rows[g * epg:(g + 1) * epg]
        pair = None
        for a in range(epg):
            for b in range(a + 1, epg):
                pair = r[a] + r[b] if pair is None else jnp.maximum(pair, r[a] + r[b])
        gsum.append(pair)
    _, g_idx = _first_max(gsum)
    cand = [_pick(g_idx, [sel_rows[g * epg + j] for g in range(N_GROUPS)]) for j in range(epg)]
    cand_sc = [_pick(g_idx, [sc_rows[g * epg + j] for g in range(N_GROUPS)]) for j in range(epg)]
    _, i1 = _first_max(cand)
    _, i2 = _first_max([jnp.where(i1 == j, -jnp.inf, cand[j]) for j in range(epg)])
    w1 = _pick(i1, cand_sc)
    w2 = _pick(i2, cand_sc)
    tot = w1 + w2
    e1 = g_idx * epg + i1
    e2 = g_idx * epg + i2
    eidx_ref[0:1, :] = e1
    eidx_ref[1:2, :] = e2
    wsel_ref[0:1, :] = w1 / tot
    wsel_ref[1:2, :] = w2 / tot

    erow = lax.broadcasted_iota(jnp.int32, (N_EXPERTS, tm), 0)
    oh1 = erow == e1
    oh2 = erow == e2
    onehot = jnp.where(oh1 | oh2, 1.0, 0.0)
    si = lax.broadcasted_iota(jnp.int32, (tm, tm), 0)
    ti = lax.broadcasted_iota(jnp.int32, (tm, tm), 1)
    incl = jnp.where(si <= ti, 1.0, 0.0).astype(BF16)
    csum = jnp.dot(onehot.astype(BF16), incl, preferred_element_type=F32)
    carry = carry_scr[...]
    before = csum - onehot + carry[:, 0:1]
    r1 = jnp.sum(jnp.where(oh1, before, 0.0), axis=0, keepdims=True)
    r2 = jnp.sum(jnp.where(oh2, before, 0.0), axis=0, keepdims=True)
    rank_ref[0:1, :] = r1.astype(jnp.int32)
    rank_ref[1:2, :] = r2.astype(jnp.int32)
    new_carry = carry + csum[:, tm - 1:tm]
    carry_scr[...] = new_carry
    cnt_ref[...] = new_carry.astype(jnp.int32)


def _router(x, g, shift, scale, rw_t, rbias, seq):
    t, d = x.shape
    tm = ROW_TILE
    e = rw_t.shape[0]
    row = lambda i: (0, i)
    return pl.pallas_call(
        _router_kernel,
        out_shape=(jax.ShapeDtypeStruct((t, d), F32),
                   jax.ShapeDtypeStruct((TOP_K, t), jnp.int32),
                   jax.ShapeDtypeStruct((TOP_K, t), F32),
                   jax.ShapeDtypeStruct((TOP_K, t), jnp.int32),
                   jax.ShapeDtypeStruct((e, LANES), jnp.int32)),
        grid=(t // tm,),
        in_specs=[
            pl.BlockSpec((tm, d), lambda i: (i, 0)),
            pl.BlockSpec((1, d), lambda i: (0, 0)),
            pl.BlockSpec((1, 1, d), lambda i: (i * tm // seq, 0, 0)),
            pl.BlockSpec((1, 1, d), lambda i: (i * tm // seq, 0, 0)),
            pl.BlockSpec((e, d), lambda i: (0, 0)),
            pl.BlockSpec((e, 1), lambda i: (0, 0)),
        ],
        out_specs=(pl.BlockSpec((tm, d), lambda i: (i, 0)),
                   pl.BlockSpec((TOP_K, tm), row),
                   pl.BlockSpec((TOP_K, tm), row),
                   pl.BlockSpec((TOP_K, tm), row),
                   pl.BlockSpec((e, LANES), lambda i: (0, 0))),
        scratch_shapes=[pltpu.VMEM((e, LANES), F32)],
        compiler_params=_params(("arbitrary",)),
        name="router",
    )(x, g.reshape(1, d), shift, scale, rw_t, rbias.reshape(e, 1))


def _invert_kernel(pos_ref, fill_ref, src_ref, *, n_tok):
    def clear(p, carry):
        src_ref[p] = 0
        return carry

    for e in range(N_EXPERTS + 1):
        lax.fori_loop(fill_ref[2 * e], fill_ref[2 * e + 1], clear, 0)

    for slot in range(TOP_K):
        def place(tok, carry, slot=slot):
            src_ref[pos_ref[slot * n_tok + tok]] = tok
            return carry

        lax.fori_loop(0, n_tok, place, 0, unroll=8)


def _invert_positions(pos_flat, fill, n_rows, n_tok):
    assert pos_flat.shape[0] == TOP_K * n_tok
    return pl.pallas_call(
        functools.partial(_invert_kernel, n_tok=n_tok),
        out_shape=jax.ShapeDtypeStruct((n_rows,), jnp.int32),
        in_specs=[pl.BlockSpec(memory_space=pltpu.SMEM), pl.BlockSpec(memory_space=pltpu.SMEM)],
        out_specs=pl.BlockSpec(memory_space=pltpu.SMEM),
        name="invert_positions",
    )(pos_flat, fill)


def _expert_kernel(te_ref, tv_ref, tf_ref, tn_ref, src_ref, h_hbm, wg_hbm, wu_hbm, wd_hbm, y_ref,
                   xbuf0, xbuf1, sg, su, sd, bg, bu, bd, gsem, wsem, *, layer):
    i = pl.program_id(0)
    xbufs = (xbuf0, xbuf1)
    tm = xbuf0.shape[0]

    def gather(tile, buf):
        for r in range(tm):
            tok = src_ref[tile * tm + r]
            pltpu.make_async_copy(h_hbm.at[pl.ds(tok, 1)], xbufs[buf].at[pl.ds(r, 1)], gsem.at[buf]).start()

    def fetch(e):
        return (pltpu.make_async_copy(wg_hbm.at[layer, e], sg, wsem.at[0]),
                pltpu.make_async_copy(wu_hbm.at[layer, e], su, wsem.at[1]),
                pltpu.make_async_copy(wd_hbm.at[layer, e], sd, wsem.at[2]))

    @pl.when(i == 0)
    def _():
        for c in fetch(te_ref[0]):
            c.start(priority=WEIGHT_DMA_PRIORITY)
        gather(0, 0)

    pending = (i == 0) | (tv_ref[jnp.maximum(i - 1, 0)] > 0)
    for par in range(2):
        @pl.when(pending & (i % 2 == par))
        def _(par=par):
            pltpu.make_async_copy(h_hbm.at[pl.ds(0, tm)], xbufs[par], gsem.at[par]).wait()

    @pl.when(tf_ref[i] > 0)
    def _():
        for c in fetch(te_ref[i]):
            c.wait()
        bg[...] = sg[...].astype(BF16)
        bu[...] = su[...].astype(BF16)
        bd[...] = sd[...].astype(BF16)

        @pl.when(tn_ref[i] >= 0)
        def _():
            for c in fetch(tn_ref[i]):
                c.start(priority=WEIGHT_DMA_PRIORITY)

    for par in range(2):
        @pl.when((tv_ref[i] > 0) & (i % 2 == par))
        def _(par=par):
            gather(i + 1, 1 - par)
            x = xbufs[par][...].astype(BF16)
            a = jnp.dot(x, bg[...], preferred_element_type=F32)
            u = jnp.dot(x, bu[...], preferred_element_type=F32)
            hid = (a * jax.nn.sigmoid(a) * u).astype(BF16)
            y_ref[...] = jnp.dot(hid, bd[...], preferred_element_type=F32)

    @pl.when(tv_ref[i] == 0)
    def _():
        y_ref[...] = jnp.zeros_like(y_ref)


def _experts(tile_expert, tile_valid, tile_first, tile_next, src, h, w_gate, w_up, w_down, layer, n_rows):
    d = h.shape[1]
    f = w_gate.shape[3]
    tm = EXPERT_TILE
    grid_spec = pltpu.PrefetchScalarGridSpec(
        num_scalar_prefetch=5,
        grid=(n_rows // tm,),
        in_specs=[pl.BlockSpec(memory_space=pl.ANY)] * 4,
        out_specs=pl.BlockSpec((tm, d), lambda i, *_: (i, 0)),
        scratch_shapes=[pltpu.VMEM((tm, d), F32), pltpu.VMEM((tm, d), F32),
                        pltpu.VMEM((d, f), F32), pltpu.VMEM((d, f), F32), pltpu.VMEM((f, d), F32),
                        pltpu.VMEM((d, f), BF16), pltpu.VMEM((d, f), BF16), pltpu.VMEM((f, d), BF16),
                        pltpu.SemaphoreType.DMA((2,)), pltpu.SemaphoreType.DMA((3,))],
    )
    return pl.pallas_call(
        functools.partial(_expert_kernel, layer=layer),
        out_shape=jax.ShapeDtypeStruct((n_rows, d), F32),
        grid_spec=grid_spec,
        compiler_params=_params(("arbitrary",)),
        name="experts",
    )(tile_expert, tile_valid, tile_first, tile_next, src, h, w_gate, w_up, w_down)


def _combine_kernel(pos_ref, x_ref, w_ref, g_ref, y_ref, *rest, with_next):
    if with_next:
        ng_ref, nsh_ref, nsc_ref, o_ref, hn_ref, buf0, buf1, sem = rest
    else:
        o_ref, buf0, buf1, sem = rest
    i = pl.program_id(0)
    n = pl.num_programs(0)
    tm = x_ref.shape[0]
    t = n * tm
    bufs = (buf0, buf1)

    def gather(tile, par):
        for r in range(tm):
            for slot in range(TOP_K):
                p = pos_ref[slot * t + tile * tm + r]
                pltpu.make_async_copy(y_ref.at[pl.ds(p, 1)], bufs[par].at[slot, pl.ds(r, 1)],
                                      sem.at[par]).start(priority=slot)

    def combine(par):
        w = w_ref[...]
        y = w[:, 0:1] * bufs[par][0] + w[:, 1:2] * bufs[par][1]
        x_new = x_ref[...] + g_ref[0] * y
        o_ref[...] = x_new
        if with_next:
            hn_ref[...] = _norm_mod(x_new, ng_ref[...], nsh_ref[0], nsc_ref[0]).astype(BF16)

    @pl.when(i == 0)
    def _():
        gather(0, 0)

    for par in range(2):
        @pl.when(i % 2 == par)
        def _(par=par):
            for slot in range(TOP_K):
                pltpu.make_async_copy(y_ref.at[pl.ds(0, tm)], bufs[par].at[slot], sem.at[par]).wait()

        @pl.when((i % 2 == par) & (i + 1 < n))
        def _(par=par):
            gather(i + 1, 1 - par)
            combine(par)

        @pl.when((i % 2 == par) & (i + 1 == n))
        def _(par=par):
            combine(par)


def _combine(pos_flat, x, wsel_t, gate, y, seq, next_norm=None):
    t, d = x.shape
    tm = ROW_TILE
    assert seq % tm == 0 and t % tm == 0
    row = pl.BlockSpec((tm, d), lambda i, p: (i, 0))
    per_batch = pl.BlockSpec((1, 1, d), lambda i, p: (i * tm // seq, 0, 0))
    in_specs = [row, pl.BlockSpec((tm, TOP_K), lambda i, p: (i, 0)), per_batch, pl.BlockSpec(memory_space=pl.ANY)]
    operands = [pos_flat, x, wsel_t, gate, y]
    out_shape = jax.ShapeDtypeStruct((t, d), F32)
    out_specs = row
    if next_norm is not None:
        in_specs += [pl.BlockSpec((1, d), lambda i, p: (0, 0)), per_batch, per_batch]
        operands += [next_norm[0].reshape(1, d), next_norm[1], next_norm[2]]
        out_shape = (out_shape, jax.ShapeDtypeStruct((t, d), BF16))
        out_specs = (row, row)
    grid_spec = pltpu.PrefetchScalarGridSpec(
        num_scalar_prefetch=1,
        grid=(t // tm,),
        in_specs=in_specs,
        out_specs=out_specs,
        scratch_shapes=[pltpu.VMEM((TOP_K, tm, d), F32), pltpu.VMEM((TOP_K, tm, d), F32),
                        pltpu.SemaphoreType.DMA((2,))],
    )
    return pl.pallas_call(
        functools.partial(_combine_kernel, with_next=next_norm is not None),
        out_shape=out_shape,
        grid_spec=grid_spec,
        compiler_params=_params(("arbitrary",)),
        name="combine",
    )(*operands)


def _moe(x, g, shift, scale, gate, rw_t, rbias, w_gate, w_up, w_down, layer, seq, next_norm):
    t, d = x.shape
    te = EXPERT_TILE
    n_rows = TOP_K * t + N_EXPERTS * te
    n_tiles = n_rows // te
    h, eidx, wsel, rank, cnt = _router(x, g, shift, scale, rw_t, rbias, seq)
    counts = cnt[:, 0]
    padded = ((counts + te - 1) // te) * te
    ends = jnp.cumsum(padded)
    starts = ends - padded
    pos = rank
    for e in range(N_EXPERTS):
        pos = pos + jnp.where(eidx == e, starts[e], 0)
    pos_flat = pos.reshape(-1)
    tile_start = jnp.arange(n_tiles, dtype=jnp.int32) * te
    tile_expert = jnp.minimum(jnp.sum(tile_start[:, None] >= ends[None, :], axis=1), N_EXPERTS - 1).astype(jnp.int32)
    tile_valid = (tile_start < ends[-1]).astype(jnp.int32)
    tile_expert = jnp.where(tile_valid > 0, tile_expert, tile_expert[jnp.maximum(ends[-1] // te - 1, 0)])
    prev_expert = jnp.concatenate([jnp.full((1,), -1, jnp.int32), tile_expert[:-1]])
    tile_first = ((tile_valid > 0) & (tile_expert != prev_expert)).astype(jnp.int32)
    later = (tile_expert[None, :] > tile_expert[:, None]) & (tile_valid[None, :] > 0)
    tile_next = jnp.min(jnp.where(later, tile_expert[None, :], N_EXPERTS), axis=1)
    tile_next = jnp.where(tile_next < N_EXPERTS, tile_next, -1).astype(jnp.int32)
    fill = jnp.stack([jnp.concatenate([starts + counts, ends[-1:]]),
                      jnp.concatenate([ends, jnp.full((1,), n_rows, ends.dtype)])], axis=1).reshape(-1).astype(jnp.int32)
    src = _invert_positions(pos_flat, fill, n_rows, t)
    y = _experts(tile_expert, tile_valid, tile_first, tile_next, src, h, w_gate, w_up, w_down, layer, n_rows)
    return _combine(pos_flat, x, wsel.T, gate, y, seq, next_norm)


def kernel(x, c, positions, ada_w, ada_b, norm1_g, norm2_g, ab_w_in, q_norm_g, k_norm_g, sinks, gm_vnorm_g, gm_ws, gm_b, ab_w_out, c_w_in, c_lower_bounds, c_onorm_g, c_w_out, router_w, router_bias, moe_w_gate, moe_w_up, moe_w_down):
    bsz, seq, d = x.shape
    depth = ada_w.shape[0]
    t = bsz * seq
    xt = x.reshape(t, d)

    c_pad = jnp.concatenate([c, jnp.zeros((8 - bsz % 8, d), F32)], axis=0) if bsz % 8 else c
    mod = _ada_mod(c_pad, ada_w, ada_b)[:, :bsz]
    mod = mod.reshape(depth, bsz, 6, 1, d)
    cos_t, sin_t = _rope_tables(positions)

    sm = jax.nn.softmax(c_lower_bounds.astype(F32), axis=0)
    lower_bounds = jnp.cumsum(sm, axis=0) - sm[0:1]
    rw_t = router_w.T
    ab_in, ab_out, c_in, c_out = (w.astype(BF16) for w in (ab_w_in, ab_w_out, c_w_in, c_w_out))

    h1 = None
    for l in range(depth):
        sh1, sc1, g1, sh2, sc2, g2 = (mod[l, :, k] for k in range(6))
        w_in, idx = (ab_in, l // 2) if l % 2 == 0 else (c_in, l // 2)
        proj = _in_proj(xt, norm1_g[l], sh1, sc1, w_in, idx, seq) if h1 is None else _in_proj_normed(h1, w_in, idx)
        if l % 2 == 0:
            i = l // 2
            mix = _mix_even(proj, cos_t, sin_t, q_norm_g[i], k_norm_g[i], sinks[i],
                            gm_vnorm_g[i], gm_ws[i], gm_b[i], bsz, seq)
            xt = _out_proj(mix, ab_out, i, xt, g1, seq)
        else:
            j = l // 2
            o = _hgrn(proj, lower_bounds[l], c_onorm_g[j], bsz, seq)
            xt = _out_proj(o, c_out, j, xt, g1, seq)
        next_norm = (norm1_g[l + 1], mod[l + 1, :, 0], mod[l + 1, :, 1]) if l + 1 < depth else None
        out = _moe(xt, norm2_g[l], sh2, sc2, g2, rw_t, router_bias,
                   moe_w_gate, moe_w_up, moe_w_down, l, seq, next_norm)
        xt, h1 = out if next_norm is not None else (out, None)
    return xt.reshape(bsz, seq, d)
```

```python
import functools

import jax
import jax.numpy as jnp
from jax import lax
from jax.experimental import pallas as pl
from jax.experimental.pallas import tpu as pltpu

F32 = jnp.float32
BF16 = jnp.bfloat16

EPS = 1e-6
HEAD_DIM = 64
KV_REP = 4
WINDOW = 128
ROPE_THETA = 10000.0
B_GROUP_DIM = 128
B_CHUNK = 128
C_KDIM = 128
C_VDIM = 128
C_CHUNK = 64
N_GROUPS = 4
EXPERTS_PER_GROUP = 4
N_EXPERTS = N_GROUPS * EXPERTS_PER_GROUP
TOP_K = 2

LANES = 128
SUBLANES = 8
VMEM_LIMIT = 56 * 1024 * 1024

ROW_TILE = 512
PROJ_ROW_TILE = 1024
PROJ_COL_TILE = 2048
EXPERT_TILE = 256
WEIGHT_DMA_PRIORITY = 1
HGRN_BLOCK = 16
HGRN_MAX_BLOCK_DECAY = 100.0


def _nt_dot(a, b):
    return lax.dot_general(a, b, (((1,), (1,)), ((), ())), preferred_element_type=F32)


def _params(sem):
    return pltpu.CompilerParams(dimension_semantics=sem, vmem_limit_bytes=VMEM_LIMIT)


def _ada_kernel(c_ref, w_ref, b_ref, o_ref):
    c = c_ref[...]
    cond = c * jax.nn.sigmoid(c)
    o_ref[0] = jnp.dot(cond, w_ref[0], preferred_element_type=F32) + b_ref[0]


def _ada_mod(c_pad, ada_w, ada_b):
    depth, d, n = ada_w.shape
    tn = 1536
    rows = c_pad.shape[0]
    return pl.pallas_call(
        _ada_kernel,
        out_shape=jax.ShapeDtypeStruct((depth, rows, n), F32),
        grid=(depth, n // tn),
        in_specs=[
            pl.BlockSpec((rows, d), lambda l, j: (0, 0)),
            pl.BlockSpec((1, d, tn), lambda l, j: (l, 0, j)),
            pl.BlockSpec((1, 1, tn), lambda l, j: (l, 0, j)),
        ],
        out_specs=pl.BlockSpec((1, rows, tn), lambda l, j: (l, 0, j)),
        compiler_params=_params(("arbitrary", "arbitrary")),
        name="ada_mod",
    )(c_pad, ada_w, ada_b.reshape(depth, 1, n))


def _rope_table_kernel(pos_ref, freq_ref, sign_ref, cos_ref, sin_ref):
    ang = pos_ref[...].astype(F32) * freq_ref[...]
    cos_ref[...] = jnp.cos(ang)
    sin_ref[...] = jnp.sin(ang) * sign_ref[...]


def _rope_tables(positions):
    t = positions.size
    half = HEAD_DIM // 2
    inv_freq = ROPE_THETA ** (-jnp.arange(half, dtype=F32) / half)
    freq = jnp.tile(inv_freq, LANES // half).reshape(1, LANES)
    sign = jnp.tile(jnp.concatenate([-jnp.ones((half,), F32), jnp.ones((half,), F32)]),
                    LANES // HEAD_DIM).reshape(1, LANES)
    tm = min(1024, t)
    return pl.pallas_call(
        _rope_table_kernel,
        out_shape=(jax.ShapeDtypeStruct((t, LANES), F32), jax.ShapeDtypeStruct((t, LANES), F32)),
        grid=(t // tm,),
        in_specs=[
            pl.BlockSpec((tm, 1), lambda i: (i, 0)),
            pl.BlockSpec((1, LANES), lambda i: (0, 0)),
            pl.BlockSpec((1, LANES), lambda i: (0, 0)),
        ],
        out_specs=(pl.BlockSpec((tm, LANES), lambda i: (i, 0)),
                   pl.BlockSpec((tm, LANES), lambda i: (i, 0))),
        compiler_params=_params(("arbitrary",)),
        name="rope_tables",
    )(positions.reshape(t, 1), freq, sign)


def _norm_mod(x, g, shift, scale):
    y = x * lax.rsqrt(jnp.mean(x * x, axis=-1, keepdims=True) + EPS)
    return (y * g) * (1 + scale) + shift


def _in_proj_kernel(x_ref, g_ref, sh_ref, sc_ref, w_ref, o_ref, h_scr):
    @pl.when(pl.program_id(1) == 0)
    def _():
        h_scr[...] = _norm_mod(x_ref[...], g_ref[...], sh_ref[0], sc_ref[0]).astype(BF16)

    o_ref[...] = jnp.dot(h_scr[...], w_ref[0], preferred_element_type=F32).astype(o_ref.dtype)


def _matmul_kernel(h_ref, w_ref, o_ref):
    o_ref[...] = jnp.dot(h_ref[...], w_ref[0], preferred_element_type=F32).astype(o_ref.dtype)


def _in_proj_normed(h, w, layer):
    t, d = h.shape
    n = w.shape[2]
    tm = PROJ_ROW_TILE
    tn = max(c for c in range(LANES, PROJ_COL_TILE + 1, LANES) if n % c == 0)
    assert t % tm == 0
    return pl.pallas_call(
        _matmul_kernel,
        out_shape=jax.ShapeDtypeStruct((t, n), BF16),
        grid=(t // tm, n // tn),
        in_specs=[
            pl.BlockSpec((tm, d), lambda i, j: (i, 0)),
            pl.BlockSpec((1, d, tn), lambda i, j: (layer, 0, j)),
        ],
        out_specs=pl.BlockSpec((tm, tn), lambda i, j: (i, j)),
        compiler_params=_params(("arbitrary", "arbitrary")),
        name="in_proj",
    )(h, w)


def _in_proj(x, g, shift, scale, w, layer, seq):
    t, d = x.shape
    n = w.shape[2]
    tm = PROJ_ROW_TILE
    tn = max(c for c in range(LANES, PROJ_COL_TILE + 1, LANES) if n % c == 0)
    assert seq % tm == 0 and t % tm == 0
    return pl.pallas_call(
        _in_proj_kernel,
        out_shape=jax.ShapeDtypeStruct((t, n), BF16),
        grid=(t // tm, n // tn),
        in_specs=[
            pl.BlockSpec((tm, d), lambda i, j: (i, 0)),
            pl.BlockSpec((1, d), lambda i, j: (0, 0)),
            pl.BlockSpec((1, 1, d), lambda i, j: (i * tm // seq, 0, 0)),
            pl.BlockSpec((1, 1, d), lambda i, j: (i * tm // seq, 0, 0)),
            pl.BlockSpec((1, d, tn), lambda i, j: (layer, 0, j)),
        ],
        out_specs=pl.BlockSpec((tm, tn), lambda i, j: (i, j)),
        scratch_shapes=[pltpu.VMEM((tm, d), BF16)],
        compiler_params=_params(("arbitrary", "arbitrary")),
        name="in_proj",
    )(x, g.reshape(1, d), shift, scale, w)


def _out_proj_kernel(a_ref, w_ref, x_ref, g_ref, o_ref):
    y = jnp.dot(a_ref[...], w_ref[0], preferred_element_type=F32)
    o_ref[...] = x_ref[...] + g_ref[0] * y


def _out_proj(a, w, layer, x, gate, seq):
    t, k = a.shape
    n = w.shape[2]
    tm = PROJ_ROW_TILE
    assert seq % tm == 0 and t % tm == 0
    return pl.pallas_call(
        _out_proj_kernel,
        out_shape=jax.ShapeDtypeStruct((t, n), F32),
        grid=(t // tm,),
        in_specs=[
            pl.BlockSpec((tm, k), lambda i: (i, 0)),
            pl.BlockSpec((1, k, n), lambda i: (layer, 0, 0), pipeline_mode=pl.Buffered(1)),
            pl.BlockSpec((tm, n), lambda i: (i, 0)),
            pl.BlockSpec((1, 1, n), lambda i: (i * tm // seq, 0, 0)),
        ],
        out_specs=pl.BlockSpec((tm, n), lambda i: (i, 0)),
        compiler_params=_params(("arbitrary",)),
        name="out_proj",
    )(a, w, x, gate)


def _segment_mean_matrix():
    r = lax.broadcasted_iota(jnp.int32, (LANES, LANES), 0) // HEAD_DIM
    c = lax.broadcasted_iota(jnp.int32, (LANES, LANES), 1) // HEAD_DIM
    return jnp.where(r == c, 1.0 / HEAD_DIM, 0.0).astype(F32)


def _head_norm_rope(x, gain, cos, sin, seg_mean, low_half):
    ms = jnp.dot(x * x, seg_mean, preferred_element_type=F32)
    y = x * lax.rsqrt(ms + EPS) * gain
    partner = jnp.where(low_half, pltpu.roll(y, LANES - HEAD_DIM // 2, 1),
                        pltpu.roll(y, HEAD_DIM // 2, 1))
    return y * cos + partner * sin


def _gelu_tanh(x):
    return 0.5 * x * (1.0 + jnp.tanh(0.7978845608028654 * (x + 0.044715 * (x * x * x))))


def _mix_kernel(sinks_ref,
                q0_ref, q1_ref, kv_ref, u0_ref, u1_ref, v0_ref, v1_ref,
                cos_ref, sin_ref, qg_ref, kg_ref, vng_ref, ws_ref, bs_ref,
                o_ref, kd_scr, vd_scr):
    n = pl.program_id(1)
    w = WINDOW
    n_kv = kd_scr.shape[0]

    @pl.when(n == 0)
    def _():
        kd_scr[...] = jnp.zeros_like(kd_scr)
        vd_scr[...] = jnp.zeros_like(vd_scr)

    lane = lax.broadcasted_iota(jnp.int32, (w, LANES), 1)
    low_half = (lane % HEAD_DIM) < (HEAD_DIM // 2)
    first_head = lane < HEAD_DIM
    seg_mean = _segment_mean_matrix()
    cos = cos_ref[...]
    sin = sin_ref[...]

    kv = kv_ref[...].astype(F32)
    for c in range(n_kv // 2):
        kc = _head_norm_rope(kv[:, c * LANES:(c + 1) * LANES], kg_ref[...], cos, sin, seg_mean, low_half)
        vc = kv[:, (n_kv // 2 + c) * LANES:(n_kv // 2 + c + 1) * LANES]
        kc_sw = pltpu.roll(kc, HEAD_DIM, 1)
        vc_sw = pltpu.roll(vc, HEAD_DIM, 1)
        kd_scr[2 * c, w:, :] = jnp.where(first_head, kc, kc_sw).astype(BF16)
        kd_scr[2 * c + 1, w:, :] = jnp.where(first_head, kc_sw, kc).astype(BF16)
        vd_scr[2 * c, w:, :] = jnp.where(first_head, vc, vc_sw).astype(BF16)
        vd_scr[2 * c + 1, w:, :] = jnp.where(first_head, vc_sw, vc).astype(BF16)

    qi = lax.broadcasted_iota(jnp.int32, (w, 2 * w), 0)
    kj = lax.broadcasted_iota(jnp.int32, (w, 2 * w), 1)
    diff = qi + w - kj
    band = (diff >= 0) & (diff < w) & ((kj >= w) | (n > 0))
    band4 = jnp.concatenate([band] * KV_REP, axis=0)

    scale = HEAD_DIM ** -0.5
    ones_kv = jnp.ones((2 * w, LANES), BF16)
    q_refs = (q0_ref, q1_ref)
    chunks_per_ref = q0_ref.shape[1] // LANES
    for g in range(n_kv):
        rows = []
        for cc in range(2):
            c = 2 * g + cc
            qc = q_refs[c // chunks_per_ref][:, (c % chunks_per_ref) * LANES:(c % chunks_per_ref + 1) * LANES]
            qc = _head_norm_rope(qc.astype(F32), qg_ref[...], cos, sin, seg_mean, low_half)
            rows.append(jnp.where(first_head, qc, 0.0))
            rows.append(jnp.where(first_head, 0.0, qc))
        qm = jnp.concatenate(rows, axis=0).astype(BF16)
        s = lax.dot_general(qm, kd_scr[g], (((1,), (1,)), ((), ())),
                            preferred_element_type=F32) * scale
        s = jnp.where(band4, s, -jnp.inf)
        sink = jnp.concatenate(
            [jnp.full((w, LANES), sinks_ref[KV_REP * g + r], F32) for r in range(KV_REP)], axis=0)
        m = jnp.maximum(jnp.broadcast_to(jnp.max(s, axis=-1, keepdims=True), (KV_REP * w, LANES)), sink)
        p = jnp.exp(s - jnp.concatenate([m, m], axis=1)).astype(BF16)
        total = jnp.dot(p, ones_kv, preferred_element_type=F32) + jnp.exp(sink - m)
        o = jnp.dot(p, vd_scr[g], preferred_element_type=F32) / total
        for cc in range(2):
            oc = jnp.where(first_head, o[(2 * cc) * w:(2 * cc + 1) * w], o[(2 * cc + 1) * w:(2 * cc + 2) * w])
            c = 2 * g + cc
            o_ref[:, c * LANES:(c + 1) * LANES] = oc.astype(o_ref.dtype)

    kd_scr[:, :w, :] = kd_scr[:, w:, :]
    vd_scr[:, :w, :] = vd_scr[:, w:, :]

    a_q = 2 * q0_ref.shape[1]
    u = _gelu_tanh(jnp.concatenate([u0_ref[...], u1_ref[...]], axis=1).astype(F32))
    v = _gelu_tanh(jnp.concatenate([v0_ref[...], v1_ref[...]], axis=1).astype(F32))
    mu = jnp.mean(v, axis=-1, keepdims=True)
    vc = v - mu
    vn = (vc * lax.rsqrt(jnp.mean(vc * vc, axis=-1, keepdims=True) + EPS) * vng_ref[...]).astype(BF16)
    ti = lax.broadcasted_iota(jnp.int32, (w, w), 0)
    si = lax.broadcasted_iota(jnp.int32, (w, w), 1)
    causal = si <= ti
    bs = bs_ref[...]
    for g in range(ws_ref.shape[0]):
        wg = jnp.where(causal, ws_ref[g], 0.0).astype(BF16)
        sv = jnp.dot(wg, vn[:, g * LANES:(g + 1) * LANES], preferred_element_type=F32) + bs[:, g:g + 1]
        o_ref[:, a_q + g * LANES:a_q + (g + 1) * LANES] = (u[:, g * LANES:(g + 1) * LANES] * sv).astype(o_ref.dtype)


def _mix_even(proj, cos_t, sin_t, q_g, k_g, sinks, vnorm_g, ws, bs, bsz, seq):
    t = proj.shape[0]
    w = WINDOW
    nb = seq // w
    heads = sinks.shape[0]
    n_kv = heads // KV_REP
    a_q = heads * HEAD_DIM
    a_kv = n_kv * HEAD_DIM
    b_w = vnorm_g.shape[0]
    cw = 512
    assert a_q == 2 * cw and 2 * a_kv == cw and b_w == 2 * cw
    row = lambda b, n: b * nb + n
    gain2 = lambda g: jnp.tile(g, LANES // HEAD_DIM).reshape(1, LANES)
    col = lambda j: pl.BlockSpec((w, cw), lambda b, n, s: (row(b, n), j))
    grid_spec = pltpu.PrefetchScalarGridSpec(
        num_scalar_prefetch=1,
        grid=(bsz, nb),
        in_specs=[
            col(0), col(1), col(2), col(3), col(4), col(5), col(6),
            pl.BlockSpec((w, LANES), lambda b, n, s: (row(b, n), 0)),
            pl.BlockSpec((w, LANES), lambda b, n, s: (row(b, n), 0)),
            pl.BlockSpec((1, LANES), lambda b, n, s: (0, 0)),
            pl.BlockSpec((1, LANES), lambda b, n, s: (0, 0)),
            pl.BlockSpec((1, b_w), lambda b, n, s: (0, 0)),
            pl.BlockSpec(ws.shape, lambda b, n, s: (0, 0, 0)),
            pl.BlockSpec((w, ws.shape[0]), lambda b, n, s: (0, 0)),
        ],
        out_specs=pl.BlockSpec((w, a_q + b_w), lambda b, n, s: (row(b, n), 0)),
        scratch_shapes=[pltpu.VMEM((n_kv, 2 * w, LANES), BF16),
                        pltpu.VMEM((n_kv, 2 * w, LANES), BF16)],
    )
    return pl.pallas_call(
        _mix_kernel,
        out_shape=jax.ShapeDtypeStruct((t, a_q + b_w), BF16),
        grid_spec=grid_spec,
        compiler_params=_params(("arbitrary", "arbitrary")),
        name="mix_even",
    )(sinks, proj, proj, proj, proj, proj, proj, proj, cos_t, sin_t,
      gain2(q_g), gain2(k_g), vnorm_g.reshape(1, b_w), ws, bs.T)


def _hgrn_kernel(q_ref, f_ref, i_ref, gt_ref, lb_ref, og_ref, o_ref,
                 oi_scr, qd_scr, u_scr, d_scr):
    c = C_CHUNK
    nc = q_ref.shape[0] // c
    nv = c // SUBLANES
    lb = lb_ref[...]
    og = og_ref[...]
    t_row = lax.broadcasted_iota(jnp.int32, (c, LANES), 0)
    tt = lax.broadcasted_iota(jnp.int32, (c, c), 0)
    ss = lax.broadcasted_iota(jnp.int32, (c, c), 1)

    def inputs(ci):
        r0 = pl.multiple_of(ci * c, c)
        q = q_ref[pl.ds(r0, c), :].astype(F32)
        fl = f_ref[pl.ds(r0, c), :].astype(F32)
        v = i_ref[pl.ds(r0, c), :]
        f = lb + (1 - lb) * jax.nn.sigmoid(fl)
        k = 1 - f
        b = jnp.log2(f)
        for d in (1, 2, 4, 8, 16, 32):
            b = b + jnp.where(t_row >= d, pltpu.roll(b, d, 0), 0.0)
        return q, f, k, b, v

    def row(b, r, n):
        b3 = b.reshape(nv, SUBLANES, LANES)
        rep = jnp.broadcast_to(b3[r // SUBLANES:r // SUBLANES + 1, r % SUBLANES:r % SUBLANES + 1, :],
                               (n // SUBLANES, SUBLANES, LANES))
        return rep.reshape(n, LANES)

    def half_level(q, k, b, m):
        a_parts, b_parts = [], []
        for blk in range(c // (2 * m)):
            lo, mid, hi = blk * 2 * m, blk * 2 * m + m, (blk + 1) * 2 * m
            zero = jnp.zeros((m, LANES), BF16)
            ref = row(b, mid - 1, m)
            a_parts += [zero, (q[mid:hi] * jnp.exp2(b[mid:hi] - ref)).astype(BF16)]
            b_parts += [(k[lo:mid] * jnp.exp2(ref - b[lo:mid])).astype(BF16), zero]
        return _nt_dot(jnp.concatenate(a_parts, axis=0), jnp.concatenate(b_parts, axis=0))

    def finish(ci, q, k, b, v, scores):
        oi_scr[ci] = jnp.dot(scores.astype(BF16), v, preferred_element_type=F32)
        qd_scr[ci] = (q * jnp.exp2(b)).astype(BF16)
        b_last = b[c - 1:c, :]
        kd = (k * jnp.exp2(b_last - b)).astype(BF16)
        u_scr[ci] = lax.dot_general(v, kd, (((0,), (0,)), ((), ())), preferred_element_type=F32)
        d_scr[ci] = jnp.exp2(b_last)

    def intra_fast(ci, worst):
        q, f, k, b, v = inputs(ci)
        blk = HGRN_BLOCK
        starts = [jnp.zeros((blk, LANES), F32)] + [row(b, s - 1, blk) for s in range(blk, c, blk)]
        base = jnp.concatenate(starts, axis=0)
        s_d = _nt_dot((q * jnp.exp2(b - base)).astype(BF16), (k * jnp.exp2(base - b)).astype(BF16))
        s_16 = half_level(q, k, b, blk)
        s_32 = half_level(q, k, b, 2 * blk)
        scores = jnp.where(((tt // blk) == (ss // blk)) & (ss <= tt), s_d,
                           jnp.where((tt // (2 * blk)) == (ss // (2 * blk)), s_16, s_32))
        finish(ci, q, k, b, v, scores)
        for s in range(0, c, blk):
            worst = jnp.maximum(worst, base[s:s + 1, :] - b[s + blk - 1:s + blk, :])
        return worst

    def intra_safe(ci, carry):
        q, f, k, b, v = inputs(ci)
        scores = jnp.where(tt == ss, jnp.sum(q * k, axis=-1, keepdims=True), 0.0)
        odd = (t_row % 2) == 1
        s_1 = _nt_dot(jnp.where(odd, q * f, 0.0).astype(BF16), jnp.where(odd, 0.0, k).astype(BF16))
        scores = scores + jnp.where((tt // 2) == (ss // 2), s_1, 0.0)
        b3 = b.reshape(nv, SUBLANES, LANES)
        sub = lax.broadcasted_iota(jnp.int32, b3.shape, 1)
        small_refs = {4: jnp.broadcast_to(b3[:, 3:4, :], b3.shape).reshape(c, LANES),
                      2: jnp.where(sub < 4, jnp.broadcast_to(b3[:, 1:2, :], b3.shape),
                                   jnp.broadcast_to(b3[:, 5:6, :], b3.shape)).reshape(c, LANES)}
        for m in (32, 16, 8, 4, 2):
            if m >= SUBLANES:
                s_l = half_level(q, k, b, m)
            else:
                ref = small_refs[m]
                upper = (t_row % (2 * m)) >= m
                s_l = _nt_dot(jnp.where(upper, q * jnp.exp2(b - ref), 0.0).astype(BF16),
                              jnp.where(upper, 0.0, k * jnp.exp2(ref - b)).astype(BF16))
            scores = scores + (s_l if 2 * m == c else jnp.where((tt // (2 * m)) == (ss // (2 * m)), s_l, 0.0))
        finish(ci, q, k, b, v, scores)
        return carry

    worst = lax.fori_loop(0, nc, intra_fast, jnp.zeros((1, LANES), F32), unroll=8)

    @pl.when(jnp.max(worst) > HGRN_MAX_BLOCK_DECAY)
    def _():
        lax.fori_loop(0, nc, intra_safe, 0, unroll=2)

    def inter(ci, state_t):
        r0 = pl.multiple_of(ci * c, c)
        gate = gt_ref[pl.ds(r0, c), :].astype(F32)
        o = oi_scr[ci] + _nt_dot(qd_scr[ci], state_t.astype(BF16))
        y = o * lax.rsqrt(jnp.mean(o * o, axis=-1, keepdims=True) + EPS) * og
        o_ref[pl.ds(r0, c), :] = (y * (gate * jax.nn.sigmoid(gate))).astype(o_ref.dtype)
        return state_t * d_scr[ci] + u_scr[ci]

    lax.fori_loop(0, nc, inter, jnp.zeros((C_VDIM, C_KDIM), F32), unroll=8)


def _hgrn(proj, lb, og, bsz, seq):
    t, n4 = proj.shape
    heads = n4 // (4 * LANES)
    nc = seq // C_CHUNK
    return pl.pallas_call(
        _hgrn_kernel,
        out_shape=jax.ShapeDtypeStruct((t, heads * C_VDIM), BF16),
        grid=(bsz, heads),
        in_specs=[
            pl.BlockSpec((seq, LANES), lambda b, h: (b, h)),
            pl.BlockSpec((seq, LANES), lambda b, h: (b, heads + h)),
            pl.BlockSpec((seq, LANES), lambda b, h: (b, 2 * heads + h)),
            pl.BlockSpec((seq, LANES), lambda b, h: (b, 3 * heads + h)),
            pl.BlockSpec((1, LANES), lambda b, h: (0, h)),
            pl.BlockSpec((1, LANES), lambda b, h: (0, 0)),
        ],
        out_specs=pl.BlockSpec((seq, LANES), lambda b, h: (b, h)),
        scratch_shapes=[pltpu.VMEM((nc, C_CHUNK, C_VDIM), F32),
                        pltpu.VMEM((nc, C_CHUNK, C_KDIM), BF16),
                        pltpu.VMEM((nc, C_VDIM, C_KDIM), F32),
                        pltpu.VMEM((nc, 1, C_KDIM), F32)],
        compiler_params=_params(("arbitrary", "arbitrary")),
        name="hgrn2",
    )(proj, proj, proj, proj, lb.reshape(1, -1), og.reshape(1, LANES))


def _first_max(vals):
    best, idx = vals[0], jnp.zeros(vals[0].shape, jnp.int32)
    for i in range(1, len(vals)):
        better = vals[i] > best
        best = jnp.where(better, vals[i], best)
        idx = jnp.where(better, i, idx)
    return best, idx


def _pick(idx, vals):
    out = vals[-1]
    for i in range(len(vals) - 2, -1, -1):
        out = jnp.where(idx == i, vals[i], out)
    return out


def _router_kernel(x_ref, g_ref, sh_ref, sc_ref, rw_ref, rb_ref,
                   h_ref, eidx_ref, wsel_ref, rank_ref, cnt_ref, carry_scr):
    i = pl.program_id(0)
    tm = x_ref.shape[0]

    @pl.when(i == 0)
    def _():
        carry_scr[...] = jnp.zeros_like(carry_scr)

    h = _norm_mod(x_ref[...], g_ref[...], sh_ref[0], sc_ref[0])
    h_ref[...] = h
    rw = rw_ref[...]
    w_hi = rw.astype(BF16)
    w_lo = (rw - w_hi.astype(F32)).astype(BF16)
    h_hi = h.astype(BF16)
    h_lo = (h - h_hi.astype(F32)).astype(BF16)
    head = _nt_dot(jnp.concatenate([w_hi, w_lo], axis=0), h_hi)
    logits = head[:N_EXPERTS] + head[N_EXPERTS:] + _nt_dot(w_hi, h_lo)
    scores = jax.nn.sigmoid(logits)
    sel = scores + rb_ref[...]
    epg = EXPERTS_PER_GROUP
    sel_rows = [sel[e:e + 1, :] for e in range(N_EXPERTS)]
    sc_rows = [scores[e:e + 1, :] for e in range(N_EXPERTS)]
    gsum = []
    for g in range(N_GROUPS):
        r = sel_rows[g * epg:(g + 1) * epg]
        pair = None
        for a in range(epg):
            for b in range(a + 1, epg):
                pair = r[a] + r[b] if pair is None else jnp.maximum(pair, r[a] + r[b])
        gsum.append(pair)
    _, g_idx = _first_max(gsum)
    cand = [_pick(g_idx, [sel_rows[g * epg + j] for g in range(N_GROUPS)]) for j in range(epg)]
    cand_sc = [_pick(g_idx, [sc_rows[g * epg + j] for g in range(N_GROUPS)]) for j in range(epg)]
    _, i1 = _first_max(cand)
    _, i2 = _first_max([jnp.where(i1 == j, -jnp.inf, cand[j]) for j in range(epg)])
    w1 = _pick(i1, cand_sc)
    w2 = _pick(i2, cand_sc)
    tot = w1 + w2
    e1 = g_idx * epg + i1
    e2 = g_idx * epg + i2
    eidx_ref[0:1, :] = e1
    eidx_ref[1:2, :] = e2
    wsel_ref[0:1, :] = w1 / tot
    wsel_ref[1:2, :] = w2 / tot

    erow = lax.broadcasted_iota(jnp.int32, (N_EXPERTS, tm), 0)
    oh1 = erow == e1
    oh2 = erow == e2
    onehot = jnp.where(oh1 | oh2, 1.0, 0.0)
    si = lax.broadcasted_iota(jnp.int32, (tm, tm), 0)
    ti = lax.broadcasted_iota(jnp.int32, (tm, tm), 1)
    incl = jnp.where(si <= ti, 1.0, 0.0).astype(BF16)
    csum = jnp.dot(onehot.astype(BF16), incl, preferred_element_type=F32)
    carry = carry_scr[...]
    before = csum - onehot + carry[:, 0:1]
    r1 = jnp.sum(jnp.where(oh1, before, 0.0), axis=0, keepdims=True)
    r2 = jnp.sum(jnp.where(oh2, before, 0.0), axis=0, keepdims=True)
    rank_ref[0:1, :] = r1.astype(jnp.int32)
    rank_ref[1:2, :] = r2.astype(jnp.int32)
    new_carry = carry + csum[:, tm - 1:tm]
    carry_scr[...] = new_carry
    cnt_ref[...] = new_carry.astype(jnp.int32)


def _router(x, g, shift, scale, rw_t, rbias, seq):
    t, d = x.shape
    tm = ROW_TILE
    e = rw_t.shape[0]
    row = lambda i: (0, i)
    return pl.pallas_call(
        _router_kernel,
        out_shape=(jax.ShapeDtypeStruct((t, d), F32),
                   jax.ShapeDtypeStruct((TOP_K, t), jnp.int32),
                   jax.ShapeDtypeStruct((TOP_K, t), F32),
                   jax.ShapeDtypeStruct((TOP_K, t), jnp.int32),
                   jax.ShapeDtypeStruct((e, LANES), jnp.int32)),
        grid=(t // tm,),
        in_specs=[
            pl.BlockSpec((tm, d), lambda i: (i, 0)),
            pl.BlockSpec((1, d), lambda i: (0, 0)),
            pl.BlockSpec((1, 1, d), lambda i: (i * tm // seq, 0, 0)),
            pl.BlockSpec((1, 1, d), lambda i: (i * tm // seq, 0, 0)),
            pl.BlockSpec((e, d), lambda i: (0, 0)),
            pl.BlockSpec((e, 1), lambda i: (0, 0)),
        ],
        out_specs=(pl.BlockSpec((tm, d), lambda i: (i, 0)),
                   pl.BlockSpec((TOP_K, tm), row),
                   pl.BlockSpec((TOP_K, tm), row),
                   pl.BlockSpec((TOP_K, tm), row),
                   pl.BlockSpec((e, LANES), lambda i: (0, 0))),
        scratch_shapes=[pltpu.VMEM((e, LANES), F32)],
        compiler_params=_params(("arbitrary",)),
        name="router",
    )(x, g.reshape(1, d), shift, scale, rw_t, rbias.reshape(e, 1))


def _invert_kernel(pos_ref, fill_ref, src_ref, *, n_tok):
    def clear(p, carry):
        src_ref[p] = 0
        return carry

    for e in range(N_EXPERTS + 1):
        lax.fori_loop(fill_ref[2 * e], fill_ref[2 * e + 1], clear, 0)

    for slot in range(TOP_K):
        def place(tok, carry, slot=slot):
            src_ref[pos_ref[slot * n_tok + tok]] = tok
            return carry

        lax.fori_loop(0, n_tok, place, 0, unroll=8)


def _invert_positions(pos_flat, fill, n_rows, n_tok):
    assert pos_flat.shape[0] == TOP_K * n_tok
    return pl.pallas_call(
        functools.partial(_invert_kernel, n_tok=n_tok),
        out_shape=jax.ShapeDtypeStruct((n_rows,), jnp.int32),
        in_specs=[pl.BlockSpec(memory_space=pltpu.SMEM), pl.BlockSpec(memory_space=pltpu.SMEM)],
        out_specs=pl.BlockSpec(memory_space=pltpu.SMEM),
        name="invert_positions",
    )(pos_flat, fill)


def _expert_kernel(te_ref, tv_ref, tf_ref, tn_ref, src_ref, h_hbm, wg_hbm, wu_hbm, wd_hbm, y_ref,
                   xbuf0, xbuf1, sg, su, sd, bg, bu, bd, gsem, wsem, *, layer):
    i = pl.program_id(0)
    xbufs = (xbuf0, xbuf1)
    tm = xbuf0.shape[0]

    def gather(tile, buf):
        for r in range(tm):
            tok = src_ref[tile * tm + r]
            pltpu.make_async_copy(h_hbm.at[pl.ds(tok, 1)], xbufs[buf].at[pl.ds(r, 1)], gsem.at[buf]).start()

    def fetch(e):
        return (pltpu.make_async_copy(wg_hbm.at[layer, e], sg, wsem.at[0]),
                pltpu.make_async_copy(wu_hbm.at[layer, e], su, wsem.at[1]),
                pltpu.make_async_copy(wd_hbm.at[layer, e], sd, wsem.at[2]))

    @pl.when(i == 0)
    def _():
        for c in fetch(te_ref[0]):
            c.start(priority=WEIGHT_DMA_PRIORITY)
        gather(0, 0)

    pending = (i == 0) | (tv_ref[jnp.maximum(i - 1, 0)] > 0)
    for par in range(2):
        @pl.when(pending & (i % 2 == par))
        def _(par=par):
            pltpu.make_async_copy(h_hbm.at[pl.ds(0, tm)], xbufs[par], gsem.at[par]).wait()

    @pl.when(tf_ref[i] > 0)
    def _():
        for c in fetch(te_ref[i]):
            c.wait()
        bg[...] = sg[...].astype(BF16)
        bu[...] = su[...].astype(BF16)
        bd[...] = sd[...].astype(BF16)

        @pl.when(tn_ref[i] >= 0)
        def _():
            for c in fetch(tn_ref[i]):
                c.start(priority=WEIGHT_DMA_PRIORITY)

    for par in range(2):
        @pl.when((tv_ref[i] > 0) & (i % 2 == par))
        def _(par=par):
            gather(i + 1, 1 - par)
            x = xbufs[par][...].astype(BF16)
            a = jnp.dot(x, bg[...], preferred_element_type=F32)
            u = jnp.dot(x, bu[...], preferred_element_type=F32)
            hid = (a * jax.nn.sigmoid(a) * u).astype(BF16)
            y_ref[...] = jnp.dot(hid, bd[...], preferred_element_type=F32)

    @pl.when(tv_ref[i] == 0)
    def _():
        y_ref[...] = jnp.zeros_like(y_ref)


def _experts(tile_expert, tile_valid, tile_first, tile_next, src, h, w_gate, w_up, w_down, layer, n_rows):
    d = h.shape[1]
    f = w_gate.shape[3]
    tm = EXPERT_TILE
    grid_spec = pltpu.PrefetchScalarGridSpec(
        num_scalar_prefetch=5,
        grid=(n_rows // tm,),
        in_specs=[pl.BlockSpec(memory_space=pl.ANY)] * 4,
        out_specs=pl.BlockSpec((tm, d), lambda i, *_: (i, 0)),
        scratch_shapes=[pltpu.VMEM((tm, d), F32), pltpu.VMEM((tm, d), F32),
                        pltpu.VMEM((d, f), F32), pltpu.VMEM((d, f), F32), pltpu.VMEM((f, d), F32),
                        pltpu.VMEM((d, f), BF16), pltpu.VMEM((d, f), BF16), pltpu.VMEM((f, d), BF16),
                        pltpu.SemaphoreType.DMA((2,)), pltpu.SemaphoreType.DMA((3,))],
    )
    return pl.pallas_call(
        functools.partial(_expert_kernel, layer=layer),
        out_shape=jax.ShapeDtypeStruct((n_rows, d), F32),
        grid_spec=grid_spec,
        compiler_params=_params(("arbitrary",)),
        name="experts",
    )(tile_expert, tile_valid, tile_first, tile_next, src, h, w_gate, w_up, w_down)


def _combine_kernel(pos_ref, x_ref, w_ref, g_ref, y_ref, *rest, with_next):
    if with_next:
        ng_ref, nsh_ref, nsc_ref, o_ref, hn_ref, buf0, buf1, sem = rest
    else:
        o_ref, buf0, buf1, sem = rest
    i = pl.program_id(0)
    n = pl.num_programs(0)
    tm = x_ref.shape[0]
    t = n * tm
    bufs = (buf0, buf1)

    def gather(tile, par):
        for r in range(tm):
            for slot in range(TOP_K):
                p = pos_ref[slot * t + tile * tm + r]
                pltpu.make_async_copy(y_ref.at[pl.ds(p, 1)], bufs[par].at[slot, pl.ds(r, 1)],
                                      sem.at[par]).start(priority=slot)

    def combine(par):
        w = w_ref[...]
        y = w[:, 0:1] * bufs[par][0] + w[:, 1:2] * bufs[par][1]
        x_new = x_ref[...] + g_ref[0] * y
        o_ref[...] = x_new
        if with_next:
            hn_ref[...] = _norm_mod(x_new, ng_ref[...], nsh_ref[0], nsc_ref[0]).astype(BF16)

    @pl.when(i == 0)
    def _():
        gather(0, 0)

    for par in range(2):
        @pl.when(i % 2 == par)
        def _(par=par):
            for slot in range(TOP_K):
                pltpu.make_async_copy(y_ref.at[pl.ds(0, tm)], bufs[par].at[slot], sem.at[par]).wait()

        @pl.when((i % 2 == par) & (i + 1 < n))
        def _(par=par):
            gather(i + 1, 1 - par)
            combine(par)

        @pl.when((i % 2 == par) & (i + 1 == n))
        def _(par=par):
            combine(par)


def _combine(pos_flat, x, wsel_t, gate, y, seq, next_norm=None):
    t, d = x.shape
    tm = ROW_TILE
    assert seq % tm == 0 and t % tm == 0
    row = pl.BlockSpec((tm, d), lambda i, p: (i, 0))
    per_batch = pl.BlockSpec((1, 1, d), lambda i, p: (i * tm // seq, 0, 0))
    in_specs = [row, pl.BlockSpec((tm, TOP_K), lambda i, p: (i, 0)), per_batch, pl.BlockSpec(memory_space=pl.ANY)]
    operands = [pos_flat, x, wsel_t, gate, y]
    out_shape = jax.ShapeDtypeStruct((t, d), F32)
    out_specs = row
    if next_norm is not None:
        in_specs += [pl.BlockSpec((1, d), lambda i, p: (0, 0)), per_batch, per_batch]
        operands += [next_norm[0].reshape(1, d), next_norm[1], next_norm[2]]
        out_shape = (out_shape, jax.ShapeDtypeStruct((t, d), BF16))
        out_specs = (row, row)
    grid_spec = pltpu.PrefetchScalarGridSpec(
        num_scalar_prefetch=1,
        grid=(t // tm,),
        in_specs=in_specs,
        out_specs=out_specs,
        scratch_shapes=[pltpu.VMEM((TOP_K, tm, d), F32), pltpu.VMEM((TOP_K, tm, d), F32),
                        pltpu.SemaphoreType.DMA((2,))],
    )
    return pl.pallas_call(
        functools.partial(_combine_kernel, with_next=next_norm is not None),
        out_shape=out_shape,
        grid_spec=grid_spec,
        compiler_params=_params(("arbitrary",)),
        name="combine",
    )(*operands)


def _moe(x, g, shift, scale, gate, rw_t, rbias, w_gate, w_up, w_down, layer, seq, next_norm):
    t, d = x.shape
    te = EXPERT_TILE
    n_rows = TOP_K * t + N_EXPERTS * te
    n_tiles = n_rows // te
    h, eidx, wsel, rank, cnt = _router(x, g, shift, scale, rw_t, rbias, seq)
    counts = cnt[:, 0]
    padded = ((counts + te - 1) // te) * te
    ends = jnp.cumsum(padded)
    starts = ends - padded
    pos = rank
    for e in range(N_EXPERTS):
        pos = pos + jnp.where(eidx == e, starts[e], 0)
    pos_flat = pos.reshape(-1)
    tile_start = jnp.arange(n_tiles, dtype=jnp.int32) * te
    tile_expert = jnp.minimum(jnp.sum(tile_start[:, None] >= ends[None, :], axis=1), N_EXPERTS - 1).astype(jnp.int32)
    tile_valid = (tile_start < ends[-1]).astype(jnp.int32)
    tile_expert = jnp.where(tile_valid > 0, tile_expert, tile_expert[jnp.maximum(ends[-1] // te - 1, 0)])
    prev_expert = jnp.concatenate([jnp.full((1,), -1, jnp.int32), tile_expert[:-1]])
    tile_first = ((tile_valid > 0) & (tile_expert != prev_expert)).astype(jnp.int32)
    later = (tile_expert[None, :] > tile_expert[:, None]) & (tile_valid[None, :] > 0)
    tile_next = jnp.min(jnp.where(later, tile_expert[None, :], N_EXPERTS), axis=1)
    tile_next = jnp.where(tile_next < N_EXPERTS, tile_next, -1).astype(jnp.int32)
    fill = jnp.stack([jnp.concatenate([starts + counts, ends[-1:]]),
                      jnp.concatenate([ends, jnp.full((1,), n_rows, ends.dtype)])], axis=1).reshape(-1).astype(jnp.int32)
    src = _invert_positions(pos_flat, fill, n_rows, t)
    y = _experts(tile_expert, tile_valid, tile_first, tile_next, src, h, w_gate, w_up, w_down, layer, n_rows)
    return _combine(pos_flat, x, wsel.T, gate, y, seq, next_norm)


def kernel(x, c, positions, ada_w, ada_b, norm1_g, norm2_g, ab_w_in, q_norm_g, k_norm_g, sinks, gm_vnorm_g, gm_ws, gm_b, ab_w_out, c_w_in, c_lower_bounds, c_onorm_g, c_w_out, router_w, router_bias, moe_w_gate, moe_w_up, moe_w_down):
    bsz, seq, d = x.shape
    depth = ada_w.shape[0]
    t = bsz * seq
    xt = x.reshape(t, d)

    c_pad = jnp.concatenate([c, jnp.zeros((8 - bsz % 8, d), F32)], axis=0) if bsz % 8 else c
    mod = _ada_mod(c_pad, ada_w, ada_b)[:, :bsz]
    mod = mod.reshape(depth, bsz, 6, 1, d)
    cos_t, sin_t = _rope_tables(positions)

    sm = jax.nn.softmax(c_lower_bounds.astype(F32), axis=0)
    lower_bounds = jnp.cumsum(sm, axis=0) - sm[0:1]
    rw_t = router_w.T
    ab_in, ab_out, c_in, c_out = (w.astype(BF16) for w in (ab_w_in, ab_w_out, c_w_in, c_w_out))

    h1 = None
    for l in range(depth):
        sh1, sc1, g1, sh2, sc2, g2 = (mod[l, :, k] for k in range(6))
        w_in, idx = (ab_in, l // 2) if l % 2 == 0 else (c_in, l // 2)
        proj = _in_proj(xt, norm1_g[l], sh1, sc1, w_in, idx, seq) if h1 is None else _in_proj_normed(h1, w_in, idx)
        if l % 2 == 0:
            i = l // 2
            mix = _mix_even(proj, cos_t, sin_t, q_norm_g[i], k_norm_g[i], sinks[i],
                            gm_vnorm_g[i], gm_ws[i], gm_b[i], bsz, seq)
            xt = _out_proj(mix, ab_out, i, xt, g1, seq)
        else:
            j = l // 2
            o = _hgrn(proj, lower_bounds[l], c_onorm_g[j], bsz, seq)
            xt = _out_proj(o, c_out, j, xt, g1, seq)
        next_norm = (norm1_g[l + 1], mod[l + 1, :, 0], mod[l + 1, :, 1]) if l + 1 < depth else None
        out = _moe(xt, norm2_g[l], sh2, sc2, g2, rw_t, router_bias,
                   moe_w_gate, moe_w_up, moe_w_down, l, seq, next_norm)
        xt, h1 = out if next_norm is not None else (out, None)
    return xt.reshape(bsz, seq, d)
```

```python
import functools

import jax
import jax.numpy as jnp
from jax import lax
from jax.experimental import pallas as pl
from jax.experimental.pallas import tpu as pltpu

F32 = jnp.float32
BF16 = jnp.bfloat16

EPS = 1e-6
HEAD_DIM = 64
KV_REP = 4
WINDOW = 128
ROPE_THETA = 10000.0
B_GROUP_DIM = 128
B_CHUNK = 128
C_KDIM = 128
C_VDIM = 128
C_CHUNK = 64
N_GROUPS = 4
EXPERTS_PER_GROUP = 4
N_EXPERTS = N_GROUPS * EXPERTS_PER_GROUP
TOP_K = 2

LANES = 128
SUBLANES = 8
VMEM_LIMIT = 56 * 1024 * 1024

ROW_TILE = 512
PROJ_ROW_TILE = 1024
PROJ_COL_TILE = 2048
EXPERT_TILE = 256
WEIGHT_DMA_PRIORITY = 1
HGRN_BLOCK = 32
HGRN_MAX_BLOCK_DECAY = 100.0


def _nt_dot(a, b):
    return lax.dot_general(a, b, (((1,), (1,)), ((), ())), preferred_element_type=F32)


def _params(sem):
    return pltpu.CompilerParams(dimension_semantics=sem, vmem_limit_bytes=VMEM_LIMIT)


def _ada_kernel(c_ref, w_ref, b_ref, o_ref):
    c = c_ref[...]
    cond = c * jax.nn.sigmoid(c)
    o_ref[0] = jnp.dot(cond, w_ref[0], preferred_element_type=F32) + b_ref[0]


def _ada_mod(c_pad, ada_w, ada_b):
    depth, d, n = ada_w.shape
    tn = 1536
    rows = c_pad.shape[0]
    return pl.pallas_call(
        _ada_kernel,
        out_shape=jax.ShapeDtypeStruct((depth, rows, n), F32),
        grid=(depth, n // tn),
        in_specs=[
            pl.BlockSpec((rows, d), lambda l, j: (0, 0)),
            pl.BlockSpec((1, d, tn), lambda l, j: (l, 0, j)),
            pl.BlockSpec((1, 1, tn), lambda l, j: (l, 0, j)),
        ],
        out_specs=pl.BlockSpec((1, rows, tn), lambda l, j: (l, 0, j)),
        compiler_params=_params(("arbitrary", "arbitrary")),
        name="ada_mod",
    )(c_pad, ada_w, ada_b.reshape(depth, 1, n))


def _rope_table_kernel(pos_ref, freq_ref, sign_ref, cos_ref, sin_ref):
    ang = pos_ref[...].astype(F32) * freq_ref[...]
    cos_ref[...] = jnp.cos(ang)
    sin_ref[...] = jnp.sin(ang) * sign_ref[...]


def _rope_tables(positions):
    t = positions.size
    half = HEAD_DIM // 2
    inv_freq = ROPE_THETA ** (-jnp.arange(half, dtype=F32) / half)
    freq = jnp.tile(inv_freq, LANES // half).reshape(1, LANES)
    sign = jnp.tile(jnp.concatenate([-jnp.ones((half,), F32), jnp.ones((half,), F32)]),
                    LANES // HEAD_DIM).reshape(1, LANES)
    tm = min(1024, t)
    return pl.pallas_call(
        _rope_table_kernel,
        out_shape=(jax.ShapeDtypeStruct((t, LANES), F32), jax.ShapeDtypeStruct((t, LANES), F32)),
        grid=(t // tm,),
        in_specs=[
            pl.BlockSpec((tm, 1), lambda i: (i, 0)),
            pl.BlockSpec((1, LANES), lambda i: (0, 0)),
            pl.BlockSpec((1, LANES), lambda i: (0, 0)),
        ],
        out_specs=(pl.BlockSpec((tm, LANES), lambda i: (i, 0)),
                   pl.BlockSpec((tm, LANES), lambda i: (i, 0))),
        compiler_params=_params(("arbitrary",)),
        name="rope_tables",
    )(positions.reshape(t, 1), freq, sign)


def _norm_mod(x, g, shift, scale):
    y = x * lax.rsqrt(jnp.mean(x * x, axis=-1, keepdims=True) + EPS)
    return (y * g) * (1 + scale) + shift


def _in_proj_kernel(x_ref, g_ref, sh_ref, sc_ref, w_ref, o_ref, h_scr):
    @pl.when(pl.program_id(1) == 0)
    def _():
        h_scr[...] = _norm_mod(x_ref[...], g_ref[...], sh_ref[0], sc_ref[0]).astype(BF16)

    o_ref[...] = jnp.dot(h_scr[...], w_ref[0], preferred_element_type=F32).astype(o_ref.dtype)


def _matmul_kernel(h_ref, w_ref, o_ref):
    o_ref[...] = jnp.dot(h_ref[...], w_ref[0], preferred_element_type=F32).astype(o_ref.dtype)


def _in_proj_normed(h, w, layer):
    t, d = h.shape
    n = w.shape[2]
    tm = PROJ_ROW_TILE
    tn = max(c for c in range(LANES, PROJ_COL_TILE + 1, LANES) if n % c == 0)
    assert t % tm == 0
    return pl.pallas_call(
        _matmul_kernel,
        out_shape=jax.ShapeDtypeStruct((t, n), BF16),
        grid=(t // tm, n // tn),
        in_specs=[
            pl.BlockSpec((tm, d), lambda i, j: (i, 0)),
            pl.BlockSpec((1, d, tn), lambda i, j: (layer, 0, j)),
        ],
        out_specs=pl.BlockSpec((tm, tn), lambda i, j: (i, j)),
        compiler_params=_params(("arbitrary", "arbitrary")),
        name="in_proj",
    )(h, w)


def _in_proj(x, g, shift, scale, w, layer, seq):
    t, d = x.shape
    n = w.shape[2]
    tm = PROJ_ROW_TILE
    tn = max(c for c in range(LANES, PROJ_COL_TILE + 1, LANES) if n % c == 0)
    assert seq % tm == 0 and t % tm == 0
    return pl.pallas_call(
        _in_proj_kernel,
        out_shape=jax.ShapeDtypeStruct((t, n), BF16),
        grid=(t // tm, n // tn),
        in_specs=[
            pl.BlockSpec((tm, d), lambda i, j: (i, 0)),
            pl.BlockSpec((1, d), lambda i, j: (0, 0)),
            pl.BlockSpec((1, 1, d), lambda i, j: (i * tm // seq, 0, 0)),
            pl.BlockSpec((1, 1, d), lambda i, j: (i * tm // seq, 0, 0)),
            pl.BlockSpec((1, d, tn), lambda i, j: (layer, 0, j)),
        ],
        out_specs=pl.BlockSpec((tm, tn), lambda i, j: (i, j)),
        scratch_shapes=[pltpu.VMEM((tm, d), BF16)],
        compiler_params=_params(("arbitrary", "arbitrary")),
        name="in_proj",
    )(x, g.reshape(1, d), shift, scale, w)


def _out_proj_kernel(a_ref, w_ref, x_ref, g_ref, o_ref):
    y = jnp.dot(a_ref[...], w_ref[0], preferred_element_type=F32)
    o_ref[...] = x_ref[...] + g_ref[0] * y


def _out_proj(a, w, layer, x, gate, seq):
    t, k = a.shape
    n = w.shape[2]
    tm = PROJ_ROW_TILE
    assert seq % tm == 0 and t % tm == 0
    return pl.pallas_call(
        _out_proj_kernel,
        out_shape=jax.ShapeDtypeStruct((t, n), F32),
        grid=(t // tm,),
        in_specs=[
            pl.BlockSpec((tm, k), lambda i: (i, 0)),
            pl.BlockSpec((1, k, n), lambda i: (layer, 0, 0), pipeline_mode=pl.Buffered(1)),
            pl.BlockSpec((tm, n), lambda i: (i, 0)),
            pl.BlockSpec((1, 1, n), lambda i: (i * tm // seq, 0, 0)),
        ],
        out_specs=pl.BlockSpec((tm, n), lambda i: (i, 0)),
        compiler_params=_params(("arbitrary",)),
        name="out_proj",
    )(a, w, x, gate)


def _segment_mean_matrix():
    r = lax.broadcasted_iota(jnp.int32, (LANES, LANES), 0) // HEAD_DIM
    c = lax.broadcasted_iota(jnp.int32, (LANES, LANES), 1) // HEAD_DIM
    return jnp.where(r == c, 1.0 / HEAD_DIM, 0.0).astype(F32)


def _head_norm_rope(x, gain, cos, sin, seg_mean, low_half):
    ms = jnp.dot(x * x, seg_mean, preferred_element_type=F32)
    y = x * lax.rsqrt(ms + EPS) * gain
    partner = jnp.where(low_half, pltpu.roll(y, LANES - HEAD_DIM // 2, 1),
                        pltpu.roll(y, HEAD_DIM // 2, 1))
    return y * cos + partner * sin


def _gelu_tanh(x):
    return 0.5 * x * (1.0 + jnp.tanh(0.7978845608028654 * (x + 0.044715 * (x * x * x))))


def _mix_kernel(sinks_ref,
                q0_ref, q1_ref, kv_ref, u0_ref, u1_ref, v0_ref, v1_ref,
                cos_ref, sin_ref, qg_ref, kg_ref, vng_ref, ws_ref, bs_ref,
                o_ref, kd_scr, vd_scr):
    n = pl.program_id(1)
    w = WINDOW
    n_kv = kd_scr.shape[0]

    @pl.when(n == 0)
    def _():
        kd_scr[...] = jnp.zeros_like(kd_scr)
        vd_scr[...] = jnp.zeros_like(vd_scr)

    lane = lax.broadcasted_iota(jnp.int32, (w, LANES), 1)
    low_half = (lane % HEAD_DIM) < (HEAD_DIM // 2)
    first_head = lane < HEAD_DIM
    seg_mean = _segment_mean_matrix()
    cos = cos_ref[...]
    sin = sin_ref[...]

    kv = kv_ref[...].astype(F32)
    for c in range(n_kv // 2):
        kc = _head_norm_rope(kv[:, c * LANES:(c + 1) * LANES], kg_ref[...], cos, sin, seg_mean, low_half)
        vc = kv[:, (n_kv // 2 + c) * LANES:(n_kv // 2 + c + 1) * LANES]
        kc_sw = pltpu.roll(kc, HEAD_DIM, 1)
        vc_sw = pltpu.roll(vc, HEAD_DIM, 1)
        kd_scr[2 * c, w:, :] = jnp.where(first_head, kc, kc_sw).astype(BF16)
        kd_scr[2 * c + 1, w:, :] = jnp.where(first_head, kc_sw, kc).astype(BF16)
        vd_scr[2 * c, w:, :] = jnp.where(first_head, vc, vc_sw).astype(BF16)
        vd_scr[2 * c + 1, w:, :] = jnp.where(first_head, vc_sw, vc).astype(BF16)

    qi = lax.broadcasted_iota(jnp.int32, (w, 2 * w), 0)
    kj = lax.broadcasted_iota(jnp.int32, (w, 2 * w), 1)
    diff = qi + w - kj
    band = (diff >= 0) & (diff < w) & ((kj >= w) | (n > 0))
    band4 = jnp.concatenate([band] * KV_REP, axis=0)

    scale = HEAD_DIM ** -0.5
    ones_kv = jnp.ones((2 * w, LANES), BF16)
    q_refs = (q0_ref, q1_ref)
    chunks_per_ref = q0_ref.shape[1] // LANES
    for g in range(n_kv):
        rows = []
        for cc in range(2):
            c = 2 * g + cc
            qc = q_refs[c // chunks_per_ref][:, (c % chunks_per_ref) * LANES:(c % chunks_per_ref + 1) * LANES]
            qc = _head_norm_rope(qc.astype(F32), qg_ref[...], cos, sin, seg_mean, low_half)
            rows.append(jnp.where(first_head, qc, 0.0))
            rows.append(jnp.where(first_head, 0.0, qc))
        qm = jnp.concatenate(rows, axis=0).astype(BF16)
        s = lax.dot_general(qm, kd_scr[g], (((1,), (1,)), ((), ())),
                            preferred_element_type=F32) * scale
        s = jnp.where(band4, s, -jnp.inf)
        sink = jnp.concatenate(
            [jnp.full((w, LANES), sinks_ref[KV_REP * g + r], F32) for r in range(KV_REP)], axis=0)
        m = jnp.maximum(jnp.broadcast_to(jnp.max(s, axis=-1, keepdims=True), (KV_REP * w, LANES)), sink)
        p = jnp.exp(s - jnp.concatenate([m, m], axis=1)).astype(BF16)
        total = jnp.dot(p, ones_kv, preferred_element_type=F32) + jnp.exp(sink - m)
        o = jnp.dot(p, vd_scr[g], preferred_element_type=F32) / total
        for cc in range(2):
            oc = jnp.where(first_head, o[(2 * cc) * w:(2 * cc + 1) * w], o[(2 * cc + 1) * w:(2 * cc + 2) * w])
            c = 2 * g + cc
            o_ref[:, c * LANES:(c + 1) * LANES] = oc.astype(o_ref.dtype)

    kd_scr[:, :w, :] = kd_scr[:, w:, :]
    vd_scr[:, :w, :] = vd_scr[:, w:, :]

    a_q = 2 * q0_ref.shape[1]
    u = _gelu_tanh(jnp.concatenate([u0_ref[...], u1_ref[...]], axis=1).astype(F32))
    v = _gelu_tanh(jnp.concatenate([v0_ref[...], v1_ref[...]], axis=1).astype(F32))
    mu = jnp.mean(v, axis=-1, keepdims=True)
    vc = v - mu
    vn = (vc * lax.rsqrt(jnp.mean(vc * vc, axis=-1, keepdims=True) + EPS) * vng_ref[...]).astype(BF16)
    ti = lax.broadcasted_iota(jnp.int32, (w, w), 0)
    si = lax.broadcasted_iota(jnp.int32, (w, w), 1)
    causal = si <= ti
    bs = bs_ref[...]
    for g in range(ws_ref.shape[0]):
        wg = jnp.where(causal, ws_ref[g], 0.0).astype(BF16)
        sv = jnp.dot(wg, vn[:, g * LANES:(g + 1) * LANES], preferred_element_type=F32) + bs[:, g:g + 1]
        o_ref[:, a_q + g * LANES:a_q + (g + 1) * LANES] = (u[:, g * LANES:(g + 1) * LANES] * sv).astype(o_ref.dtype)


def _mix_even(proj, cos_t, sin_t, q_g, k_g, sinks, vnorm_g, ws, bs, bsz, seq):
    t = proj.shape[0]
    w = WINDOW
    nb = seq // w
    heads = sinks.shape[0]
    n_kv = heads // KV_REP
    a_q = heads * HEAD_DIM
    a_kv = n_kv * HEAD_DIM
    b_w = vnorm_g.shape[0]
    cw = 512
    assert a_q == 2 * cw and 2 * a_kv == cw and b_w == 2 * cw
    row = lambda b, n: b * nb + n
    gain2 = lambda g: jnp.tile(g, LANES // HEAD_DIM).reshape(1, LANES)
    col = lambda j: pl.BlockSpec((w, cw), lambda b, n, s: (row(b, n), j))
    grid_spec = pltpu.PrefetchScalarGridSpec(
        num_scalar_prefetch=1,
        grid=(bsz, nb),
        in_specs=[
            col(0), col(1), col(2), col(3), col(4), col(5), col(6),
            pl.BlockSpec((w, LANES), lambda b, n, s: (row(b, n), 0)),
            pl.BlockSpec((w, LANES), lambda b, n, s: (row(b, n), 0)),
            pl.BlockSpec((1, LANES), lambda b, n, s: (0, 0)),
            pl.BlockSpec((1, LANES), lambda b, n, s: (0, 0)),
            pl.BlockSpec((1, b_w), lambda b, n, s: (0, 0)),
            pl.BlockSpec(ws.shape, lambda b, n, s: (0, 0, 0)),
            pl.BlockSpec((w, ws.shape[0]), lambda b, n, s: (0, 0)),
        ],
        out_specs=pl.BlockSpec((w, a_q + b_w), lambda b, n, s: (row(b, n), 0)),
        scratch_shapes=[pltpu.VMEM((n_kv, 2 * w, LANES), BF16),
                        pltpu.VMEM((n_kv, 2 * w, LANES), BF16)],
    )
    return pl.pallas_call(
        _mix_kernel,
        out_shape=jax.ShapeDtypeStruct((t, a_q + b_w), BF16),
        grid_spec=grid_spec,
        compiler_params=_params(("arbitrary", "arbitrary")),
        name="mix_even",
    )(sinks, proj, proj, proj, proj, proj, proj, proj, cos_t, sin_t,
      gain2(q_g), gain2(k_g), vnorm_g.reshape(1, b_w), ws, bs.T)


def _hgrn_kernel(q_ref, f_ref, i_ref, gt_ref, lb_ref, og_ref, o_ref,
                 oi_scr, qd_scr, u_scr, d_scr):
    c = C_CHUNK
    nc = q_ref.shape[0] // c
    nv = c // SUBLANES
    lb = lb_ref[...]
    og = og_ref[...]
    t_row = lax.broadcasted_iota(jnp.int32, (c, LANES), 0)
    tt = lax.broadcasted_iota(jnp.int32, (c, c), 0)
    ss = lax.broadcasted_iota(jnp.int32, (c, c), 1)

    def inputs(ci):
        r0 = pl.multiple_of(ci * c, c)
        q = q_ref[pl.ds(r0, c), :].astype(F32)
        fl = f_ref[pl.ds(r0, c), :].astype(F32)
        v = i_ref[pl.ds(r0, c), :]
        f = lb + (1 - lb) * jax.nn.sigmoid(fl)
        k = 1 - f
        b = jnp.log2(f)
        for d in (1, 2, 4, 8, 16, 32):
            b = b + jnp.where(t_row >= d, pltpu.roll(b, d, 0), 0.0)
        return q, f, k, b, v

    def row(b, r, n):
        b3 = b.reshape(nv, SUBLANES, LANES)
        rep = jnp.broadcast_to(b3[r // SUBLANES:r // SUBLANES + 1, r % SUBLANES:r % SUBLANES + 1, :],
                               (n // SUBLANES, SUBLANES, LANES))
        return rep.reshape(n, LANES)

    def half_level(q, k, b, m):
        a_parts, b_parts = [], []
        for blk in range(c // (2 * m)):
            lo, mid, hi = blk * 2 * m, blk * 2 * m + m, (blk + 1) * 2 * m
            zero = jnp.zeros((m, LANES), BF16)
            ref = row(b, mid - 1, m)
            a_parts += [zero, (q[mid:hi] * jnp.exp2(b[mid:hi] - ref)).astype(BF16)]
            b_parts += [(k[lo:mid] * jnp.exp2(ref - b[lo:mid])).astype(BF16), zero]
        return _nt_dot(jnp.concatenate(a_parts, axis=0), jnp.concatenate(b_parts, axis=0))

    def finish(ci, q, k, b, v, scores):
        oi_scr[ci] = jnp.dot(scores.astype(BF16), v, preferred_element_type=F32)
        qd_scr[ci] = (q * jnp.exp2(b)).astype(BF16)
        b_last = b[c - 1:c, :]
        kd = (k * jnp.exp2(b_last - b)).astype(BF16)
        u_scr[ci] = lax.dot_general(v, kd, (((0,), (0,)), ((), ())), preferred_element_type=F32)
        d_scr[ci] = jnp.exp2(b_last)

    def intra_fast(ci, worst):
        q, f, k, b, v = inputs(ci)
        blk = HGRN_BLOCK
        starts = [jnp.zeros((blk, LANES), F32)] + [row(b, s - 1, blk) for s in range(blk, c, blk)]
        base = jnp.concatenate(starts, axis=0)
        s_d = _nt_dot((q * jnp.exp2(b - base)).astype(BF16), (k * jnp.exp2(base - b)).astype(BF16))
        scores = None
        m = c // 2
        while m >= blk:
            s_l = half_level(q, k, b, m)
            scores = s_l if scores is None else jnp.where((tt // (2 * m)) == (ss // (2 * m)), s_l, scores)
            m //= 2
        scores = jnp.where(((tt // blk) == (ss // blk)) & (ss <= tt), s_d, scores)
        finish(ci, q, k, b, v, scores)
        for s in range(0, c, blk):
            worst = jnp.maximum(worst, base[s:s + 1, :] - b[s + blk - 1:s + blk, :])
        return worst

    def intra_safe(ci, carry):
        q, f, k, b, v = inputs(ci)
        scores = jnp.where(tt == ss, jnp.sum(q * k, axis=-1, keepdims=True), 0.0)
        odd = (t_row % 2) == 1
        s_1 = _nt_dot(jnp.where(odd, q * f, 0.0).astype(BF16), jnp.where(odd, 0.0, k).astype(BF16))
        scores = scores + jnp.where((tt // 2) == (ss // 2), s_1, 0.0)
        b3 = b.reshape(nv, SUBLANES, LANES)
        sub = lax.broadcasted_iota(jnp.int32, b3.shape, 1)
        small_refs = {4: jnp.broadcast_to(b3[:, 3:4, :], b3.shape).reshape(c, LANES),
                      2: jnp.where(sub < 4, jnp.broadcast_to(b3[:, 1:2, :], b3.shape),
                                   jnp.broadcast_to(b3[:, 5:6, :], b3.shape)).reshape(c, LANES)}
        for m in (32, 16, 8, 4, 2):
            if m >= SUBLANES:
                s_l = half_level(q, k, b, m)
            else:
                ref = small_refs[m]
                upper = (t_row % (2 * m)) >= m
                s_l = _nt_dot(jnp.where(upper, q * jnp.exp2(b - ref), 0.0).astype(BF16),
                              jnp.where(upper, 0.0, k * jnp.exp2(ref - b)).astype(BF16))
            scores = scores + (s_l if 2 * m == c else jnp.where((tt // (2 * m)) == (ss // (2 * m)), s_l, 0.0))
        finish(ci, q, k, b, v, scores)
        return carry

    worst = lax.fori_loop(0, nc, intra_fast, jnp.zeros((1, LANES), F32), unroll=8)

    @pl.when(jnp.max(worst) > HGRN_MAX_BLOCK_DECAY)
    def _():
        lax.fori_loop(0, nc, intra_safe, 0, unroll=2)

    def inter(ci, state_t):
        r0 = pl.multiple_of(ci * c, c)
        gate = gt_ref[pl.ds(r0, c), :].astype(F32)
        o = oi_scr[ci] + _nt_dot(qd_scr[ci], state_t.astype(BF16))
        y = o * lax.rsqrt(jnp.mean(o * o, axis=-1, keepdims=True) + EPS) * og
        o_ref[pl.ds(r0, c), :] = (y * (gate * jax.nn.sigmoid(gate))).astype(o_ref.dtype)
        return state_t * d_scr[ci] + u_scr[ci]

    lax.fori_loop(0, nc, inter, jnp.zeros((C_VDIM, C_KDIM), F32), unroll=8)


def _hgrn(proj, lb, og, bsz, seq):
    t, n4 = proj.shape
    heads = n4 // (4 * LANES)
    nc = seq // C_CHUNK
    return pl.pallas_call(
        _hgrn_kernel,
        out_shape=jax.ShapeDtypeStruct((t, heads * C_VDIM), BF16),
        grid=(bsz, heads),
        in_specs=[
            pl.BlockSpec((seq, LANES), lambda b, h: (b, h)),
            pl.BlockSpec((seq, LANES), lambda b, h: (b, heads + h)),
            pl.BlockSpec((seq, LANES), lambda b, h: (b, 2 * heads + h)),
            pl.BlockSpec((seq, LANES), lambda b, h: (b, 3 * heads + h)),
            pl.BlockSpec((1, LANES), lambda b, h: (0, h)),
            pl.BlockSpec((1, LANES), lambda b, h: (0, 0)),
        ],
        out_specs=pl.BlockSpec((seq, LANES), lambda b, h: (b, h)),
        scratch_shapes=[pltpu.VMEM((nc, C_CHUNK, C_VDIM), F32),
                        pltpu.VMEM((nc, C_CHUNK, C_KDIM), BF16),
                        pltpu.VMEM((nc, C_VDIM, C_KDIM), F32),
                        pltpu.VMEM((nc, 1, C_KDIM), F32)],
        compiler_params=_params(("arbitrary", "arbitrary")),
        name="hgrn2",
    )(proj, proj, proj, proj, lb.reshape(1, -1), og.reshape(1, LANES))


def _first_max(vals):
    best, idx = vals[0], jnp.zeros(vals[0].shape, jnp.int32)
    for i in range(1, len(vals)):
        better = vals[i] > best
        best = jnp.where(better, vals[i], best)
        idx = jnp.where(better, i, idx)
    return best, idx


def _pick(idx, vals):
    out = vals[-1]
    for i in range(len(vals) - 2, -1, -1):
        out = jnp.where(idx == i, vals[i], out)
    return out


def _router_kernel(x_ref, g_ref, sh_ref, sc_ref, rw_ref, rb_ref,
                   h_ref, eidx_ref, wsel_ref, rank_ref, cnt_ref, carry_scr):
    i = pl.program_id(0)
    tm = x_ref.shape[0]

    @pl.when(i == 0)
    def _():
        carry_scr[...] = jnp.zeros_like(carry_scr)

    h = _norm_mod(x_ref[...], g_ref[...], sh_ref[0], sc_ref[0])
    h_ref[...] = h
    rw = rw_ref[...]
    w_hi = rw.astype(BF16)
    w_lo = (rw - w_hi.astype(F32)).astype(BF16)
    h_hi = h.astype(BF16)
    h_lo = (h - h_hi.astype(F32)).astype(BF16)
    head = _nt_dot(jnp.concatenate([w_hi, w_lo], axis=0), h_hi)
    logits = head[:N_EXPERTS] + head[N_EXPERTS:] + _nt_dot(w_hi, h_lo)
    scores = jax.nn.sigmoid(logits)
    sel = scores + rb_ref[...]
    epg = EXPERTS_PER_GROUP
    sel_rows = [sel[e:e + 1, :] for e in range(N_EXPERTS)]
    sc_rows = [scores[e:e + 1, :] for e in range(N_EXPERTS)]
    gsum = []
    for g in range(N_GROUPS):
        r = sel_rows[g * epg:(g + 1) * epg]
        pair = None
        for a in range(epg):
            for b in range(a + 1, epg):
                pair = r[a] + r[b] if pair is None else jnp.maximum(pair, r[a] + r[b])
        gsum.append(pair)
    _, g_idx = _first_max(gsum)
    cand = [_pick(g_idx, [sel_rows[g * epg + j] for g in range(N_GROUPS)]) for j in range(epg)]
    cand_sc = [_pick(g_idx, [sc_rows[g * epg + j] for g in range(N_GROUPS)]) for j in range(epg)]
    _, i1 = _first_max(cand)
    _, i2 = _first_max([jnp.where(i1 == j, -jnp.inf, cand[j]) for j in range(epg)])
    w1 = _pick(i1, cand_sc)
    w2 = _pick(i2, cand_sc)
    tot = w1 + w2
    e1 = g_idx * epg + i1
    e2 = g_idx * epg + i2
    eidx_ref[0:1, :] = e1
    eidx_ref[1:2, :] = e2
    wsel_ref[0:1, :] = w1 / tot
    wsel_ref[1:2, :] = w2 / tot

    erow = lax.broadcasted_iota(jnp.int32, (N_EXPERTS, tm), 0)
    oh1 = erow == e1
    oh2 = erow == e2
    onehot = jnp.where(oh1 | oh2, 1.0, 0.0)
    si = lax.broadcasted_iota(jnp.int32, (tm, tm), 0)
    ti = lax.broadcasted_iota(jnp.int32, (tm, tm), 1)
    incl = jnp.where(si <= ti, 1.0, 0.0).astype(BF16)
    csum = jnp.dot(onehot.astype(BF16), incl, preferred_element_type=F32)
    carry = carry_scr[...]
    before = csum - onehot + carry[:, 0:1]
    r1 = jnp.sum(jnp.where(oh1, before, 0.0), axis=0, keepdims=True)
    r2 = jnp.sum(jnp.where(oh2, before, 0.0), axis=0, keepdims=True)
    rank_ref[0:1, :] = r1.astype(jnp.int32)
    rank_ref[1:2, :] = r2.astype(jnp.int32)
    new_carry = carry + csum[:, tm - 1:tm]
    carry_scr[...] = new_carry
    cnt_ref[...] = new_carry.astype(jnp.int32)


def _router(x, g, shift, scale, rw_t, rbias, seq):
    t, d = x.shape
    tm = ROW_TILE
    e = rw_t.shape[0]
    row = lambda i: (0, i)
    return pl.pallas_call(
        _router_kernel,
        out_shape=(jax.ShapeDtypeStruct((t, d), F32),
                   jax.ShapeDtypeStruct((TOP_K, t), jnp.int32),
                   jax.ShapeDtypeStruct((TOP_K, t), F32),
                   jax.ShapeDtypeStruct((TOP_K, t), jnp.int32),
                   jax.ShapeDtypeStruct((e, LANES), jnp.int32)),
        grid=(t // tm,),
        in_specs=[
            pl.BlockSpec((tm, d), lambda i: (i, 0)),
            pl.BlockSpec((1, d), lambda i: (0, 0)),
            pl.BlockSpec((1, 1, d), lambda i: (i * tm // seq, 0, 0)),
            pl.BlockSpec((1, 1, d), lambda i: (i * tm // seq, 0, 0)),
            pl.BlockSpec((e, d), lambda i: (0, 0)),
            pl.BlockSpec((e, 1), lambda i: (0, 0)),
        ],
        out_specs=(pl.BlockSpec((tm, d), lambda i: (i, 0)),
                   pl.BlockSpec((TOP_K, tm), row),
                   pl.BlockSpec((TOP_K, tm), row),
                   pl.BlockSpec((TOP_K, tm), row),
                   pl.BlockSpec((e, LANES), lambda i: (0, 0))),
        scratch_shapes=[pltpu.VMEM((e, LANES), F32)],
        compiler_params=_params(("arbitrary",)),
        name="router",
    )(x, g.reshape(1, d), shift, scale, rw_t, rbias.reshape(e, 1))


def _invert_kernel(pos_ref, fill_ref, src_ref, *, n_tok):
    def clear(p, carry):
        src_ref[p] = 0
        return carry

    for e in range(N_EXPERTS + 1):
        lax.fori_loop(fill_ref[2 * e], fill_ref[2 * e + 1], clear, 0)

    for slot in range(TOP_K):
        def place(tok, carry, slot=slot):
            src_ref[pos_ref[slot * n_tok + tok]] = tok
            return carry

        lax.fori_loop(0, n_tok, place, 0, unroll=8)


def _invert_positions(pos_flat, fill, n_rows, n_tok):
    assert pos_flat.shape[0] == TOP_K * n_tok
    return pl.pallas_call(
        functools.partial(_invert_kernel, n_tok=n_tok),
        out_shape=jax.ShapeDtypeStruct((n_rows,), jnp.int32),
        in_specs=[pl.BlockSpec(memory_space=pltpu.SMEM), pl.BlockSpec(memory_space=pltpu.SMEM)],
        out_specs=pl.BlockSpec(memory_space=pltpu.SMEM),
        name="invert_positions",
    )(pos_flat, fill)


def _expert_kernel(te_ref, tv_ref, tf_ref, tn_ref, src_ref, h_hbm, wg_hbm, wu_hbm, wd_hbm, y_ref,
                   xbuf0, xbuf1, sg, su, sd, bg, bu, bd, gsem, wsem, *, layer):
    i = pl.program_id(0)
    xbufs = (xbuf0, xbuf1)
    tm = xbuf0.shape[0]

    def gather(tile, buf):
        for r in range(tm):
            tok = src_ref[tile * tm + r]
            pltpu.make_async_copy(h_hbm.at[pl.ds(tok, 1)], xbufs[buf].at[pl.ds(r, 1)], gsem.at[buf]).start()

    def fetch(e):
        return (pltpu.make_async_copy(wg_hbm.at[layer, e], sg, wsem.at[0]),
                pltpu.make_async_copy(wu_hbm.at[layer, e], su, wsem.at[1]),
                pltpu.make_async_copy(wd_hbm.at[layer, e], sd, wsem.at[2]))

    @pl.when(i == 0)
    def _():
        for c in fetch(te_ref[0]):
            c.start(priority=WEIGHT_DMA_PRIORITY)
        gather(0, 0)

    pending = (i == 0) | (tv_ref[jnp.maximum(i - 1, 0)] > 0)
    for par in range(2):
        @pl.when(pending & (i % 2 == par))
        def _(par=par):
            pltpu.make_async_copy(h_hbm.at[pl.ds(0, tm)], xbufs[par], gsem.at[par]).wait()

    @pl.when(tf_ref[i] > 0)
    def _():
        for c in fetch(te_ref[i]):
            c.wait()
        bg[...] = sg[...].astype(BF16)
        bu[...] = su[...].astype(BF16)
        bd[...] = sd[...].astype(BF16)

        @pl.when(tn_ref[i] >= 0)
        def _():
            for c in fetch(tn_ref[i]):
                c.start(priority=WEIGHT_DMA_PRIORITY)

    for par in range(2):
        @pl.when((tv_ref[i] > 0) & (i % 2 == par))
        def _(par=par):
            gather(i + 1, 1 - par)
            x = xbufs[par][...].astype(BF16)
            a = jnp.dot(x, bg[...], preferred_element_type=F32)
            u = jnp.dot(x, bu[...], preferred_element_type=F32)
            hid = (a * jax.nn.sigmoid(a) * u).astype(BF16)
            y_ref[...] = jnp.dot(hid, bd[...], preferred_element_type=F32)

    @pl.when(tv_ref[i] == 0)
    def _():
        y_ref[...] = jnp.zeros_like(y_ref)


def _experts(tile_expert, tile_valid, tile_first, tile_next, src, h, w_gate, w_up, w_down, layer, n_rows):
    d = h.shape[1]
    f = w_gate.shape[3]
    tm = EXPERT_TILE
    grid_spec = pltpu.PrefetchScalarGridSpec(
        num_scalar_prefetch=5,
        grid=(n_rows // tm,),
        in_specs=[pl.BlockSpec(memory_space=pl.ANY)] * 4,
        out_specs=pl.BlockSpec((tm, d), lambda i, *_: (i, 0)),
        scratch_shapes=[pltpu.VMEM((tm, d), F32), pltpu.VMEM((tm, d), F32),
                        pltpu.VMEM((d, f), F32), pltpu.VMEM((d, f), F32), pltpu.VMEM((f, d), F32),
                        pltpu.VMEM((d, f), BF16), pltpu.VMEM((d, f), BF16), pltpu.VMEM((f, d), BF16),
                        pltpu.SemaphoreType.DMA((2,)), pltpu.SemaphoreType.DMA((3,))],
    )
    return pl.pallas_call(
        functools.partial(_expert_kernel, layer=layer),
        out_shape=jax.ShapeDtypeStruct((n_rows, d), F32),
        grid_spec=grid_spec,
        compiler_params=_params(("arbitrary",)),
        name="experts",
    )(tile_expert, tile_valid, tile_first, tile_next, src, h, w_gate, w_up, w_down)


def _combine_kernel(pos_ref, x_ref, w_ref, g_ref, y_ref, *rest, with_next):
    if with_next:
        ng_ref, nsh_ref, nsc_ref, o_ref, hn_ref, buf0, buf1, sem = rest
    else:
        o_ref, buf0, buf1, sem = rest
    i = pl.program_id(0)
    n = pl.num_programs(0)
    tm = x_ref.shape[0]
    t = n * tm
    bufs = (buf0, buf1)

    def gather(tile, par):
        for r in range(tm):
            for slot in range(TOP_K):
                p = pos_ref[slot * t + tile * tm + r]
                pltpu.make_async_copy(y_ref.at[pl.ds(p, 1)], bufs[par].at[slot, pl.ds(r, 1)],
                                      sem.at[par]).start(priority=slot)

    def combine(par):
        w = w_ref[...]
        y = w[:, 0:1] * bufs[par][0] + w[:, 1:2] * bufs[par][1]
        x_new = x_ref[...] + g_ref[0] * y
        o_ref[...] = x_new
        if with_next:
            hn_ref[...] = _norm_mod(x_new, ng_ref[...], nsh_ref[0], nsc_ref[0]).astype(BF16)

    @pl.when(i == 0)
    def _():
        gather(0, 0)

    for par in range(2):
        @pl.when(i % 2 == par)
        def _(par=par):
            for slot in range(TOP_K):
                pltpu.make_async_copy(y_ref.at[pl.ds(0, tm)], bufs[par].at[slot], sem.at[par]).wait()

        @pl.when((i % 2 == par) & (i + 1 < n))
        def _(par=par):
            gather(i + 1, 1 - par)
            combine(par)

        @pl.when((i % 2 == par) & (i + 1 == n))
        def _(par=par):
            combine(par)


def _combine(pos_flat, x, wsel_t, gate, y, seq, next_norm=None):
    t, d = x.shape
    tm = ROW_TILE
    assert seq % tm == 0 and t % tm == 0
    row = pl.BlockSpec((tm, d), lambda i, p: (i, 0))
    per_batch = pl.BlockSpec((1, 1, d), lambda i, p: (i * tm // seq, 0, 0))
    in_specs = [row, pl.BlockSpec((tm, TOP_K), lambda i, p: (i, 0)), per_batch, pl.BlockSpec(memory_space=pl.ANY)]
    operands = [pos_flat, x, wsel_t, gate, y]
    out_shape = jax.ShapeDtypeStruct((t, d), F32)
    out_specs = row
    if next_norm is not None:
        in_specs += [pl.BlockSpec((1, d), lambda i, p: (0, 0)), per_batch, per_batch]
        operands += [next_norm[0].reshape(1, d), next_norm[1], next_norm[2]]
        out_shape = (out_shape, jax.ShapeDtypeStruct((t, d), BF16))
        out_specs = (row, row)
    grid_spec = pltpu.PrefetchScalarGridSpec(
        num_scalar_prefetch=1,
        grid=(t // tm,),
        in_specs=in_specs,
        out_specs=out_specs,
        scratch_shapes=[pltpu.VMEM((TOP_K, tm, d), F32), pltpu.VMEM((TOP_K, tm, d), F32),
                        pltpu.SemaphoreType.DMA((2,))],
    )
    return pl.pallas_call(
        functools.partial(_combine_kernel, with_next=next_norm is not None),
        out_shape=out_shape,
        grid_spec=grid_spec,
        compiler_params=_params(("arbitrary",)),
        name="combine",
    )(*operands)


def _moe(x, g, shift, scale, gate, rw_t, rbias, w_gate, w_up, w_down, layer, seq, next_norm):
    t, d = x.shape
    te = EXPERT_TILE
    n_rows = TOP_K * t + N_EXPERTS * te
    n_tiles = n_rows // te
    h, eidx, wsel, rank, cnt = _router(x, g, shift, scale, rw_t, rbias, seq)
    counts = cnt[:, 0]
    padded = ((counts + te - 1) // te) * te
    ends = jnp.cumsum(padded)
    starts = ends - padded
    pos = rank
    for e in range(N_EXPERTS):
        pos = pos + jnp.where(eidx == e, starts[e], 0)
    pos_flat = pos.reshape(-1)
    tile_start = jnp.arange(n_tiles, dtype=jnp.int32) * te
    tile_expert = jnp.minimum(jnp.sum(tile_start[:, None] >= ends[None, :], axis=1), N_EXPERTS - 1).astype(jnp.int32)
    tile_valid = (tile_start < ends[-1]).astype(jnp.int32)
    tile_expert = jnp.where(tile_valid > 0, tile_expert, tile_expert[jnp.maximum(ends[-1] // te - 1, 0)])
    prev_expert = jnp.concatenate([jnp.full((1,), -1, jnp.int32), tile_expert[:-1]])
    tile_first = ((tile_valid > 0) & (tile_expert != prev_expert)).astype(jnp.int32)
    later = (tile_expert[None, :] > tile_expert[:, None]) & (tile_valid[None, :] > 0)
    tile_next = jnp.min(jnp.where(later, tile_expert[None, :], N_EXPERTS), axis=1)
    tile_next = jnp.where(tile_next < N_EXPERTS, tile_next, -1).astype(jnp.int32)
    fill = jnp.stack([jnp.concatenate([starts + counts, ends[-1:]]),
                      jnp.concatenate([ends, jnp.full((1,), n_rows, ends.dtype)])], axis=1).reshape(-1).astype(jnp.int32)
    src = _invert_positions(pos_flat, fill, n_rows, t)
    y = _experts(tile_expert, tile_valid, tile_first, tile_next, src, h, w_gate, w_up, w_down, layer, n_rows)
    return _combine(pos_flat, x, wsel.T, gate, y, seq, next_norm)


def kernel(x, c, positions, ada_w, ada_b, norm1_g, norm2_g, ab_w_in, q_norm_g, k_norm_g, sinks, gm_vnorm_g, gm_ws, gm_b, ab_w_out, c_w_in, c_lower_bounds, c_onorm_g, c_w_out, router_w, router_bias, moe_w_gate, moe_w_up, moe_w_down):
    bsz, seq, d = x.shape
    depth = ada_w.shape[0]
    t = bsz * seq
    xt = x.reshape(t, d)

    c_pad = jnp.concatenate([c, jnp.zeros((8 - bsz % 8, d), F32)], axis=0) if bsz % 8 else c
    mod = _ada_mod(c_pad, ada_w, ada_b)[:, :bsz]
    mod = mod.reshape(depth, bsz, 6, 1, d)
    cos_t, sin_t = _rope_tables(positions)

    sm = jax.nn.softmax(c_lower_bounds.astype(F32), axis=0)
    lower_bounds = jnp.cumsum(sm, axis=0) - sm[0:1]
    rw_t = router_w.T
    ab_in, ab_out, c_in, c_out = (w.astype(BF16) for w in (ab_w_in, ab_w_out, c_w_in, c_w_out))

    h1 = None
    for l in range(depth):
        sh1, sc1, g1, sh2, sc2, g2 = (mod[l, :, k] for k in range(6))
        w_in, idx = (ab_in, l // 2) if l % 2 == 0 else (c_in, l // 2)
        proj = _in_proj(xt, norm1_g[l], sh1, sc1, w_in, idx, seq) if h1 is None else _in_proj_normed(h1, w_in, idx)
        if l % 2 == 0:
            i = l // 2
            mix = _mix_even(proj, cos_t, sin_t, q_norm_g[i], k_norm_g[i], sinks[i],
                            gm_vnorm_g[i], gm_ws[i], gm_b[i], bsz, seq)
            xt = _out_proj(mix, ab_out, i, xt, g1, seq)
        else:
            j = l // 2
            o = _hgrn(proj, lower_bounds[l], c_onorm_g[j], bsz, seq)
            xt = _out_proj(o, c_out, j, xt, g1, seq)
        next_norm = (norm1_g[l + 1], mod[l + 1, :, 0], mod[l + 1, :, 1]) if l + 1 < depth else None
        out = _moe(xt, norm2_g[l], sh2, sc2, g2, rw_t, router_bias,
                   moe_w_gate, moe_w_up, moe_w_down, l, seq, next_norm)
        xt, h1 = out if next_norm is not None else (out, None)
    return xt.reshape(bsz, seq, d)
```
